```python
import math
import jax, jax.numpy as jnp
from jax import lax
import numpy as np

D_MODEL = 1024
BATCH = 4
SEQ = 4096
DEPTH = 1

N_META = 16
BLOCK_Q = 128
A_HEADS = 8
A_HEAD_DIM = 64
A_WIDTH = A_HEADS * A_HEAD_DIM
IDX_HEADS = 8
IDX_DIM = 64
TOPK_MAX = 256
B_HEADS = 8
B_NOPE = 64
B_ROPE = 32
B_V = 64
B_WIDTH = B_HEADS * B_V
Q_LORA = 256
KV_LORA = 128
ROPE_THETA = 10000.0
REL_BUCKETS = 32
REL_MAX_DIST = 128
D_MIX = A_WIDTH + B_WIDTH
IN_SPLITS = (
    A_WIDTH,
    A_WIDTH,
    A_WIDTH,
    A_WIDTH,
    IDX_HEADS * IDX_DIM,
    IDX_DIM,
    IDX_HEADS,
    Q_LORA,
    KV_LORA,
    B_ROPE,
    B_WIDTH,
)
D_IN = sum(IN_SPLITS)
IN_OFFSETS = tuple(int(v) for v in np.cumsum(IN_SPLITS)[:-1])
ALPHA = (2.0 * DEPTH) ** 0.25
BETA = (8.0 * DEPTH) ** -0.25
LN_EPS = 1e-5
RMS_EPS = 1e-6
NEG = float(np.finfo(np.float32).min)

kernel_name = "hymba_dsa_mla_deepnorm"


def layer_norm(x, g, b):
    x32 = x.astype(jnp.float32)
    mu = jnp.mean(x32, axis=-1, keepdims=True)
    var = jnp.mean(jnp.square(x32 - mu), axis=-1, keepdims=True)
    y = (x32 - mu) * lax.rsqrt(var + LN_EPS)
    return (y * g.astype(jnp.float32) + b.astype(jnp.float32)).astype(x.dtype)


def rms_norm(x, g):
    x32 = x.astype(jnp.float32)
    y = x32 * lax.rsqrt(jnp.mean(jnp.square(x32), axis=-1, keepdims=True) + RMS_EPS)
    return (y * g.astype(jnp.float32)).astype(x.dtype)


def apply_rope(x, cos, sin):
    x32 = x.astype(jnp.float32)
    half = x.shape[-1] // 2
    x1, x2 = x32[..., :half], x32[..., half:]
    return jnp.concatenate([x1 * cos - x2 * sin, x2 * cos + x1 * sin], axis=-1).astype(x.dtype)


def t5_bucket(dist):
    max_exact = REL_BUCKETS // 2
    d = jnp.maximum(dist, 0)
    d_f = jnp.maximum(d, 1).astype(jnp.float32)
    large = max_exact + (jnp.log(d_f / max_exact) / math.log(REL_MAX_DIST / max_exact)
                         * (REL_BUCKETS - max_exact)).astype(jnp.int32)
    large = jnp.minimum(large, REL_BUCKETS - 1)
    return jnp.where(d < max_exact, d, large)


def hybrid_layer(h, w_in, w_uq, q_norm_g, w_ukv, kv_norm_g, rel_bias, w_out, ln_g, ln_b,
                 cos, sin, topk, n_blk):
    B, Lp, _ = h.shape
    proj = h @ w_in
    (qa, ka, va, ga, qi, ki, wi, cq, ckv, kpe, gb) = jnp.split(proj, IN_OFFSETS, axis=-1)

    qa = qa.reshape(B, Lp, A_HEADS, A_HEAD_DIM)
    ka = ka.reshape(B, Lp, A_HEADS, A_HEAD_DIM)
    va = va.reshape(B, Lp, A_HEADS, A_HEAD_DIM)
    qi = qi.reshape(B, Lp, IDX_HEADS, IDX_DIM)
    wi = wi * (IDX_HEADS ** -0.5)

    qb = (rms_norm(cq, q_norm_g) @ w_uq).reshape(B, Lp, B_HEADS, B_NOPE + B_ROPE)
    q_nope, q_pe = qb[..., :B_NOPE], qb[..., B_NOPE:]
    q_pe = apply_rope(q_pe, cos[:, None, :], sin[:, None, :])
    kvb = (rms_norm(ckv, kv_norm_g) @ w_ukv).reshape(B, Lp, B_HEADS, B_NOPE + B_V)
    k_nope, vb = kvb[..., :B_NOPE], kvb[..., B_NOPE:]
    k_pe = apply_rope(kpe, cos, sin)

    scale_a = A_HEAD_DIM ** -0.5
    scale_i = IDX_DIM ** -0.5
    scale_b = (B_NOPE + B_ROPE) ** -0.5
    s_pos = jnp.arange(Lp, dtype=jnp.int32)
    bidx = jnp.arange(B)[:, None, None]

    def block(i):
        q0 = i * BLOCK_Q
        t_pos = q0 + jnp.arange(BLOCK_Q, dtype=jnp.int32)
        causal = s_pos[None, :] <= t_pos[:, None]
        sl = lambda a: lax.dynamic_slice_in_dim(a, q0, BLOCK_Q, axis=1)

        rel = jnp.einsum('bqhd,bkd->bqhk', sl(qi), ki,
                         preferred_element_type=jnp.float32) * scale_i
        idx_score = jnp.einsum('bqhk,bqh->bqk', jax.nn.relu(rel),
                               sl(wi).astype(jnp.float32))
        idx_score = jnp.where(s_pos[None, None, :] < N_META, jnp.inf, idx_score)
        idx_score = jnp.where(causal[None], idx_score, -jnp.inf)
        _, sel = lax.top_k(idx_score, topk)
        valid = sel <= t_pos[None, :, None]
        k_sel = ka[bidx, sel]
        v_sel = va[bidx, sel]
        la = jnp.einsum('bqhd,bqkhd->bqhk', sl(qa), k_sel,
                        preferred_element_type=jnp.float32) * scale_a
        bias = rel_bias[t5_bucket(t_pos[None, :, None] - sel)].astype(jnp.float32)
        la = la + jnp.transpose(bias, (0, 1, 3, 2))
        la = jnp.where(valid[:, :, None, :], la, NEG)
        pa = jax.nn.softmax(la, axis=-1).astype(v_sel.dtype)
        oa = jnp.einsum('bqhk,bqkhd->bqhd', pa, v_sel).reshape(B, BLOCK_Q, A_WIDTH)

        lb = (jnp.einsum('bqhd,bkhd->bhqk', sl(q_nope), k_nope, preferred_element_type=jnp.float32)
              + jnp.einsum('bqhr,bkr->bhqk', sl(q_pe), k_pe, preferred_element_type=jnp.float32)) * scale_b
        lb = jnp.where(causal[None, None], lb, NEG)
        pb = jax.nn.softmax(lb, axis=-1).astype(vb.dtype)
        ob = jnp.einsum('bhqk,bkhd->bqhd', pb, vb).reshape(B, BLOCK_Q, B_WIDTH)
        return oa, ob

    oa, ob = lax.map(block, jnp.arange(n_blk, dtype=jnp.int32))
    oa = jnp.transpose(oa, (1, 0, 2, 3)).reshape(B, Lp, A_WIDTH)
    ob = jnp.transpose(ob, (1, 0, 2, 3)).reshape(B, Lp, B_WIDTH)

    mixed = jnp.concatenate([oa * jax.nn.silu(ga), ob * jax.nn.silu(gb)], axis=-1)
    out = mixed @ w_out
    return layer_norm(ALPHA * h + out, ln_g, ln_b)


def setup_inputs(seed: int = 0) -> dict:
    key = jax.random.key(seed)
    ks = jax.random.split(key, 16)
    f32 = jnp.float32
    nrm = lambda k, shape: jax.random.normal(k, shape, dtype=f32)
    return {
        "x": nrm(ks[0], (BATCH, SEQ, D_MODEL)),
        "meta_tokens": nrm(ks[1], (N_META, D_MODEL)),
        "ln_emb_g": 1.0 + 0.02 * nrm(ks[2], (D_MODEL,)),
        "ln_emb_b": 0.02 * nrm(ks[3], (D_MODEL,)),
        "w_in": nrm(ks[4], (DEPTH, D_MODEL, D_IN)) * D_MODEL ** -0.5,
        "w_uq": nrm(ks[5], (DEPTH, Q_LORA, B_HEADS * (B_NOPE + B_ROPE))) * Q_LORA ** -0.5,
        "q_norm_g": 1.0 + 0.02 * nrm(ks[6], (DEPTH, Q_LORA)),
        "w_ukv": nrm(ks[7], (DEPTH, KV_LORA, B_HEADS * (B_NOPE + B_V))) * KV_LORA ** -0.5,
        "kv_norm_g": 1.0 + 0.02 * nrm(ks[8], (DEPTH, KV_LORA)),
        "rel_bias": 0.2 * nrm(ks[9], (REL_BUCKETS, A_HEADS)),
        "w_out": nrm(ks[10], (DEPTH, D_MIX, D_MODEL)) * (D_MIX ** -0.5) * BETA,
        "ln_post_g": 1.0 + 0.02 * nrm(ks[11], (DEPTH, D_MODEL)),
        "ln_post_b": 0.02 * nrm(ks[12], (DEPTH, D_MODEL)),
    }


def reference(x, meta_tokens, ln_emb_g, ln_emb_b, w_in, w_uq, q_norm_g, w_ukv, kv_norm_g,
              rel_bias, w_out, ln_post_g, ln_post_b):
    B, S, D = x.shape
    L = N_META + S
    n_blk = -(-L // BLOCK_Q)
    Lp = n_blk * BLOCK_Q
    topk = min(TOPK_MAX, S // 4)

    meta = jnp.broadcast_to(meta_tokens[None].astype(x.dtype), (B, N_META, D))
    h = jnp.concatenate([meta, x], axis=1)
    h = layer_norm(h, ln_emb_g, ln_emb_b)
    h = jnp.pad(h, ((0, 0), (0, Lp - L), (0, 0)))

    pos = jnp.arange(Lp, dtype=jnp.float32)
    inv_freq = ROPE_THETA ** (-jnp.arange(0, B_ROPE, 2, dtype=jnp.float32) / B_ROPE)
    ang = pos[:, None] * inv_freq[None, :]
    cos, sin = jnp.cos(ang), jnp.sin(ang)

    for l in range(DEPTH):
        h = hybrid_layer(h, w_in[l], w_uq[l], q_norm_g[l], w_ukv[l], kv_norm_g[l], rel_bias,
                         w_out[l], ln_post_g[l], ln_post_b[l], cos, sin, topk, n_blk)
    return h[:, N_META:N_META + S, :]
```

```python
import functools
import math

import jax
import jax.numpy as jnp
from jax import lax
from jax.experimental import pallas as pl
from jax.experimental.pallas import tpu as pltpu

F32 = jnp.float32
BF16 = jnp.bfloat16
I32 = jnp.int32

N_META = 16
BLOCK_Q = 128
HEADS = 8
A_HEAD_DIM = 64
A_WIDTH = HEADS * A_HEAD_DIM
IDX_DIM = 64
TOPK_MAX = 256
B_NOPE = 64
B_ROPE = 32
B_V = 64
B_WIDTH = HEADS * B_V
Q_LORA = 256
KV_LORA = 128
ROPE_THETA = 10000.0
REL_BUCKETS = 32
REL_MAX_DIST = 128
LN_EPS = 1e-5
RMS_EPS = 1e-6
DEPTH = 1
ALPHA = (2.0 * DEPTH) ** 0.25

LANES = 128
BLOCKS_PER_CHUNK = 3
CHUNK = BLOCKS_PER_CHUNK * BLOCK_Q
MASKED = -1e30
INT_MIN = -2147483648
KEY_NEG_INF = -0x7F800000
VMEM_LIMIT = 56 * 1024 * 1024

_SPLITS = (A_WIDTH, A_WIDTH, A_WIDTH, A_WIDTH, HEADS * IDX_DIM, IDX_DIM, HEADS,
           Q_LORA, KV_LORA, B_ROPE, B_WIDTH)
_OFF = [0]
for _s in _SPLITS:
    _OFF.append(_OFF[-1] + _s)
(O_QA, O_KA, O_VA, O_GA, O_QI, O_KI, O_WI, O_CQ, O_CKV, O_KPE, O_GB, O_END) = _OFF

NN_KA = 0
NN_G = NN_KA + A_WIDTH
NN_CQ = NN_G + A_WIDTH + B_WIDTH
NN_CKV = NN_CQ + Q_LORA
NN_PE = NN_CKV + KV_LORA
NN_PER = NN_PE + LANES
NN_KI = NN_PER + LANES
NN_END = NN_KI + IDX_DIM
NT_QA = 0
NT_VA = NT_QA + HEADS * LANES
NT_QI = NT_VA + A_WIDTH
NT_WI = NT_QI + HEADS * IDX_DIM
NT_END = NT_WI + 16

_NT_DIMS = (((1,), (1,)), ((), ()))


def _dot_nt(a, b):
    return lax.dot_general(a, b, _NT_DIMS, preferred_element_type=F32)


def _layer_norm(x, g, b):
    mu = jnp.mean(x, axis=-1, keepdims=True)
    xc = x - mu
    var = jnp.mean(xc * xc, axis=-1, keepdims=True)
    return xc * lax.rsqrt(var + LN_EPS) * g + b


def _rms_norm(x, g):
    return x * lax.rsqrt(jnp.mean(x * x, axis=-1, keepdims=True) + RMS_EPS) * g


def _proj_kernel(x_ref, lng_ref, lnb_ref, wnn_ref, wnt_ref, wuq_ref, wuqr_ref, wk_ref, wv_ref,
                 qg_ref, kvg_ref, cos_ref, sin_ref, ct_ref, st_ref,
                 ka_ref, ki_ref, g_ref, kb_ref, qat_ref, qit_ref, wit_ref, vat_ref, qbt_ref,
                 vbt_ref, *, seq_len):
    c = pl.program_id(1)
    y = _layer_norm(x_ref[0], lng_ref[...], lnb_ref[...])
    pos = c * CHUNK + lax.broadcasted_iota(I32, (CHUNK, 1), 0)
    y = jnp.where(pos < seq_len, y, 0.0)
    hb = y.astype(BF16)

    def nn(lo, hi):
        return jnp.dot(hb, wnn_ref[:, lo:hi], preferred_element_type=F32)

    def nt(lo, hi):
        return _dot_nt(wnt_ref[lo:hi, :], hb)

    ka_ref[0] = nn(NN_KA, NN_G).astype(BF16)
    g_ref[0] = nn(NN_G, NN_CQ)
    ki_ref[0] = nn(NN_KI, NN_END).astype(BF16)

    qat = nt(NT_QA, NT_VA).astype(BF16)
    for s in range(BLOCKS_PER_CHUNK):
        qat_ref[0, s] = qat[:, s * BLOCK_Q:(s + 1) * BLOCK_Q]
    vat_ref[0, 0] = nt(NT_VA, NT_QI).astype(BF16)
    qit = nt(NT_QI, NT_WI).astype(BF16)
    for s in range(BLOCKS_PER_CHUNK):
        for h in range(HEADS):
            qit_ref[0, s, :, h * LANES:(h + 1) * LANES] = (
                qit[h * IDX_DIM:(h + 1) * IDX_DIM, s * BLOCK_Q:(s + 1) * BLOCK_Q])
    wit = nt(NT_WI, NT_END) * (HEADS ** -0.5)
    wit = wit * (IDX_DIM ** -0.5)
    for s in range(BLOCKS_PER_CHUNK):
        wit_ref[0, s] = wit[0:HEADS, s * BLOCK_Q:(s + 1) * BLOCK_Q]

    cqn = _rms_norm(nn(NN_CQ, NN_CKV), qg_ref[...]).astype(BF16)
    ckvn = _rms_norm(nn(NN_CKV, NN_PE), kvg_ref[...]).astype(BF16)
    qbt = _dot_nt(wuq_ref[...], cqn)
    qbrt = _dot_nt(wuqr_ref[...], cqn)
    ct = ct_ref[...]
    st = st_ref[...]
    for h in range(HEADS):
        rows = slice(h * LANES, (h + 1) * LANES)
        qh = (qbt[rows] * ct + qbrt[rows] * st).astype(BF16)
        for s in range(BLOCKS_PER_CHUNK):
            qbt_ref[0, s, rows, :] = qh[:, s * BLOCK_Q:(s + 1) * BLOCK_Q]
    kn = jnp.dot(ckvn, wk_ref[...], preferred_element_type=F32)
    kpe = nn(NN_PE, NN_PER) * cos_ref[...] + nn(NN_PER, NN_KI) * sin_ref[...]
    for h in range(HEADS):
        cols = slice(h * LANES, (h + 1) * LANES)
        kb_ref[0, :, cols] = (kn[:, cols] + kpe).astype(BF16)
    vbt_ref[0, 0] = _dot_nt(wv_ref[...], ckvn).astype(BF16)


def _t5_bucket(d):
    max_exact = REL_BUCKETS // 2
    d = jnp.maximum(d, 0)
    d_f = jnp.maximum(d, 1).astype(F32)
    large = max_exact + (jnp.log(d_f / max_exact) / math.log(REL_MAX_DIST / max_exact)
                         * (REL_BUCKETS - max_exact)).astype(I32)
    large = jnp.minimum(large, REL_BUCKETS - 1)
    return jnp.where(d < max_exact, d, large)


def _dsa_kernel(relb_ref, ka_ref, ki_ref, vat_ref, qat_ref, qit_ref, wit_ref, oa_ref,
                keys_s, bias_s, acc_s, j_s, *, topk, lp):
    b = pl.program_id(0)
    i = pl.program_id(1)
    nch = i // BLOCKS_PER_CHUNK + 1
    row = lax.broadcasted_iota(I32, (CHUNK, BLOCK_Q), 0)
    pos_q = i * BLOCK_Q + lax.broadcasted_iota(I32, (CHUNK, BLOCK_Q), 1)

    @pl.when((b == 0) & (i == 0))
    def _():
        r = lax.broadcasted_iota(I32, (BLOCK_Q, BLOCK_Q), 0)
        q = lax.broadcasted_iota(I32, (BLOCK_Q, BLOCK_Q), 1)
        for dl in range(2):
            bucket = _t5_bucket(dl * BLOCK_Q + q - r)
            for h in range(HEADS):
                t = jnp.zeros((BLOCK_Q, BLOCK_Q), F32)
                for bk in range(REL_BUCKETS):
                    t = jnp.where(bucket == bk, relb_ref[bk, h], t)
                bias_s[h, dl] = t - relb_ref[REL_BUCKETS - 1, h]

    def idx_body(c, carry):
        k0 = pl.multiple_of(c * CHUNK, CHUNK)
        rel = jnp.dot(ki_ref[0, pl.ds(k0, CHUNK), :], qit_ref[0, 0],
                      preferred_element_type=F32)
        s = jnp.zeros((CHUNK, BLOCK_Q), F32)
        for h in range(HEADS):
            s = s + jnp.maximum(rel[:, h * LANES:(h + 1) * LANES], 0.0) * wit_ref[0, 0, h:h + 1, :]
        pos_k = k0 + row
        s = jnp.where(pos_k < N_META, jnp.inf, s)
        s = jnp.where(pos_k <= pos_q, s, -jnp.inf)
        bits = lax.bitcast_convert_type(s, I32)
        keys_s[pl.ds(k0, CHUNK), :] = jnp.where(bits < 0, INT_MIN - bits, bits)
        return carry

    lax.fori_loop(0, nch, idx_body, 0)

    def count(pred):
        def body(c, acc):
            k0 = pl.multiple_of(c * CHUNK, CHUNK)
            hit = pred(keys_s[pl.ds(k0, CHUNK), :], k0 + row)
            return acc + jnp.sum(jnp.where(hit, 1, 0).reshape(CHUNK // 8, 8, BLOCK_Q), axis=0)
        acc = lax.fori_loop(0, nch, body, jnp.zeros((8, BLOCK_Q), I32))
        return jnp.sum(acc, axis=0, keepdims=True)

    def bit_body(it, carry):
        t, cnt_t = carry
        cand = t | jnp.left_shift(jnp.int32(1), 31 - it)
        thr_c = cand ^ INT_MIN
        cnt = count(lambda k, p: k >= thr_c)
        ok = cnt >= topk
        return jnp.where(ok, cand, t), jnp.where(ok, cnt, cnt_t)

    t0 = jnp.zeros((1, BLOCK_Q), I32)
    t, cnt_t = lax.fori_loop(0, 32, bit_body, (t0, jnp.full((1, BLOCK_Q), lp, I32)))
    thr = t ^ INT_MIN

    j_s[...] = jnp.full((1, BLOCK_Q), lp, I32)
    tie_cut = (cnt_t > topk) & (thr > KEY_NEG_INF)
    any_cut = jnp.max(jnp.where(tie_cut, 1, 0))

    @pl.when(any_cut > 0)
    def _():
        need = topk - count(lambda k, p: k > thr)

        def jbit(it, j):
            cand = j | jnp.left_shift(jnp.int32(1), 12 - it)
            cnt = count(lambda k, p: (k == thr) & (p < cand))
            return jnp.where(cnt < need, cand, j)

        j = lax.fori_loop(0, 13, jbit, jnp.zeros((1, BLOCK_Q), I32))
        j_s[...] = jnp.where(tie_cut, j, lp)

    j_last = j_s[...]

    acc_s[...] = jnp.zeros_like(acc_s)

    def att_body(c, carry, near):
        ms, ls = carry
        k0 = pl.multiple_of(c * CHUNK, CHUNK)
        key = keys_s[pl.ds(k0, CHUNK), :]
        pos_k = k0 + row
        sel = (key > thr) | ((key == thr) & (pos_k <= j_last))
        sel = sel & (pos_k <= pos_q)
        madd = jnp.where(sel, 0.0, MASKED)
        new_ms, new_ls = [], []
        for h in range(HEADS):
            pair = slice((h // 2) * LANES, (h // 2 + 1) * LANES)
            s = jnp.dot(ka_ref[0, pl.ds(k0, CHUNK), pair], qat_ref[0, 0, h * LANES:(h + 1) * LANES, :],
                        preferred_element_type=F32) + madd
            if near:
                parts = []
                for sub in range(BLOCKS_PER_CHUNK):
                    dl = i - (c * BLOCKS_PER_CHUNK + sub)
                    parts.append(jnp.where(dl == 0, bias_s[h, 0],
                                           jnp.where(dl == 1, bias_s[h, 1], 0.0)))
                s = s + jnp.concatenate(parts, axis=0)
            m_new = jnp.maximum(ms[h], jnp.max(s, axis=0, keepdims=True))
            alpha = jnp.exp(ms[h] - m_new)
            p = jnp.exp(s - m_new)
            new_ls.append(alpha * ls[h] + jnp.sum(p, axis=0, keepdims=True))
            new_ms.append(m_new)
            rows = slice(h * A_HEAD_DIM, (h + 1) * A_HEAD_DIM)
            pv = jnp.dot(vat_ref[0, c, rows, :], p.astype(BF16), preferred_element_type=F32)
            acc_s[rows, :] = alpha * acc_s[rows, :] + pv
        return tuple(new_ms), tuple(new_ls)

    init = (tuple(jnp.full((1, BLOCK_Q), MASKED, F32) for _ in range(HEADS)),
            tuple(jnp.zeros((1, BLOCK_Q), F32) for _ in range(HEADS)))
    n_far = jnp.maximum(nch - 2, 0)
    carry = lax.fori_loop(0, n_far, functools.partial(att_body, near=False), init)
    ms, ls = lax.fori_loop(n_far, nch, functools.partial(att_body, near=True), carry)

    outs = []
    for h in range(HEADS):
        rows = slice(h * A_HEAD_DIM, (h + 1) * A_HEAD_DIM)
        outs.append(acc_s[rows, :] * (1.0 / ls[h]))
    oa_ref[0] = jnp.concatenate(outs, axis=0).T


def _mla_kernel(kb_ref, vbt_ref, qbt_ref, ob_ref, acc_s):
    i = pl.program_id(1)
    nch = i // BLOCKS_PER_CHUNK + 1
    row = lax.broadcasted_iota(I32, (CHUNK, BLOCK_Q), 0)
    pos_q = i * BLOCK_Q + lax.broadcasted_iota(I32, (CHUNK, BLOCK_Q), 1)
    acc_s[...] = jnp.zeros_like(acc_s)

    def body(c, carry, diagonal):
        ms, ls = carry
        k0 = pl.multiple_of(c * CHUNK, CHUNK)
        new_ms, new_ls = [], []
        for h in range(HEADS):
            grp = slice(h * LANES, (h + 1) * LANES)
            s = jnp.dot(kb_ref[0, pl.ds(k0, CHUNK), grp], qbt_ref[0, 0, grp, :],
                        preferred_element_type=F32)
            if diagonal:
                s = jnp.where(k0 + row <= pos_q, s, MASKED)
            m_new = jnp.maximum(ms[h], jnp.max(s, axis=0, keepdims=True))
            alpha = jnp.exp(ms[h] - m_new)
            p = jnp.exp(s - m_new)
            new_ls.append(alpha * ls[h] + jnp.sum(p, axis=0, keepdims=True))
            new_ms.append(m_new)
            rows = slice(h * B_V, (h + 1) * B_V)
            pv = jnp.dot(vbt_ref[0, c, rows, :], p.astype(BF16), preferred_element_type=F32)
            acc_s[rows, :] = alpha * acc_s[rows, :] + pv
        return tuple(new_ms), tuple(new_ls)

    init = (tuple(jnp.full((1, BLOCK_Q), MASKED, F32) for _ in range(HEADS)),
            tuple(jnp.zeros((1, BLOCK_Q), F32) for _ in range(HEADS)))
    carry = lax.fori_loop(0, nch - 1, functools.partial(body, diagonal=False), init)
    ms, ls = body(nch - 1, carry, diagonal=True)

    outs = []
    for h in range(HEADS):
        rows = slice(h * B_V, (h + 1) * B_V)
        outs.append(acc_s[rows, :] * (1.0 / ls[h]))
    ob_ref[0] = jnp.concatenate(outs, axis=0).T


def _silu(x):
    return x * (1.0 / (1.0 + jnp.exp(-x)))


def _out_kernel(x_ref, lng_ref, lnb_ref, oa_ref, ob_ref, g_ref, wout_ref, pg_ref, pb_ref, out_ref):
    h = _layer_norm(x_ref[0], lng_ref[...], lnb_ref[...])
    g = g_ref[0]
    mixed = jnp.concatenate([oa_ref[0] * _silu(g[:, :A_WIDTH]), ob_ref[0] * _silu(g[:, A_WIDTH:])],
                            axis=-1)
    out = jnp.dot(mixed.astype(BF16), wout_ref[...], preferred_element_type=F32)
    out_ref[0] = _layer_norm(ALPHA * h + out, pg_ref[...], pb_ref[...])


def _rot_cols(w):
    half = w.shape[-1] // 2
    return jnp.concatenate([-w[:, half:], w[:, :half]], axis=-1)


def _full(shape):
    return pl.BlockSpec(shape, lambda *_: (0,) * len(shape))


def kernel(x, meta_tokens, ln_emb_g, ln_emb_b, w_in, w_uq, q_norm_g, w_ukv, kv_norm_g, rel_bias,
           w_out, ln_post_g, ln_post_b):
    B, S, D = x.shape
    assert w_in.shape[0] == DEPTH == 1
    L = N_META + S
    n_blk = -(-L // BLOCK_Q)
    assert n_blk % BLOCKS_PER_CHUNK == 0
    n_ch = n_blk // BLOCKS_PER_CHUNK
    lp = n_blk * BLOCK_Q
    topk = min(TOPK_MAX, S // 4)
    assert CHUNK >= topk and lp < (1 << 13)

    w = w_in[0]
    zeros = lambda n: jnp.zeros((D, n), F32)
    w_kpe = w[:, O_KPE:O_GB]
    pad_pe = lambda m: jnp.concatenate([zeros(B_NOPE), m, zeros(LANES - B_NOPE - B_ROPE)], axis=1)
    wnn = jnp.concatenate([w[:, O_KA:O_VA], w[:, O_GA:O_QI], w[:, O_GB:O_END], w[:, O_CQ:O_CKV],
                           w[:, O_CKV:O_KPE], pad_pe(w_kpe), pad_pe(_rot_cols(w_kpe)),
                           w[:, O_KI:O_WI]], axis=1).astype(BF16)
    assert wnn.shape[1] == NN_END
    scale_a = A_HEAD_DIM ** -0.5
    qa_groups = []
    for h in range(HEADS):
        wh = w[:, O_QA + h * A_HEAD_DIM:O_QA + (h + 1) * A_HEAD_DIM] * scale_a
        qa_groups += [zeros(A_HEAD_DIM), wh] if h % 2 else [wh, zeros(A_HEAD_DIM)]
    wnt = jnp.concatenate(qa_groups + [w[:, O_VA:O_GA], w[:, O_QI:O_KI], w[:, O_WI:O_CQ], zeros(8)],
                          axis=1).T.astype(BF16)
    assert wnt.shape[0] == NT_END

    wq = w_uq[0]
    d_q = B_NOPE + B_ROPE
    uq, uqr = [], []
    zq = lambda n: jnp.zeros((Q_LORA, n), F32)
    for h in range(HEADS):
        wh = wq[:, h * d_q:(h + 1) * d_q]
        uq += [wh, zq(LANES - d_q)]
        uqr += [zq(B_NOPE), _rot_cols(wh[:, B_NOPE:]), zq(LANES - d_q)]
    wuq_t = jnp.concatenate(uq, axis=1).T.astype(BF16)
    wuqr_t = jnp.concatenate(uqr, axis=1).T.astype(BF16)
    wkv = w_ukv[0]
    d_kv = B_NOPE + B_V
    wk_groups, wv_groups = [], []
    for h in range(HEADS):
        wk_groups += [wkv[:, h * d_kv:h * d_kv + B_NOPE], jnp.zeros((KV_LORA, LANES - B_NOPE), F32)]
        wv_groups.append(wkv[:, h * d_kv + B_NOPE:(h + 1) * d_kv])
    wk = jnp.concatenate(wk_groups, axis=1).astype(BF16)
    wv_t = jnp.concatenate(wv_groups, axis=1).T.astype(BF16)

    pos = jnp.arange(lp, dtype=F32)
    inv_freq = ROPE_THETA ** (-jnp.arange(0, B_ROPE, 2, dtype=F32) / B_ROPE)
    ang = pos[:, None] * inv_freq[None, :]
    cos2 = jnp.concatenate([jnp.cos(ang)] * 2, axis=1)
    sin2 = jnp.concatenate([jnp.sin(ang)] * 2, axis=1)
    zl = lambda n: jnp.zeros((lp, n), F32)
    cos128 = jnp.concatenate([zl(B_NOPE), cos2, zl(LANES - d_q)], axis=1)
    sin128 = jnp.concatenate([zl(B_NOPE), sin2, zl(LANES - d_q)], axis=1)
    scale_b = d_q ** -0.5
    ct = (jnp.concatenate([jnp.ones((lp, B_NOPE), F32), cos2, zl(LANES - d_q)], axis=1) * scale_b).T
    st = (sin128 * scale_b).T

    row2 = lambda v: v.reshape(1, -1).astype(F32)
    hp = jnp.concatenate([jnp.broadcast_to(meta_tokens[None].astype(x.dtype), (B, N_META, D)), x,
                          jnp.zeros((B, lp - L, D), x.dtype)], axis=1)

    params = pltpu.CompilerParams(dimension_semantics=("arbitrary", "arbitrary"),
                                  vmem_limit_bytes=VMEM_LIMIT)
    sds = jax.ShapeDtypeStruct

    chunk_rows = lambda n: pl.BlockSpec((1, CHUNK, n), lambda b, c: (b, c, 0))
    per_block = lambda r, n: pl.BlockSpec((1, BLOCKS_PER_CHUNK, r, n), lambda b, c: (b, c, 0, 0))
    per_chunk_t = pl.BlockSpec((1, 1, A_WIDTH, CHUNK), lambda b, c: (b, c, 0, 0))
    ka, ki, gates, kb, qat, qit, wit, vat, qbt, vbt = pl.pallas_call(
        functools.partial(_proj_kernel, seq_len=L),
        grid=(B, n_ch),
        in_specs=[chunk_rows(D), _full((1, D)), _full((1, D)), _full((D, NN_END)), _full((NT_END, D)),
                  _full((HEADS * LANES, Q_LORA)), _full((HEADS * LANES, Q_LORA)),
                  _full((KV_LORA, HEADS * LANES)), _full((B_WIDTH, KV_LORA)),
                  _full((1, Q_LORA)), _full((1, KV_LORA)),
                  pl.BlockSpec((CHUNK, LANES), lambda b, c: (c, 0)),
                  pl.BlockSpec((CHUNK, LANES), lambda b, c: (c, 0)),
                  pl.BlockSpec((LANES, CHUNK), lambda b, c: (0, c)),
                  pl.BlockSpec((LANES, CHUNK), lambda b, c: (0, c))],
        out_specs=[chunk_rows(A_WIDTH), chunk_rows(IDX_DIM), chunk_rows(A_WIDTH + B_WIDTH),
                   chunk_rows(HEADS * LANES), per_block(HEADS * LANES, BLOCK_Q),
                   per_block(IDX_DIM, HEADS * LANES), per_block(HEADS, BLOCK_Q), per_chunk_t,
                   per_block(HEADS * LANES, BLOCK_Q), per_chunk_t],
        out_shape=[sds((B, lp, A_WIDTH), BF16), sds((B, lp, IDX_DIM), BF16),
                   sds((B, lp, A_WIDTH + B_WIDTH), F32), sds((B, lp, HEADS * LANES), BF16),
                   sds((B, n_blk, HEADS * LANES, BLOCK_Q), BF16),
                   sds((B, n_blk, IDX_DIM, HEADS * LANES), BF16),
                   sds((B, n_blk, HEADS, BLOCK_Q), F32), sds((B, n_ch, A_WIDTH, CHUNK), BF16),
                   sds((B, n_blk, HEADS * LANES, BLOCK_Q), BF16),
                   sds((B, n_ch, B_WIDTH, CHUNK), BF16)],
        compiler_params=params, name="proj",
    )(hp, row2(ln_emb_g), row2(ln_emb_b), wnn, wnt, wuq_t, wuqr_t, wk, wv_t,
      row2(q_norm_g[0]), row2(kv_norm_g[0]), cos128, sin128, ct, st)

    batch_rows = lambda n: pl.BlockSpec((1, lp, n), lambda b, i: (b, 0, 0))
    batch_t = pl.BlockSpec((1, n_ch, A_WIDTH, CHUNK), lambda b, i: (b, 0, 0, 0))
    blk = lambda r, n: pl.BlockSpec((1, 1, r, n), lambda b, i: (b, i, 0, 0))
    out_blk = pl.BlockSpec((1, BLOCK_Q, A_WIDTH), lambda b, i: (b, i, 0))
    oa = pl.pallas_call(
        functools.partial(_dsa_kernel, topk=topk, lp=lp),
        grid=(B, n_blk),
        in_specs=[pl.BlockSpec(memory_space=pltpu.SMEM), batch_rows(A_WIDTH), batch_rows(IDX_DIM),
                  batch_t, blk(HEADS * LANES, BLOCK_Q), blk(IDX_DIM, HEADS * LANES),
                  blk(HEADS, BLOCK_Q)],
        out_specs=out_blk,
        out_shape=sds((B, lp, A_WIDTH), F32),
        scratch_shapes=[pltpu.VMEM((lp, BLOCK_Q), I32),
                        pltpu.VMEM((HEADS, 2, BLOCK_Q, BLOCK_Q), F32),
                        pltpu.VMEM((A_WIDTH, BLOCK_Q), F32),
                        pltpu.VMEM((1, BLOCK_Q), I32)],
        compiler_params=params, name="dsa",
    )(rel_bias.astype(F32), ka, ki, vat, qat, qit, wit)

    ob = pl.pallas_call(
        _mla_kernel,
        grid=(B, n_blk),
        in_specs=[batch_rows(HEADS * LANES), batch_t, blk(HEADS * LANES, BLOCK_Q)],
        out_specs=out_blk,
        out_shape=sds((B, lp, B_WIDTH), F32),
        scratch_shapes=[pltpu.VMEM((B_WIDTH, BLOCK_Q), F32)],
        compiler_params=params, name="mla",
    )(kb, vbt, qbt)

    out = pl.pallas_call(
        _out_kernel,
        grid=(B, n_ch),
        in_specs=[chunk_rows(D), _full((1, D)), _full((1, D)), chunk_rows(A_WIDTH),
                  chunk_rows(B_WIDTH), chunk_rows(A_WIDTH + B_WIDTH), _full((A_WIDTH + B_WIDTH, D)),
                  _full((1, D)), _full((1, D))],
        out_specs=chunk_rows(D),
        out_shape=sds((B, lp, D), x.dtype),
        compiler_params=params, name="out",
    )(hp, row2(ln_emb_g), row2(ln_emb_b), oa, ob, gates, w_out[0].astype(BF16),
      row2(ln_post_g[0]), row2(ln_post_b[0]))
    return out[:, N_META:N_META + S, :]
```

```python
import functools
import math

import jax
import jax.numpy as jnp
from jax import lax
from jax.experimental import pallas as pl
from jax.experimental.pallas import tpu as pltpu

F32 = jnp.float32
BF16 = jnp.bfloat16
I32 = jnp.int32

N_META = 16
BLOCK_Q = 128
HEADS = 8
A_HEAD_DIM = 64
A_WIDTH = HEADS * A_HEAD_DIM
IDX_DIM = 64
TOPK_MAX = 256
B_NOPE = 64
B_ROPE = 32
B_V = 64
B_WIDTH = HEADS * B_V
Q_LORA = 256
KV_LORA = 128
ROPE_THETA = 10000.0
REL_BUCKETS = 32
REL_MAX_DIST = 128
LN_EPS = 1e-5
RMS_EPS = 1e-6
DEPTH = 1
ALPHA = (2.0 * DEPTH) ** 0.25

LANES = 128
BLOCKS_PER_CHUNK = 3
CHUNK = BLOCKS_PER_CHUNK * BLOCK_Q
MASKED = -1e30
INT_MIN = -2147483648
KEY_NEG_INF = -0x7F800000
VMEM_LIMIT = 56 * 1024 * 1024

_SPLITS = (A_WIDTH, A_WIDTH, A_WIDTH, A_WIDTH, HEADS * IDX_DIM, IDX_DIM, HEADS,
           Q_LORA, KV_LORA, B_ROPE, B_WIDTH)
_OFF = [0]
for _s in _SPLITS:
    _OFF.append(_OFF[-1] + _s)
(O_QA, O_KA, O_VA, O_GA, O_QI, O_KI, O_WI, O_CQ, O_CKV, O_KPE, O_GB, O_END) = _OFF

NN_KA = 0
NN_G = NN_KA + A_WIDTH
NN_CQ = NN_G + A_WIDTH + B_WIDTH
NN_CKV = NN_CQ + Q_LORA
NN_PE = NN_CKV + KV_LORA
NN_PER = NN_PE + LANES
NN_KI = NN_PER + LANES
NN_END = NN_KI + IDX_DIM
NT_QA = 0
NT_VA = NT_QA + HEADS * LANES
NT_QI = NT_VA + A_WIDTH
NT_WI = NT_QI + HEADS * IDX_DIM
NT_END = NT_WI + 16

_NT_DIMS = (((1,), (1,)), ((), ()))


def _dot_nt(a, b):
    return lax.dot_general(a, b, _NT_DIMS, preferred_element_type=F32)


def _layer_norm(x, g, b):
    mu = jnp.mean(x, axis=-1, keepdims=True)
    xc = x - mu
    var = jnp.mean(xc * xc, axis=-1, keepdims=True)
    return xc * lax.rsqrt(var + LN_EPS) * g + b


def _rms_norm(x, g):
    return x * lax.rsqrt(jnp.mean(x * x, axis=-1, keepdims=True) + RMS_EPS) * g


def _proj_kernel(x_ref, lng_ref, lnb_ref, wnn_ref, wnt_ref, wuq_ref, wuqr_ref, wk_ref, wv_ref,
                 qg_ref, kvg_ref, cos_ref, sin_ref, ct_ref, st_ref,
                 ka_ref, ki_ref, g_ref, kb_ref, qat_ref, qit_ref, wit_ref, vat_ref, qbt_ref,
                 vbt_ref, *, seq_len):
    c = pl.program_id(1)
    y = _layer_norm(x_ref[0], lng_ref[...], lnb_ref[...])
    pos = c * CHUNK + lax.broadcasted_iota(I32, (CHUNK, 1), 0)
    y = jnp.where(pos < seq_len, y, 0.0)
    hb = y.astype(BF16)

    def nn(lo, hi):
        return jnp.dot(hb, wnn_ref[:, lo:hi], preferred_element_type=F32)

    def nt(lo, hi):
        return _dot_nt(wnt_ref[lo:hi, :], hb)

    ka_ref[0] = nn(NN_KA, NN_G).astype(BF16)
    g_ref[0] = nn(NN_G, NN_CQ)
    ki_ref[0] = nn(NN_KI, NN_END).astype(BF16)

    qat = nt(NT_QA, NT_VA).astype(BF16)
    for s in range(BLOCKS_PER_CHUNK):
        qat_ref[0, s] = qat[:, s * BLOCK_Q:(s + 1) * BLOCK_Q]
    vat_ref[0, 0] = nt(NT_VA, NT_QI).astype(BF16)
    qit = nt(NT_QI, NT_WI).astype(BF16)
    for s in range(BLOCKS_PER_CHUNK):
        for h in range(HEADS):
            qit_ref[0, s, :, h * LANES:(h + 1) * LANES] = (
                qit[h * IDX_DIM:(h + 1) * IDX_DIM, s * BLOCK_Q:(s + 1) * BLOCK_Q])
    wit = nt(NT_WI, NT_END) * (HEADS ** -0.5)
    wit = wit * (IDX_DIM ** -0.5)
    for s in range(BLOCKS_PER_CHUNK):
        wit_ref[0, s] = wit[0:HEADS, s * BLOCK_Q:(s + 1) * BLOCK_Q]

    cqn = _rms_norm(nn(NN_CQ, NN_CKV), qg_ref[...]).astype(BF16)
    ckvn = _rms_norm(nn(NN_CKV, NN_PE), kvg_ref[...]).astype(BF16)
    qbt = _dot_nt(wuq_ref[...], cqn)
    qbrt = _dot_nt(wuqr_ref[...], cqn)
    ct = ct_ref[...]
    st = st_ref[...]
    for h in range(HEADS):
        rows = slice(h * LANES, (h + 1) * LANES)
        qh = (qbt[rows] * ct + qbrt[rows] * st).astype(BF16)
        for s in range(BLOCKS_PER_CHUNK):
            qbt_ref[0, s, rows, :] = qh[:, s * BLOCK_Q:(s + 1) * BLOCK_Q]
    kn = jnp.dot(ckvn, wk_ref[...], preferred_element_type=F32)
    kpe = nn(NN_PE, NN_PER) * cos_ref[...] + nn(NN_PER, NN_KI) * sin_ref[...]
    for h in range(HEADS):
        cols = slice(h * LANES, (h + 1) * LANES)
        kb_ref[0, :, cols] = (kn[:, cols] + kpe).astype(BF16)
    vbt_ref[0, 0] = _dot_nt(wv_ref[...], ckvn).astype(BF16)


def _t5_bucket(d):
    max_exact = REL_BUCKETS // 2
    d = jnp.maximum(d, 0)
    d_f = jnp.maximum(d, 1).astype(F32)
    large = max_exact + (jnp.log(d_f / max_exact) / math.log(REL_MAX_DIST / max_exact)
                         * (REL_BUCKETS - max_exact)).astype(I32)
    large = jnp.minimum(large, REL_BUCKETS - 1)
    return jnp.where(d < max_exact, d, large)


def _dsa_kernel(relb_ref, ka_ref, ki_ref, vat_ref, qat_ref, qit_ref, wit_ref, oa_ref,
                keys_s, bias_s, acc_s, j_s, s_scr, *, topk, lp):
    b = pl.program_id(0)
    i = pl.program_id(1)
    nch = i // BLOCKS_PER_CHUNK + 1
    row = lax.broadcasted_iota(I32, (CHUNK, BLOCK_Q), 0)
    pos_q = i * BLOCK_Q + lax.broadcasted_iota(I32, (CHUNK, BLOCK_Q), 1)

    @pl.when((b == 0) & (i == 0))
    def _():
        r = lax.broadcasted_iota(I32, (BLOCK_Q, BLOCK_Q), 0)
        q = lax.broadcasted_iota(I32, (BLOCK_Q, BLOCK_Q), 1)
        for dl in range(2):
            bucket = _t5_bucket(dl * BLOCK_Q + q - r)
            for h in range(HEADS):
                t = jnp.zeros((BLOCK_Q, BLOCK_Q), F32)
                for bk in range(REL_BUCKETS):
                    t = jnp.where(bucket == bk, relb_ref[bk, h], t)
                bias_s[h, dl] = t - relb_ref[REL_BUCKETS - 1, h]

    def idx_body(c, carry):
        k0 = pl.multiple_of(c * CHUNK, CHUNK)
        rel = jnp.dot(ki_ref[0, pl.ds(k0, CHUNK), :], qit_ref[0, 0],
                      preferred_element_type=F32)
        s = jnp.zeros((CHUNK, BLOCK_Q), F32)
        for h in range(HEADS):
            s = s + jnp.maximum(rel[:, h * LANES:(h + 1) * LANES], 0.0) * wit_ref[0, 0, h:h + 1, :]
        pos_k = k0 + row
        s = jnp.where(pos_k < N_META, jnp.inf, s)
        s = jnp.where(pos_k <= pos_q, s, -jnp.inf)
        bits = lax.bitcast_convert_type(s, I32)
        keys_s[pl.ds(k0, CHUNK), :] = jnp.where(bits < 0, INT_MIN - bits, bits)
        return carry

    lax.fori_loop(0, nch, idx_body, 0)

    def count(pred):
        def body(c, acc):
            k0 = pl.multiple_of(c * CHUNK, CHUNK)
            hit = pred(keys_s[pl.ds(k0, CHUNK), :], k0 + row)
            return acc + jnp.sum(jnp.where(hit, 1, 0).reshape(CHUNK // 8, 8, BLOCK_Q), axis=0)
        acc = lax.fori_loop(0, nch, body, jnp.zeros((8, BLOCK_Q), I32))
        return jnp.sum(acc, axis=0, keepdims=True)

    def bit_body(it, carry):
        t, cnt_t = carry
        cand = t | jnp.left_shift(jnp.int32(1), 31 - it)
        thr_c = cand ^ INT_MIN
        cnt = count(lambda k, p: k >= thr_c)
        ok = cnt >= topk
        return jnp.where(ok, cand, t), jnp.where(ok, cnt, cnt_t)

    t0 = jnp.zeros((1, BLOCK_Q), I32)
    t, cnt_t = lax.fori_loop(0, 32, bit_body, (t0, jnp.full((1, BLOCK_Q), lp, I32)))
    thr = t ^ INT_MIN

    j_s[...] = jnp.full((1, BLOCK_Q), lp, I32)
    tie_cut = (cnt_t > topk) & (thr > KEY_NEG_INF)
    any_cut = jnp.max(jnp.where(tie_cut, 1, 0))

    @pl.when(any_cut > 0)
    def _():
        need = topk - count(lambda k, p: k > thr)

        def jbit(it, j):
            cand = j | jnp.left_shift(jnp.int32(1), 12 - it)
            cnt = count(lambda k, p: (k == thr) & (p < cand))
            return jnp.where(cnt < need, cand, j)

        j = lax.fori_loop(0, 13, jbit, jnp.zeros((1, BLOCK_Q), I32))
        j_s[...] = jnp.where(tie_cut, j, lp)

    j_last = j_s[...]

    acc_s[...] = jnp.zeros_like(acc_s)

    def scores(c, h):
        k0 = pl.multiple_of(c * CHUNK, CHUNK)
        pair = slice((h // 2) * LANES, (h // 2 + 1) * LANES)
        s_scr[h] = jnp.dot(ka_ref[0, pl.ds(k0, CHUNK), pair],
                           qat_ref[0, 0, h * LANES:(h + 1) * LANES, :],
                           preferred_element_type=F32)

    def att_body(c, carry, near, last):
        ms, ls = carry
        k0 = pl.multiple_of(c * CHUNK, CHUNK)
        key = keys_s[pl.ds(k0, CHUNK), :]
        pos_k = k0 + row
        sel = (key > thr) | ((key == thr) & (pos_k <= j_last))
        sel = sel & (pos_k <= pos_q)
        madd = jnp.where(sel, 0.0, MASKED)
        new_ms, new_ls = [], []
        for h in range(HEADS):
            s = s_scr[h] + madd
            if not last:
                scores(c + 1, h)
            if near:
                parts = []
                for sub in range(BLOCKS_PER_CHUNK):
                    dl = i - (c * BLOCKS_PER_CHUNK + sub)
                    parts.append(jnp.where(dl == 0, bias_s[h, 0],
                                           jnp.where(dl == 1, bias_s[h, 1], 0.0)))
                s = s + jnp.concatenate(parts, axis=0)
            m_new = jnp.maximum(ms[h], jnp.max(s, axis=0, keepdims=True))
            alpha = jnp.exp(ms[h] - m_new)
            p = jnp.exp(s - m_new)
            new_ls.append(alpha * ls[h] + jnp.sum(p, axis=0, keepdims=True))
            new_ms.append(m_new)
            rows = slice(h * A_HEAD_DIM, (h + 1) * A_HEAD_DIM)
            pv = jnp.dot(vat_ref[0, c, rows, :], p.astype(BF16), preferred_element_type=F32)
            acc_s[rows, :] = alpha * acc_s[rows, :] + pv
        return tuple(new_ms), tuple(new_ls)

    init = (tuple(jnp.full((1, BLOCK_Q), MASKED, F32) for _ in range(HEADS)),
            tuple(jnp.zeros((1, BLOCK_Q), F32) for _ in range(HEADS)))
    for h in range(HEADS):
        scores(0, h)
    n_far = jnp.maximum(nch - 2, 0)
    carry = lax.fori_loop(0, n_far, functools.partial(att_body, near=False, last=False), init)
    carry = lax.fori_loop(n_far, nch - 1, functools.partial(att_body, near=True, last=False), carry)
    ms, ls = att_body(nch - 1, carry, near=True, last=True)

    outs = []
    for h in range(HEADS):
        rows = slice(h * A_HEAD_DIM, (h + 1) * A_HEAD_DIM)
        outs.append(acc_s[rows, :] * (1.0 / ls[h]))
    oa_ref[0] = jnp.concatenate(outs, axis=0).T


def _mla_kernel(kb_ref, vbt_ref, qbt_ref, ob_ref, s_scr, acc_s):
    i = pl.program_id(1)
    nch = i // BLOCKS_PER_CHUNK + 1
    row = lax.broadcasted_iota(I32, (CHUNK, BLOCK_Q), 0)
    pos_q = i * BLOCK_Q + lax.broadcasted_iota(I32, (CHUNK, BLOCK_Q), 1)
    acc_s[...] = jnp.zeros_like(acc_s)

    def scores(c, h):
        k0 = pl.multiple_of(c * CHUNK, CHUNK)
        grp = slice(h * LANES, (h + 1) * LANES)
        s_scr[h] = jnp.dot(kb_ref[0, pl.ds(k0, CHUNK), grp], qbt_ref[0, 0, grp, :],
                           preferred_element_type=F32)

    def body(c, carry, last):
        ms, ls = carry
        new_ms, new_ls = [], []
        for h in range(HEADS):
            s = s_scr[h]
            if last:
                s = jnp.where(c * CHUNK + row <= pos_q, s, MASKED)
            else:
                scores(c + 1, h)
            m_new = jnp.maximum(ms[h], jnp.max(s, axis=0, keepdims=True))
            alpha = jnp.exp(ms[h] - m_new)
            p = jnp.exp(s - m_new)
            new_ls.append(alpha * ls[h] + jnp.sum(p, axis=0, keepdims=True))
            new_ms.append(m_new)
            rows = slice(h * B_V, (h + 1) * B_V)
            pv = jnp.dot(vbt_ref[0, c, rows, :], p.astype(BF16), preferred_element_type=F32)
            acc_s[rows, :] = alpha * acc_s[rows, :] + pv
        return tuple(new_ms), tuple(new_ls)

    for h in range(HEADS):
        scores(0, h)
    init = (tuple(jnp.full((1, BLOCK_Q), MASKED, F32) for _ in range(HEADS)),
            tuple(jnp.zeros((1, BLOCK_Q), F32) for _ in range(HEADS)))
    carry = lax.fori_loop(0, nch - 1, functools.partial(body, last=False), init)
    ms, ls = body(nch - 1, carry, last=True)

    outs = []
    for h in range(HEADS):
        rows = slice(h * B_V, (h + 1) * B_V)
        outs.append(acc_s[rows, :] * (1.0 / ls[h]))
    ob_ref[0] = jnp.concatenate(outs, axis=0).T


def _silu(x):
    return x * (1.0 / (1.0 + jnp.exp(-x)))


def _out_kernel(x_ref, lng_ref, lnb_ref, oa_ref, ob_ref, g_ref, wout_ref, pg_ref, pb_ref, out_ref):
    h = _layer_norm(x_ref[0], lng_ref[...], lnb_ref[...])
    g = g_ref[0]
    mixed = jnp.concatenate([oa_ref[0] * _silu(g[:, :A_WIDTH]), ob_ref[0] * _silu(g[:, A_WIDTH:])],
                            axis=-1)
    out = jnp.dot(mixed.astype(BF16), wout_ref[...], preferred_element_type=F32)
    out_ref[0] = _layer_norm(ALPHA * h + out, pg_ref[...], pb_ref[...])


def _rot_cols(w):
    half = w.shape[-1] // 2
    return jnp.concatenate([-w[:, half:], w[:, :half]], axis=-1)


def _full(shape):
    return pl.BlockSpec(shape, lambda *_: (0,) * len(shape))


def kernel(x, meta_tokens, ln_emb_g, ln_emb_b, w_in, w_uq, q_norm_g, w_ukv, kv_norm_g, rel_bias,
           w_out, ln_post_g, ln_post_b):
    B, S, D = x.shape
    assert w_in.shape[0] == DEPTH == 1
    L = N_META + S
    n_blk = -(-L // BLOCK_Q)
    assert n_blk % BLOCKS_PER_CHUNK == 0
    n_ch = n_blk // BLOCKS_PER_CHUNK
    lp = n_blk * BLOCK_Q
    topk = min(TOPK_MAX, S // 4)
    assert CHUNK >= topk and lp < (1 << 13)

    w = w_in[0]
    zeros = lambda n: jnp.zeros((D, n), F32)
    w_kpe = w[:, O_KPE:O_GB]
    pad_pe = lambda m: jnp.concatenate([zeros(B_NOPE), m, zeros(LANES - B_NOPE - B_ROPE)], axis=1)
    wnn = jnp.concatenate([w[:, O_KA:O_VA], w[:, O_GA:O_QI], w[:, O_GB:O_END], w[:, O_CQ:O_CKV],
                           w[:, O_CKV:O_KPE], pad_pe(w_kpe), pad_pe(_rot_cols(w_kpe)),
                           w[:, O_KI:O_WI]], axis=1).astype(BF16)
    assert wnn.shape[1] == NN_END
    scale_a = A_HEAD_DIM ** -0.5
    qa_groups = []
    for h in range(HEADS):
        wh = w[:, O_QA + h * A_HEAD_DIM:O_QA + (h + 1) * A_HEAD_DIM] * scale_a
        qa_groups += [zeros(A_HEAD_DIM), wh] if h % 2 else [wh, zeros(A_HEAD_DIM)]
    wnt = jnp.concatenate(qa_groups + [w[:, O_VA:O_GA], w[:, O_QI:O_KI], w[:, O_WI:O_CQ], zeros(8)],
                          axis=1).T.astype(BF16)
    assert wnt.shape[0] == NT_END

    wq = w_uq[0]
    d_q = B_NOPE + B_ROPE
    uq, uqr = [], []
    zq = lambda n: jnp.zeros((Q_LORA, n), F32)
    for h in range(HEADS):
        wh = wq[:, h * d_q:(h + 1) * d_q]
        uq += [wh, zq(LANES - d_q)]
        uqr += [zq(B_NOPE), _rot_cols(wh[:, B_NOPE:]), zq(LANES - d_q)]
    wuq_t = jnp.concatenate(uq, axis=1).T.astype(BF16)
    wuqr_t = jnp.concatenate(uqr, axis=1).T.astype(BF16)
    wkv = w_ukv[0]
    d_kv = B_NOPE + B_V
    wk_groups, wv_groups = [], []
    for h in range(HEADS):
        wk_groups += [wkv[:, h * d_kv:h * d_kv + B_NOPE], jnp.zeros((KV_LORA, LANES - B_NOPE), F32)]
        wv_groups.append(wkv[:, h * d_kv + B_NOPE:(h + 1) * d_kv])
    wk = jnp.concatenate(wk_groups, axis=1).astype(BF16)
    wv_t = jnp.concatenate(wv_groups, axis=1).T.astype(BF16)

    pos = jnp.arange(lp, dtype=F32)
    inv_freq = ROPE_THETA ** (-jnp.arange(0, B_ROPE, 2, dtype=F32) / B_ROPE)
    ang = pos[:, None] * inv_freq[None, :]
    cos2 = jnp.concatenate([jnp.cos(ang)] * 2, axis=1)
    sin2 = jnp.concatenate([jnp.sin(ang)] * 2, axis=1)
    zl = lambda n: jnp.zeros((lp, n), F32)
    cos128 = jnp.concatenate([zl(B_NOPE), cos2, zl(LANES - d_q)], axis=1)
    sin128 = jnp.concatenate([zl(B_NOPE), sin2, zl(LANES - d_q)], axis=1)
    scale_b = d_q ** -0.5
    ct = (jnp.concatenate([jnp.ones((lp, B_NOPE), F32), cos2, zl(LANES - d_q)], axis=1) * scale_b).T
    st = (sin128 * scale_b).T

    row2 = lambda v: v.reshape(1, -1).astype(F32)
    hp = jnp.concatenate([jnp.broadcast_to(meta_tokens[None].astype(x.dtype), (B, N_META, D)), x,
                          jnp.zeros((B, lp - L, D), x.dtype)], axis=1)

    params = pltpu.CompilerParams(dimension_semantics=("arbitrary", "arbitrary"),
                                  vmem_limit_bytes=VMEM_LIMIT)
    sds = jax.ShapeDtypeStruct

    chunk_rows = lambda n: pl.BlockSpec((1, CHUNK, n), lambda b, c: (b, c, 0))
    per_block = lambda r, n: pl.BlockSpec((1, BLOCKS_PER_CHUNK, r, n), lambda b, c: (b, c, 0, 0))
    per_chunk_t = pl.BlockSpec((1, 1, A_WIDTH, CHUNK), lambda b, c: (b, c, 0, 0))
    ka, ki, gates, kb, qat, qit, wit, vat, qbt, vbt = pl.pallas_call(
        functools.partial(_proj_kernel, seq_len=L),
        grid=(B, n_ch),
        in_specs=[chunk_rows(D), _full((1, D)), _full((1, D)), _full((D, NN_END)), _full((NT_END, D)),
                  _full((HEADS * LANES, Q_LORA)), _full((HEADS * LANES, Q_LORA)),
                  _full((KV_LORA, HEADS * LANES)), _full((B_WIDTH, KV_LORA)),
                  _full((1, Q_LORA)), _full((1, KV_LORA)),
                  pl.BlockSpec((CHUNK, LANES), lambda b, c: (c, 0)),
                  pl.BlockSpec((CHUNK, LANES), lambda b, c: (c, 0)),
                  pl.BlockSpec((LANES, CHUNK), lambda b, c: (0, c)),
                  pl.BlockSpec((LANES, CHUNK), lambda b, c: (0, c))],
        out_specs=[chunk_rows(A_WIDTH), chunk_rows(IDX_DIM), chunk_rows(A_WIDTH + B_WIDTH),
                   chunk_rows(HEADS * LANES), per_block(HEADS * LANES, BLOCK_Q),
                   per_block(IDX_DIM, HEADS * LANES), per_block(HEADS, BLOCK_Q), per_chunk_t,
                   per_block(HEADS * LANES, BLOCK_Q), per_chunk_t],
        out_shape=[sds((B, lp, A_WIDTH), BF16), sds((B, lp, IDX_DIM), BF16),
                   sds((B, lp, A_WIDTH + B_WIDTH), F32), sds((B, lp, HEADS * LANES), BF16),
                   sds((B, n_blk, HEADS * LANES, BLOCK_Q), BF16),
                   sds((B, n_blk, IDX_DIM, HEADS * LANES), BF16),
                   sds((B, n_blk, HEADS, BLOCK_Q), F32), sds((B, n_ch, A_WIDTH, CHUNK), BF16),
                   sds((B, n_blk, HEADS * LANES, BLOCK_Q), BF16),
                   sds((B, n_ch, B_WIDTH, CHUNK), BF16)],
        compiler_params=params, name="proj",
    )(hp, row2(ln_emb_g), row2(ln_emb_b), wnn, wnt, wuq_t, wuqr_t, wk, wv_t,
      row2(q_norm_g[0]), row2(kv_norm_g[0]), cos128, sin128, ct, st)

    batch_rows = lambda n: pl.BlockSpec((1, lp, n), lambda b, i: (b, 0, 0))
    batch_t = pl.BlockSpec((1, n_ch, A_WIDTH, CHUNK), lambda b, i: (b, 0, 0, 0))
    blk = lambda r, n: pl.BlockSpec((1, 1, r, n), lambda b, i: (b, i, 0, 0))
    out_blk = pl.BlockSpec((1, BLOCK_Q, A_WIDTH), lambda b, i: (b, i, 0))
    oa = pl.pallas_call(
        functools.partial(_dsa_kernel, topk=topk, lp=lp),
        grid=(B, n_blk),
        in_specs=[pl.BlockSpec(memory_space=pltpu.SMEM), batch_rows(A_WIDTH), batch_rows(IDX_DIM),
                  batch_t, blk(HEADS * LANES, BLOCK_Q), blk(IDX_DIM, HEADS * LANES),
                  blk(HEADS, BLOCK_Q)],
        out_specs=out_blk,
        out_shape=sds((B, lp, A_WIDTH), F32),
        scratch_shapes=[pltpu.VMEM((lp, BLOCK_Q), I32),
                        pltpu.VMEM((HEADS, 2, BLOCK_Q, BLOCK_Q), F32),
                        pltpu.VMEM((A_WIDTH, BLOCK_Q), F32),
                        pltpu.VMEM((1, BLOCK_Q), I32),
                        pltpu.VMEM((HEADS, CHUNK, BLOCK_Q), F32)],
        compiler_params=params, name="dsa",
    )(rel_bias.astype(F32), ka, ki, vat, qat, qit, wit)

    ob = pl.pallas_call(
        _mla_kernel,
        grid=(B, n_blk),
        in_specs=[batch_rows(HEADS * LANES), batch_t, blk(HEADS * LANES, BLOCK_Q)],
        out_specs=out_blk,
        out_shape=sds((B, lp, B_WIDTH), F32),
        scratch_shapes=[pltpu.VMEM((HEADS, CHUNK, BLOCK_Q), F32),
                        pltpu.VMEM((B_WIDTH, BLOCK_Q), F32)],
        compiler_params=params, name="mla",
    )(kb, vbt, qbt)

    out = pl.pallas_call(
        _out_kernel,
        grid=(B, n_ch),
        in_specs=[chunk_rows(D), _full((1, D)), _full((1, D)), chunk_rows(A_WIDTH),
                  chunk_rows(B_WIDTH), chunk_rows(A_WIDTH + B_WIDTH), _full((A_WIDTH + B_WIDTH, D)),
                  _full((1, D)), _full((1, D))],
        out_specs=chunk_rows(D),
        out_shape=sds((B, lp, D), x.dtype),
        compiler_params=params, name="out",
    )(hp, row2(ln_emb_g), row2(ln_emb_b), oa, ob, gates, w_out[0].astype(BF16),
      row2(ln_post_g[0]), row2(ln_post_b[0]))
    return out[:, N_META:N_META + S, :]
```

```python
import functools
import math

import jax
import jax.numpy as jnp
from jax import lax
from jax.experimental import pallas as pl
from jax.experimental.pallas import tpu as pltpu

F32 = jnp.float32
BF16 = jnp.bfloat16
I32 = jnp.int32

N_META = 16
BLOCK_Q = 128
HEADS = 8
A_HEAD_DIM = 64
A_WIDTH = HEADS * A_HEAD_DIM
IDX_DIM = 64
TOPK_MAX = 256
B_NOPE = 64
B_ROPE = 32
B_V = 64
B_WIDTH = HEADS * B_V
Q_LORA = 256
KV_LORA = 128
ROPE_THETA = 10000.0
REL_BUCKETS = 32
REL_MAX_DIST = 128
LN_EPS = 1e-5
RMS_EPS = 1e-6
DEPTH = 1
ALPHA = (2.0 * DEPTH) ** 0.25

LANES = 128
BLOCKS_PER_CHUNK = 3
CHUNK = BLOCKS_PER_CHUNK * BLOCK_Q
HALF = CHUNK // 2
MASKED = -1e30
LOG2_E = math.log2(math.e)
INT_MIN = -2147483648
KEY_NEG_INF = -0x7F800000
VMEM_LIMIT = 56 * 1024 * 1024

_SPLITS = (A_WIDTH, A_WIDTH, A_WIDTH, A_WIDTH, HEADS * IDX_DIM, IDX_DIM, HEADS,
           Q_LORA, KV_LORA, B_ROPE, B_WIDTH)
_OFF = [0]
for _s in _SPLITS:
    _OFF.append(_OFF[-1] + _s)
(O_QA, O_KA, O_VA, O_GA, O_QI, O_KI, O_WI, O_CQ, O_CKV, O_KPE, O_GB, O_END) = _OFF

NN_KA = 0
NN_G = NN_KA + A_WIDTH
NN_CQ = NN_G + A_WIDTH + B_WIDTH
NN_CKV = NN_CQ + Q_LORA
NN_PE = NN_CKV + KV_LORA
NN_PER = NN_PE + LANES
NN_KI = NN_PER + LANES
NN_END = NN_KI + IDX_DIM
NT_QA = 0
NT_VA = NT_QA + HEADS * LANES
NT_QI = NT_VA + A_WIDTH
NT_WI = NT_QI + HEADS * IDX_DIM
NT_END = NT_WI + 16

_NT_DIMS = (((1,), (1,)), ((), ()))


def _dot_nt(a, b):
    return lax.dot_general(a, b, _NT_DIMS, preferred_element_type=F32)


def _layer_norm(x, g, b):
    mu = jnp.mean(x, axis=-1, keepdims=True)
    xc = x - mu
    var = jnp.mean(xc * xc, axis=-1, keepdims=True)
    return xc * lax.rsqrt(var + LN_EPS) * g + b


def _rms_norm(x, g):
    return x * lax.rsqrt(jnp.mean(x * x, axis=-1, keepdims=True) + RMS_EPS) * g


def _proj_kernel(x_ref, lng_ref, lnb_ref, wnn_ref, wnt_ref, wuq_ref, wuqr_ref, wk_ref, wv_ref,
                 qg_ref, kvg_ref, cos_ref, sin_ref, ct_ref, st_ref,
                 ka_ref, ki_ref, g_ref, kb_ref, qat_ref, qit_ref, wit_ref, vat_ref, qbt_ref,
                 vbt_ref, *, seq_len):
    c = pl.program_id(1)
    y = _layer_norm(x_ref[0], lng_ref[...], lnb_ref[...])
    pos = c * CHUNK + lax.broadcasted_iota(I32, (CHUNK, 1), 0)
    y = jnp.where(pos < seq_len, y, 0.0)
    hb = y.astype(BF16)

    def nn(lo, hi):
        return jnp.dot(hb, wnn_ref[:, lo:hi], preferred_element_type=F32)

    def nt(lo, hi):
        return _dot_nt(wnt_ref[lo:hi, :], hb)

    ka_ref[0] = nn(NN_KA, NN_G).astype(BF16)
    g_ref[0] = nn(NN_G, NN_CQ)
    ki_ref[0] = nn(NN_KI, NN_END).astype(BF16)

    qat = (nt(NT_QA, NT_VA) * (A_HEAD_DIM ** -0.5 * LOG2_E)).astype(BF16)
    for s in range(BLOCKS_PER_CHUNK):
        qat_ref[0, s] = qat[:, s * BLOCK_Q:(s + 1) * BLOCK_Q]
    vat_ref[0, 0] = nt(NT_VA, NT_QI).astype(BF16)
    qit = nt(NT_QI, NT_WI).astype(BF16)
    for s in range(BLOCKS_PER_CHUNK):
        for h in range(HEADS):
            qit_ref[0, s, :, h * LANES:(h + 1) * LANES] = (
                qit[h * IDX_DIM:(h + 1) * IDX_DIM, s * BLOCK_Q:(s + 1) * BLOCK_Q])
    wit = nt(NT_WI, NT_END) * (HEADS ** -0.5)
    wit = wit * (IDX_DIM ** -0.5)
    for s in range(BLOCKS_PER_CHUNK):
        wit_ref[0, s] = wit[0:HEADS, s * BLOCK_Q:(s + 1) * BLOCK_Q]

    cqn = _rms_norm(nn(NN_CQ, NN_CKV), qg_ref[...]).astype(BF16)
    ckvn = _rms_norm(nn(NN_CKV, NN_PE), kvg_ref[...]).astype(BF16)
    qbt = _dot_nt(wuq_ref[...], cqn)
    qbrt = _dot_nt(wuqr_ref[...], cqn)
    ct = ct_ref[...]
    st = st_ref[...]
    for h in range(HEADS):
        rows = slice(h * LANES, (h + 1) * LANES)
        qh = (qbt[rows] * ct + qbrt[rows] * st).astype(BF16)
        for s in range(BLOCKS_PER_CHUNK):
            qbt_ref[0, s, rows, :] = qh[:, s * BLOCK_Q:(s + 1) * BLOCK_Q]
    kn = jnp.dot(ckvn, wk_ref[...], preferred_element_type=F32)
    kpe = nn(NN_PE, NN_PER) * cos_ref[...] + nn(NN_PER, NN_KI) * sin_ref[...]
    for h in range(HEADS):
        cols = slice(h * LANES, (h + 1) * LANES)
        kb_ref[0, :, cols] = (kn[:, cols] + kpe).astype(BF16)
    vbt_ref[0, 0] = _dot_nt(wv_ref[...], ckvn).astype(BF16)


def _t5_bucket(d):
    max_exact = REL_BUCKETS // 2
    d = jnp.maximum(d, 0)
    d_f = jnp.maximum(d, 1).astype(F32)
    large = max_exact + (jnp.log(d_f / max_exact) / math.log(REL_MAX_DIST / max_exact)
                         * (REL_BUCKETS - max_exact)).astype(I32)
    large = jnp.minimum(large, REL_BUCKETS - 1)
    return jnp.where(d < max_exact, d, large)


def _dsa_kernel(relb_ref, ka_ref, ki_ref, vat_ref, qat_ref, qit_ref, wit_ref, oa_ref,
                keys_s, bias_s, acc_s, j_s, s_scr, madd_s, hi_s, lo_s, lo2_s, *, topk, lp):
    b = pl.program_id(0)
    i = pl.program_id(1)
    nch = i // BLOCKS_PER_CHUNK + 1
    row = lax.broadcasted_iota(I32, (CHUNK, BLOCK_Q), 0)
    pos_q = i * BLOCK_Q + lax.broadcasted_iota(I32, (CHUNK, BLOCK_Q), 1)

    @pl.when((b == 0) & (i == 0))
    def _():
        r = lax.broadcasted_iota(I32, (BLOCK_Q, BLOCK_Q), 0)
        q = lax.broadcasted_iota(I32, (BLOCK_Q, BLOCK_Q), 1)
        for dl in range(2):
            bucket = _t5_bucket(dl * BLOCK_Q + q - r)
            for h in range(HEADS):
                t = jnp.zeros((BLOCK_Q, BLOCK_Q), F32)
                for bk in range(REL_BUCKETS):
                    t = jnp.where(bucket == bk, relb_ref[bk, h], t)
                bias_s[h, dl] = (t - relb_ref[REL_BUCKETS - 1, h]) * LOG2_E

    def idx_body(c, carry):
        k0 = pl.multiple_of(c * CHUNK, CHUNK)
        rel = jnp.dot(ki_ref[0, pl.ds(k0, CHUNK), :], qit_ref[0, 0],
                      preferred_element_type=F32)
        s = jnp.zeros((CHUNK, BLOCK_Q), F32)
        for h in range(HEADS):
            s = s + jnp.maximum(rel[:, h * LANES:(h + 1) * LANES], 0.0) * wit_ref[0, 0, h:h + 1, :]
        pos_k = k0 + row
        s = jnp.where(pos_k < N_META, jnp.inf, s)
        s = jnp.where(pos_k <= pos_q, s, -jnp.inf)
        bits = lax.bitcast_convert_type(s, I32)
        key = jnp.where(bits < 0, INT_MIN - bits, bits)
        keys_s[pl.ds(k0, CHUNK), :] = key
        h0 = pl.multiple_of(c * HALF, HALF)
        ka_, kb_ = key[:HALF], key[HALF:]
        hi_s[pl.ds(h0, HALF), :] = ((ka_ >> 16) & 0xFFFF) | (kb_ & -0x10000)
        lo_s[pl.ds(h0, HALF), :] = ((ka_ & 0xFFFF) ^ 0x8000) | ((kb_ << 16) ^ INT_MIN)
        return carry

    lax.fori_loop(0, nch, idx_body, 0)

    def count(pred):
        def body(c, acc):
            k0 = pl.multiple_of(c * CHUNK, CHUNK)
            hit = pred(keys_s[pl.ds(k0, CHUNK), :], k0 + row)
            return acc + jnp.sum(jnp.where(hit, 1, 0).reshape(CHUNK // 8, 8, BLOCK_Q), axis=0)
        acc = lax.fori_loop(0, nch, body, jnp.zeros((8, BLOCK_Q), I32))
        return jnp.sum(acc, axis=0, keepdims=True)

    def splat16(pat):
        word = pat | (pat << 16)
        return pltpu.bitcast(jnp.broadcast_to(word, (8, BLOCK_Q)), jnp.int16)

    def count16(scr, pat, strict=False):
        thr_p = splat16(pat)
        one, zero = jnp.int16(1), jnp.int16(0)

        def body(c, acc):
            h0 = pl.multiple_of(c * HALF, HALF)
            k = pltpu.bitcast(scr[pl.ds(h0, HALF), :], jnp.int16)
            parts = []
            for j in range(CHUNK // 16):
                kj = k[16 * j:16 * (j + 1)]
                parts.append(jnp.where((kj > thr_p) if strict else (kj >= thr_p), one, zero))
            while len(parts) > 1:
                parts = [a + b for a, b in zip(parts[::2], parts[1::2])] + parts[len(parts) & ~1:]
            return acc + parts[0].astype(I32)
        acc = lax.fori_loop(0, nch, body, jnp.zeros((16, BLOCK_Q), I32))
        return jnp.sum(acc, axis=0, keepdims=True)

    def hi_bit(it, carry):
        t_hi, cnt_t = carry
        cand = t_hi | jnp.left_shift(jnp.int32(1), 15 - it)
        cnt = count16(hi_s, cand ^ 0x8000)
        ok = cnt >= topk
        return jnp.where(ok, cand, t_hi), jnp.where(ok, cnt, cnt_t)

    zero_q = jnp.zeros((1, BLOCK_Q), I32)
    t_hi, cnt_t = lax.fori_loop(0, 16, hi_bit, (zero_q, jnp.full((1, BLOCK_Q), lp, I32)))
    thr_hi_p = splat16(t_hi ^ 0x8000)
    cnt_gt = count16(hi_s, t_hi ^ 0x8000, strict=True)

    def lo_prep(c, carry):
        h0 = pl.multiple_of(c * HALF, HALF)
        hi = pltpu.bitcast(hi_s[pl.ds(h0, HALF), :], jnp.int16)
        lo = pltpu.bitcast(lo_s[pl.ds(h0, HALF), :], jnp.int16)
        parts = [jnp.where(hi[16 * j:16 * (j + 1)] == thr_hi_p, lo[16 * j:16 * (j + 1)],
                           jnp.int16(-0x8000)) for j in range(CHUNK // 16)]
        lo2_s[pl.ds(h0, HALF), :] = pltpu.bitcast(jnp.concatenate(parts, axis=0), I32)
        return carry

    lax.fori_loop(0, nch, lo_prep, 0)

    def lo_bit(it, carry):
        t_lo, cnt_t = carry
        cand = t_lo | jnp.left_shift(jnp.int32(1), 15 - it)
        cnt = cnt_gt + count16(lo2_s, cand ^ 0x8000)
        ok = cnt >= topk
        return jnp.where(ok, cand, t_lo), jnp.where(ok, cnt, cnt_t)

    t_lo, cnt_t = lax.fori_loop(0, 16, lo_bit, (zero_q, cnt_t))
    thr = ((t_hi << 16) | t_lo) ^ INT_MIN

    j_s[...] = jnp.full((1, BLOCK_Q), lp, I32)
    tie_cut = (cnt_t > topk) & (thr > KEY_NEG_INF)
    any_cut = jnp.max(jnp.where(tie_cut, 1, 0))

    @pl.when(any_cut > 0)
    def _():
        need = topk - count(lambda k, p: k > thr)

        def jbit(it, j):
            cand = j | jnp.left_shift(jnp.int32(1), 12 - it)
            cnt = count(lambda k, p: (k == thr) & (p < cand))
            return jnp.where(cnt < need, cand, j)

        j = lax.fori_loop(0, 13, jbit, jnp.zeros((1, BLOCK_Q), I32))
        j_s[...] = jnp.where(tie_cut, j, lp)

    j_last = j_s[...]

    acc_s[...] = jnp.zeros_like(acc_s)

    def set_mask(c):
        k0 = pl.multiple_of(c * CHUNK, CHUNK)
        key = keys_s[pl.ds(k0, CHUNK), :]
        pos_k = k0 + row
        sel = (key > thr) | ((key == thr) & (pos_k <= j_last))
        sel = sel & (pos_k <= pos_q)
        madd_s[...] = jnp.where(sel, 0.0, MASKED)

    def scores(c, h, near):
        k0 = pl.multiple_of(c * CHUNK, CHUNK)
        pair = slice((h // 2) * LANES, (h // 2 + 1) * LANES)
        s = jnp.dot(ka_ref[0, pl.ds(k0, CHUNK), pair], qat_ref[0, 0, h * LANES:(h + 1) * LANES, :],
                    preferred_element_type=F32) + madd_s[...]
        if near:
            parts = []
            for sub in range(BLOCKS_PER_CHUNK):
                dl = i - (c * BLOCKS_PER_CHUNK + sub)
                parts.append(jnp.where(dl == 0, bias_s[h, 0],
                                       jnp.where(dl == 1, bias_s[h, 1], 0.0)))
            s = s + jnp.concatenate(parts, axis=0)
        s_scr[h] = s
        return jnp.max(s, axis=0, keepdims=True)

    def att_body(c, carry, lookahead):
        ms, ls, cms = carry
        if lookahead is not None:
            set_mask(c + 1)
        new_ms, new_ls, new_cms = [], [], []
        for h in range(HEADS):
            m_new = jnp.maximum(ms[h], cms[h])
            alpha = jnp.exp2(ms[h] - m_new)
            p = jnp.exp2(s_scr[h] - m_new)
            new_ls.append(alpha * ls[h] + jnp.sum(p, axis=0, keepdims=True))
            new_ms.append(m_new)
            rows = slice(h * A_HEAD_DIM, (h + 1) * A_HEAD_DIM)
            pv = jnp.dot(vat_ref[0, c, rows, :], p.astype(BF16), preferred_element_type=F32)
            acc_s[rows, :] = alpha * acc_s[rows, :] + pv
            if lookahead is not None:
                new_cms.append(scores(c + 1, h, near=lookahead))
        return tuple(new_ms), tuple(new_ls), tuple(new_cms)

    set_mask(0)
    init = (tuple(jnp.full((1, BLOCK_Q), MASKED, F32) for _ in range(HEADS)),
            tuple(jnp.zeros((1, BLOCK_Q), F32) for _ in range(HEADS)),
            tuple(scores(0, h, near=True) for h in range(HEADS)))
    n_far = jnp.maximum(nch - 3, 0)
    carry = lax.fori_loop(0, n_far, functools.partial(att_body, lookahead=False), init)
    carry = lax.fori_loop(n_far, nch - 1, functools.partial(att_body, lookahead=True), carry)
    ms, ls, _ = att_body(nch - 1, carry, lookahead=None)

    outs = []
    for h in range(HEADS):
        rows = slice(h * A_HEAD_DIM, (h + 1) * A_HEAD_DIM)
        outs.append(acc_s[rows, :] * (1.0 / ls[h]))
    oa_ref[0] = jnp.concatenate(outs, axis=0).T


def _mla_kernel(kb_ref, vbt_ref, qbt_ref, ob_ref, s_scr, acc_s):
    i = pl.program_id(1)
    nch = i // BLOCKS_PER_CHUNK + 1
    row = lax.broadcasted_iota(I32, (CHUNK, BLOCK_Q), 0)
    pos_q = i * BLOCK_Q + lax.broadcasted_iota(I32, (CHUNK, BLOCK_Q), 1)
    acc_s[...] = jnp.zeros_like(acc_s)

    def scores(c, h, diagonal):
        k0 = pl.multiple_of(c * CHUNK, CHUNK)
        grp = slice(h * LANES, (h + 1) * LANES)
        s = jnp.dot(kb_ref[0, pl.ds(k0, CHUNK), grp], qbt_ref[0, 0, grp, :],
                    preferred_element_type=F32)
        if diagonal:
            s = jnp.where(k0 + row <= pos_q, s, MASKED)
        s_scr[h] = s
        return jnp.max(s, axis=0, keepdims=True)

    def body(c, carry, lookahead):
        ms, ls, cms = carry
        new_ms, new_ls, new_cms = [], [], []
        for h in range(HEADS):
            m_new = jnp.maximum(ms[h], cms[h])
            alpha = jnp.exp2(ms[h] - m_new)
            p = jnp.exp2(s_scr[h] - m_new)
            new_ls.append(alpha * ls[h] + jnp.sum(p, axis=0, keepdims=True))
            new_ms.append(m_new)
            rows = slice(h * B_V, (h + 1) * B_V)
            pv = jnp.dot(vbt_ref[0, c, rows, :], p.astype(BF16), preferred_element_type=F32)
            acc_s[rows, :] = alpha * acc_s[rows, :] + pv
            if lookahead is not None:
                new_cms.append(scores(c + 1, h, diagonal=lookahead))
        return tuple(new_ms), tuple(new_ls), tuple(new_cms)

    init = (tuple(jnp.full((1, BLOCK_Q), MASKED, F32) for _ in range(HEADS)),
            tuple(jnp.zeros((1, BLOCK_Q), F32) for _ in range(HEADS)),
            tuple(scores(0, h, diagonal=True) for h in range(HEADS)))
    n_full = jnp.maximum(nch - 2, 0)
    carry = lax.fori_loop(0, n_full, functools.partial(body, lookahead=False), init)
    carry = lax.fori_loop(n_full, nch - 1, functools.partial(body, lookahead=True), carry)
    ms, ls, _ = body(nch - 1, carry, lookahead=None)

    outs = []
    for h in range(HEADS):
        rows = slice(h * B_V, (h + 1) * B_V)
        outs.append(acc_s[rows, :] * (1.0 / ls[h]))
    ob_ref[0] = jnp.concatenate(outs, axis=0).T


def _silu(x):
    return x * (1.0 / (1.0 + jnp.exp(-x)))


def _out_kernel(x_ref, lng_ref, lnb_ref, oa_ref, ob_ref, g_ref, wout_ref, pg_ref, pb_ref, out_ref):
    h = _layer_norm(x_ref[0], lng_ref[...], lnb_ref[...])
    g = g_ref[0]
    mixed = jnp.concatenate([oa_ref[0] * _silu(g[:, :A_WIDTH]), ob_ref[0] * _silu(g[:, A_WIDTH:])],
                            axis=-1)
    out = jnp.dot(mixed.astype(BF16), wout_ref[...], preferred_element_type=F32)
    out_ref[0] = _layer_norm(ALPHA * h + out, pg_ref[...], pb_ref[...])


def _rot_cols(w):
    half = w.shape[-1] // 2
    return jnp.concatenate([-w[:, half:], w[:, :half]], axis=-1)


def _full(shape):
    return pl.BlockSpec(shape, lambda *_: (0,) * len(shape))


def kernel(x, meta_tokens, ln_emb_g, ln_emb_b, w_in, w_uq, q_norm_g, w_ukv, kv_norm_g, rel_bias,
           w_out, ln_post_g, ln_post_b):
    B, S, D = x.shape
    assert w_in.shape[0] == DEPTH == 1
    L = N_META + S
    n_blk = -(-L // BLOCK_Q)
    assert n_blk % BLOCKS_PER_CHUNK == 0
    n_ch = n_blk // BLOCKS_PER_CHUNK
    lp = n_blk * BLOCK_Q
    topk = min(TOPK_MAX, S // 4)
    assert CHUNK >= topk and lp < (1 << 13)

    w = w_in[0]
    zeros = lambda n: jnp.zeros((D, n), F32)
    w_kpe = w[:, O_KPE:O_GB]
    pad_pe = lambda m: jnp.concatenate([zeros(B_NOPE), m, zeros(LANES - B_NOPE - B_ROPE)], axis=1)
    wnn = jnp.concatenate([w[:, O_KA:O_VA], w[:, O_GA:O_QI], w[:, O_GB:O_END], w[:, O_CQ:O_CKV],
                           w[:, O_CKV:O_KPE], pad_pe(w_kpe), pad_pe(_rot_cols(w_kpe)),
                           w[:, O_KI:O_WI]], axis=1).astype(BF16)
    assert wnn.shape[1] == NN_END
    qa_groups = []
    for h in range(HEADS):
        wh = w[:, O_QA + h * A_HEAD_DIM:O_QA + (h + 1) * A_HEAD_DIM]
        qa_groups += [zeros(A_HEAD_DIM), wh] if h % 2 else [wh, zeros(A_HEAD_DIM)]
    wnt = jnp.concatenate(qa_groups + [w[:, O_VA:O_GA], w[:, O_QI:O_KI], w[:, O_WI:O_CQ], zeros(8)],
                          axis=1).T.astype(BF16)
    assert wnt.shape[0] == NT_END

    wq = w_uq[0]
    d_q = B_NOPE + B_ROPE
    uq, uqr = [], []
    zq = lambda n: jnp.zeros((Q_LORA, n), F32)
    for h in range(HEADS):
        wh = wq[:, h * d_q:(h + 1) * d_q]
        uq += [wh, zq(LANES - d_q)]
        uqr += [zq(B_NOPE), _rot_cols(wh[:, B_NOPE:]), zq(LANES - d_q)]
    wuq_t = jnp.concatenate(uq, axis=1).T.astype(BF16)
    wuqr_t = jnp.concatenate(uqr, axis=1).T.astype(BF16)
    wkv = w_ukv[0]
    d_kv = B_NOPE + B_V
    wk_groups, wv_groups = [], []
    for h in range(HEADS):
        wk_groups += [wkv[:, h * d_kv:h * d_kv + B_NOPE], jnp.zeros((KV_LORA, LANES - B_NOPE), F32)]
        wv_groups.append(wkv[:, h * d_kv + B_NOPE:(h + 1) * d_kv])
    wk = jnp.concatenate(wk_groups, axis=1).astype(BF16)
    wv_t = jnp.concatenate(wv_groups, axis=1).T.astype(BF16)

    pos = jnp.arange(lp, dtype=F32)
    inv_freq = ROPE_THETA ** (-jnp.arange(0, B_ROPE, 2, dtype=F32) / B_ROPE)
    ang = pos[:, None] * inv_freq[None, :]
    cos2 = jnp.concatenate([jnp.cos(ang)] * 2, axis=1)
    sin2 = jnp.concatenate([jnp.sin(ang)] * 2, axis=1)
    zl = lambda n: jnp.zeros((lp, n), F32)
    cos128 = jnp.concatenate([zl(B_NOPE), cos2, zl(LANES - d_q)], axis=1)
    sin128 = jnp.concatenate([zl(B_NOPE), sin2, zl(LANES - d_q)], axis=1)
    scale_b = d_q ** -0.5 * LOG2_E
    ct = (jnp.concatenate([jnp.ones((lp, B_NOPE), F32), cos2, zl(LANES - d_q)], axis=1) * scale_b).T
    st = (sin128 * scale_b).T

    row2 = lambda v: v.reshape(1, -1).astype(F32)
    hp = jnp.concatenate([jnp.broadcast_to(meta_tokens[None].astype(x.dtype), (B, N_META, D)), x,
                          jnp.zeros((B, lp - L, D), x.dtype)], axis=1)

    params = pltpu.CompilerParams(dimension_semantics=("arbitrary", "arbitrary"),
                                  vmem_limit_bytes=VMEM_LIMIT)
    sds = jax.ShapeDtypeStruct

    chunk_rows = lambda n: pl.BlockSpec((1, CHUNK, n), lambda b, c: (b, c, 0))
    per_block = lambda r, n: pl.BlockSpec((1, BLOCKS_PER_CHUNK, r, n), lambda b, c: (b, c, 0, 0))
    per_chunk_t = pl.BlockSpec((1, 1, A_WIDTH, CHUNK), lambda b, c: (b, c, 0, 0))
    ka, ki, gates, kb, qat, qit, wit, vat, qbt, vbt = pl.pallas_call(
        functools.partial(_proj_kernel, seq_len=L),
        grid=(B, n_ch),
        in_specs=[chunk_rows(D), _full((1, D)), _full((1, D)), _full((D, NN_END)), _full((NT_END, D)),
                  _full((HEADS * LANES, Q_LORA)), _full((HEADS * LANES, Q_LORA)),
                  _full((KV_LORA, HEADS * LANES)), _full((B_WIDTH, KV_LORA)),
                  _full((1, Q_LORA)), _full((1, KV_LORA)),
                  pl.BlockSpec((CHUNK, LANES), lambda b, c: (c, 0)),
                  pl.BlockSpec((CHUNK, LANES), lambda b, c: (c, 0)),
                  pl.BlockSpec((LANES, CHUNK), lambda b, c: (0, c)),
                  pl.BlockSpec((LANES, CHUNK), lambda b, c: (0, c))],
        out_specs=[chunk_rows(A_WIDTH), chunk_rows(IDX_DIM), chunk_rows(A_WIDTH + B_WIDTH),
                   chunk_rows(HEADS * LANES), per_block(HEADS * LANES, BLOCK_Q),
                   per_block(IDX_DIM, HEADS * LANES), per_block(HEADS, BLOCK_Q), per_chunk_t,
                   per_block(HEADS * LANES, BLOCK_Q), per_chunk_t],
        out_shape=[sds((B, lp, A_WIDTH), BF16), sds((B, lp, IDX_DIM), BF16),
                   sds((B, lp, A_WIDTH + B_WIDTH), F32), sds((B, lp, HEADS * LANES), BF16),
                   sds((B, n_blk, HEADS * LANES, BLOCK_Q), BF16),
                   sds((B, n_blk, IDX_DIM, HEADS * LANES), BF16),
                   sds((B, n_blk, HEADS, BLOCK_Q), F32), sds((B, n_ch, A_WIDTH, CHUNK), BF16),
                   sds((B, n_blk, HEADS * LANES, BLOCK_Q), BF16),
                   sds((B, n_ch, B_WIDTH, CHUNK), BF16)],
        compiler_params=params, name="proj",
    )(hp, row2(ln_emb_g), row2(ln_emb_b), wnn, wnt, wuq_t, wuqr_t, wk, wv_t,
      row2(q_norm_g[0]), row2(kv_norm_g[0]), cos128, sin128, ct, st)

    batch_rows = lambda n: pl.BlockSpec((1, lp, n), lambda b, i: (b, 0, 0))
    batch_t = pl.BlockSpec((1, n_ch, A_WIDTH, CHUNK), lambda b, i: (b, 0, 0, 0))
    blk = lambda r, n: pl.BlockSpec((1, 1, r, n), lambda b, i: (b, i, 0, 0))
    out_blk = pl.BlockSpec((1, BLOCK_Q, A_WIDTH), lambda b, i: (b, i, 0))
    oa = pl.pallas_call(
        functools.partial(_dsa_kernel, topk=topk, lp=lp),
        grid=(B, n_blk),
        in_specs=[pl.BlockSpec(memory_space=pltpu.SMEM), batch_rows(A_WIDTH), batch_rows(IDX_DIM),
                  batch_t, blk(HEADS * LANES, BLOCK_Q), blk(IDX_DIM, HEADS * LANES),
                  blk(HEADS, BLOCK_Q)],
        out_specs=out_blk,
        out_shape=sds((B, lp, A_WIDTH), F32),
        scratch_shapes=[pltpu.VMEM((lp, BLOCK_Q), I32),
                        pltpu.VMEM((HEADS, 2, BLOCK_Q, BLOCK_Q), F32),
                        pltpu.VMEM((A_WIDTH, BLOCK_Q), F32),
                        pltpu.VMEM((1, BLOCK_Q), I32),
                        pltpu.VMEM((HEADS, CHUNK, BLOCK_Q), F32),
                        pltpu.VMEM((CHUNK, BLOCK_Q), F32),
                        pltpu.VMEM((lp // 2, BLOCK_Q), I32),
                        pltpu.VMEM((lp // 2, BLOCK_Q), I32),
                        pltpu.VMEM((lp // 2, BLOCK_Q), I32)],
        compiler_params=params, name="dsa",
    )(rel_bias.astype(F32), ka, ki, vat, qat, qit, wit)

    ob = pl.pallas_call(
        _mla_kernel,
        grid=(B, n_blk),
        in_specs=[batch_rows(HEADS * LANES), batch_t, blk(HEADS * LANES, BLOCK_Q)],
        out_specs=out_blk,
        out_shape=sds((B, lp, B_WIDTH), F32),
        scratch_shapes=[pltpu.VMEM((HEADS, CHUNK, BLOCK_Q), F32),
                        pltpu.VMEM((B_WIDTH, BLOCK_Q), F32)],
        compiler_params=params, name="mla",
    )(kb, vbt, qbt)

    out = pl.pallas_call(
        _out_kernel,
        grid=(B, n_ch),
        in_specs=[chunk_rows(D), _full((1, D)), _full((1, D)), chunk_rows(A_WIDTH),
                  chunk_rows(B_WIDTH), chunk_rows(A_WIDTH + B_WIDTH), _full((A_WIDTH + B_WIDTH, D)),
                  _full((1, D)), _full((1, D))],
        out_specs=chunk_rows(D),
        out_shape=sds((B, lp, D), x.dtype),
        compiler_params=params, name="out",
    )(hp, row2(ln_emb_g), row2(ln_emb_b), oa, ob, gates, w_out[0].astype(BF16),
      row2(ln_post_g[0]), row2(ln_post_b[0]))
    return out[:, N_META:N_META + S, :]
```

```python
import functools
import math

import jax
import jax.numpy as jnp
from jax import lax
from jax.experimental import pallas as pl
from jax.experimental.pallas import tpu as pltpu

F32 = jnp.float32
BF16 = jnp.bfloat16
I32 = jnp.int32
I16 = jnp.int16

N_META = 16
HEADS = 8
A_HEAD_DIM = 64
A_WIDTH = HEADS * A_HEAD_DIM
IDX_DIM = 64
TOPK_MAX = 256
B_NOPE = 64
B_ROPE = 32
B_V = 64
B_WIDTH = HEADS * B_V
Q_LORA = 256
KV_LORA = 128
ROPE_THETA = 10000.0
REL_BUCKETS = 32
REL_MAX_DIST = 128
LN_EPS = 1e-5
RMS_EPS = 1e-6
DEPTH = 1
ALPHA = (2.0 * DEPTH) ** 0.25

LANES = 128
TQ = 256
HALF = TQ // 2
META_ROWS = LANES
MASKED = -1e30
LOG2_E = math.log2(math.e)
INT_MIN = -2147483648
KEY_NEG_INF = -0x7F800000
VMEM_LIMIT = 56 * 1024 * 1024

_SPLITS = (A_WIDTH, A_WIDTH, A_WIDTH, A_WIDTH, HEADS * IDX_DIM, IDX_DIM, HEADS,
           Q_LORA, KV_LORA, B_ROPE, B_WIDTH)
_OFF = [0]
for _s in _SPLITS:
    _OFF.append(_OFF[-1] + _s)
(O_QA, O_KA, O_VA, O_GA, O_QI, O_KI, O_WI, O_CQ, O_CKV, O_KPE, O_GB, O_END) = _OFF

NN_KA = 0
NN_G = NN_KA + A_WIDTH
NN_CQ = NN_G + A_WIDTH + B_WIDTH
NN_CKV = NN_CQ + Q_LORA
NN_PE = NN_CKV + KV_LORA
NN_PER = NN_PE + LANES
NN_KI = NN_PER + LANES
NN_END = NN_KI + IDX_DIM
NT_QA = 0
NT_VA = NT_QA + HEADS * LANES
NT_QI = NT_VA + A_WIDTH
NT_WI = NT_QI + HEADS * IDX_DIM
NT_END = NT_WI + 16

_NT_DIMS = (((1,), (1,)), ((), ()))


def _dot_nt(a, b):
    return lax.dot_general(a, b, _NT_DIMS, preferred_element_type=F32)


def _layer_norm(x, g, b):
    mu = jnp.mean(x, axis=-1, keepdims=True)
    xc = x - mu
    var = jnp.mean(xc * xc, axis=-1, keepdims=True)
    return xc * lax.rsqrt(var + LN_EPS) * g + b


def _rms_norm(x, g):
    return x * lax.rsqrt(jnp.mean(x * x, axis=-1, keepdims=True) + RMS_EPS) * g


def _proj_kernel(x_ref, lng_ref, lnb_ref, wnn_ref, wnt_ref, wuq_ref, wuqr_ref, wk_ref, wv_ref,
                 qg_ref, kvg_ref, cos_ref, sin_ref, ct_ref, st_ref,
                 ka_ref, ki_ref, g_ref, kb_ref, qat_ref, qit_ref, wit_ref, vat_ref, qbt_ref,
                 vbt_ref, *, n_valid):
    rows = x_ref.shape[1]
    y = _layer_norm(x_ref[0], lng_ref[...], lnb_ref[...])
    if n_valid < rows:
        y = jnp.where(lax.broadcasted_iota(I32, (rows, 1), 0) < n_valid, y, 0.0)
    hb = y.astype(BF16)

    def nn(lo, hi):
        return jnp.dot(hb, wnn_ref[:, lo:hi], preferred_element_type=F32)

    def nt(lo, hi):
        return _dot_nt(wnt_ref[lo:hi, :], hb)

    ka_ref[0] = nn(NN_KA, NN_G).astype(BF16)
    g_ref[0] = nn(NN_G, NN_CQ)
    ki_ref[0] = nn(NN_KI, NN_END).astype(BF16)

    qat_ref[0, 0] = (nt(NT_QA, NT_VA) * (A_HEAD_DIM ** -0.5 * LOG2_E)).astype(BF16)
    vat_ref[0, 0] = nt(NT_VA, NT_QI).astype(BF16)
    qit = nt(NT_QI, NT_WI).astype(BF16)
    for h in range(HEADS):
        qit_ref[0, 0, :, h * rows:(h + 1) * rows] = qit[h * IDX_DIM:(h + 1) * IDX_DIM, :]
    wit = nt(NT_WI, NT_END) * (HEADS ** -0.5)
    wit = wit * (IDX_DIM ** -0.5)
    wit_ref[0, 0] = wit[0:HEADS, :]

    cqn = _rms_norm(nn(NN_CQ, NN_CKV), qg_ref[...]).astype(BF16)
    ckvn = _rms_norm(nn(NN_CKV, NN_PE), kvg_ref[...]).astype(BF16)
    qbt = _dot_nt(wuq_ref[...], cqn)
    qbrt = _dot_nt(wuqr_ref[...], cqn)
    ct = ct_ref[...]
    st = st_ref[...]
    for h in range(HEADS):
        grp = slice(h * LANES, (h + 1) * LANES)
        qbt_ref[0, 0, grp, :] = (qbt[grp] * ct + qbrt[grp] * st).astype(BF16)
    kn = jnp.dot(ckvn, wk_ref[...], preferred_element_type=F32)
    kpe = nn(NN_PE, NN_PER) * cos_ref[...] + nn(NN_PER, NN_KI) * sin_ref[...]
    for h in range(HEADS):
        cols = slice(h * LANES, (h + 1) * LANES)
        kb_ref[0, :, cols] = (kn[:, cols] + kpe).astype(BF16)
    vbt_ref[0, 0] = _dot_nt(wv_ref[...], ckvn).astype(BF16)


def _t5_bucket(d):
    max_exact = REL_BUCKETS // 2
    d = jnp.maximum(d, 0)
    d_f = jnp.maximum(d, 1).astype(F32)
    large = max_exact + (jnp.log(d_f / max_exact) / math.log(REL_MAX_DIST / max_exact)
                         * (REL_BUCKETS - max_exact)).astype(I32)
    large = jnp.minimum(large, REL_BUCKETS - 1)
    return jnp.where(d < max_exact, d, large)


def _dsa_kernel(relb_ref, ka_ref, ki_ref, vat_ref, qat_ref, qit_ref, wit_ref, kam_ref, vatm_ref,
                oa_ref, keys_s, bias_s, metab_s, acc_s, j_s, s_scr, madd_s, hi_s, lo_s, lo2_s,
                *, topk, seq):
    b = pl.program_id(0)
    i = pl.program_id(1)
    nch = i + 1
    row = lax.broadcasted_iota(I32, (TQ, TQ), 0)
    pos_q = i * TQ + lax.broadcasted_iota(I32, (TQ, TQ), 1)

    @pl.when((b == 0) & (i == 0))
    def _():
        def fill(dst, bucket, h):
            def step(bk, t):
                return jnp.where(bucket == bk, relb_ref[bk, h], t)
            t = lax.fori_loop(0, REL_BUCKETS, step, jnp.zeros(bucket.shape, F32))
            dst[...] = (t - relb_ref[REL_BUCKETS - 1, h]) * LOG2_E

        for dl in range(2):
            bucket = _t5_bucket(dl * TQ + pos_q - row)
            for h in range(HEADS):
                fill(bias_s.at[h, dl], bucket, h)
        r_m = lax.broadcasted_iota(I32, (META_ROWS, TQ), 0)
        q_m = lax.broadcasted_iota(I32, (META_ROWS, TQ), 1)
        bucket = _t5_bucket(N_META + q_m - r_m)
        for h in range(HEADS):
            fill(metab_s.at[h], bucket, h)

    def idx_body(c, carry):
        k0 = pl.multiple_of(c * TQ, TQ)
        rel = jnp.dot(ki_ref[0, pl.ds(k0, TQ), :], qit_ref[0, 0],
                      preferred_element_type=F32)
        s = jnp.zeros((TQ, TQ), F32)
        for h in range(HEADS):
            s = s + jnp.maximum(rel[:, h * TQ:(h + 1) * TQ], 0.0) * wit_ref[0, 0, h:h + 1, :]
        s = jnp.where(k0 + row <= pos_q, s, -jnp.inf)
        bits = lax.bitcast_convert_type(s, I32)
        key = jnp.where(bits < 0, INT_MIN - bits, bits)
        keys_s[pl.ds(k0, TQ), :] = key
        h0 = pl.multiple_of(c * HALF, HALF)
        ka_, kb_ = key[:HALF], key[HALF:]
        hi_s[pl.ds(h0, HALF), :] = ((ka_ >> 16) & 0xFFFF) | (kb_ & -0x10000)
        lo_s[pl.ds(h0, HALF), :] = ((ka_ & 0xFFFF) ^ 0x8000) | ((kb_ << 16) ^ INT_MIN)
        return carry

    lax.fori_loop(0, nch, idx_body, 0)

    def count(pred):
        def body(c, acc):
            k0 = pl.multiple_of(c * TQ, TQ)
            hit = pred(keys_s[pl.ds(k0, TQ), :], k0 + row)
            return acc + jnp.sum(jnp.where(hit, 1, 0).reshape(TQ // 8, 8, TQ), axis=0)
        acc = lax.fori_loop(0, nch, body, jnp.zeros((8, TQ), I32))
        return jnp.sum(acc, axis=0, keepdims=True)

    def splat16(pat):
        word = pat | (pat << 16)
        return pltpu.bitcast(jnp.broadcast_to(word, (8, TQ)), I16)

    def count16(scr, pat, strict=False):
        thr_p = splat16(pat)
        one, zero = jnp.int16(1), jnp.int16(0)

        def body(c, acc):
            h0 = pl.multiple_of(c * HALF, HALF)
            k = pltpu.bitcast(scr[pl.ds(h0, HALF), :], I16)
            parts = []
            for j in range(TQ // 16):
                kj = k[16 * j:16 * (j + 1)]
                parts.append(jnp.where((kj > thr_p) if strict else (kj >= thr_p), one, zero))
            while len(parts) > 1:
                parts = [x + y for x, y in zip(parts[::2], parts[1::2])] + parts[len(parts) & ~1:]
            return acc + parts[0].astype(I32)
        acc = lax.fori_loop(0, nch, body, jnp.zeros((16, TQ), I32))
        return jnp.sum(acc, axis=0, keepdims=True)

    def hi_bit(it, carry):
        t_hi, cnt_t = carry
        cand = t_hi | jnp.left_shift(jnp.int32(1), 15 - it)
        cnt = count16(hi_s, cand ^ 0x8000)
        ok = cnt >= topk
        return jnp.where(ok, cand, t_hi), jnp.where(ok, cnt, cnt_t)

    zero_q = jnp.zeros((1, TQ), I32)
    t_hi, cnt_t = lax.fori_loop(0, 16, hi_bit, (zero_q, jnp.full((1, TQ), seq, I32)))
    thr_hi_p = splat16(t_hi ^ 0x8000)
    cnt_gt = count16(hi_s, t_hi ^ 0x8000, strict=True)

    def lo_prep(c, carry):
        h0 = pl.multiple_of(c * HALF, HALF)
        hi = pltpu.bitcast(hi_s[pl.ds(h0, HALF), :], I16)
        lo = pltpu.bitcast(lo_s[pl.ds(h0, HALF), :], I16)
        parts = [jnp.where(hi[16 * j:16 * (j + 1)] == thr_hi_p, lo[16 * j:16 * (j + 1)],
                           jnp.int16(-0x8000)) for j in range(TQ // 16)]
        lo2_s[pl.ds(h0, HALF), :] = pltpu.bitcast(jnp.concatenate(parts, axis=0), I32)
        return carry

    lax.fori_loop(0, nch, lo_prep, 0)

    def lo_bit(it, carry):
        t_lo, cnt_t = carry
        cand = t_lo | jnp.left_shift(jnp.int32(1), 15 - it)
        cnt = cnt_gt + count16(lo2_s, cand ^ 0x8000)
        ok = cnt >= topk
        return jnp.where(ok, cand, t_lo), jnp.where(ok, cnt, cnt_t)

    t_lo, cnt_t = lax.fori_loop(0, 16, lo_bit, (zero_q, cnt_t))
    thr = ((t_hi << 16) | t_lo) ^ INT_MIN

    j_s[...] = jnp.full((1, TQ), seq, I32)
    tie_cut = (cnt_t > topk) & (thr > KEY_NEG_INF)
    any_cut = jnp.max(jnp.where(tie_cut, 1, 0))

    @pl.when(any_cut > 0)
    def _():
        need = topk - count(lambda k, p: k > thr)

        def jbit(it, j):
            cand = j | jnp.left_shift(jnp.int32(1), 12 - it)
            cnt = count(lambda k, p: (k == thr) & (p < cand))
            return jnp.where(cnt < need, cand, j)

        j = lax.fori_loop(0, 13, jbit, jnp.zeros((1, TQ), I32))
        j_s[...] = jnp.where(tie_cut, j, seq)

    j_last = j_s[...]

    def set_mask(c):
        k0 = pl.multiple_of(c * TQ, TQ)
        key = keys_s[pl.ds(k0, TQ), :]
        pos_k = k0 + row
        sel = (key > thr) | ((key == thr) & (pos_k <= j_last))
        sel = sel & (pos_k <= pos_q)
        madd_s[...] = jnp.where(sel, 0.0, MASKED)

    def scores(c, h, near):
        k0 = pl.multiple_of(c * TQ, TQ)
        pair = slice((h // 2) * LANES, (h // 2 + 1) * LANES)
        s = jnp.dot(ka_ref[0, pl.ds(k0, TQ), pair], qat_ref[0, 0, h * LANES:(h + 1) * LANES, :],
                    preferred_element_type=F32) + madd_s[...]
        if near:
            dl = i - c
            s = s + jnp.where(dl == 0, bias_s[h, 0], jnp.where(dl == 1, bias_s[h, 1], 0.0))
        s_scr[h] = s
        return jnp.max(s, axis=0, keepdims=True)

    def att_body(c, carry, lookahead):
        ms, ls, cms = carry
        if lookahead is not None:
            set_mask(c + 1)
        new_ms, new_ls, new_cms = [], [], []
        for h in range(HEADS):
            m_new = jnp.maximum(ms[h], cms[h])
            alpha = jnp.exp2(ms[h] - m_new)
            p = jnp.exp2(s_scr[h] - m_new)
            new_ls.append(alpha * ls[h] + jnp.sum(p, axis=0, keepdims=True))
            new_ms.append(m_new)
            rows = slice(h * A_HEAD_DIM, (h + 1) * A_HEAD_DIM)
            pv = jnp.dot(vat_ref[0, c, rows, :], p.astype(BF16), preferred_element_type=F32)
            acc_s[rows, :] = alpha * acc_s[rows, :] + pv
            if lookahead is not None:
                new_cms.append(scores(c + 1, h, near=lookahead))
        return tuple(new_ms), tuple(new_ls), tuple(new_cms)

    meta_ok = lax.broadcasted_iota(I32, (META_ROWS, TQ), 0) < N_META
    ms0, ls0 = [], []
    for h in range(HEADS):
        pair = slice((h // 2) * LANES, (h // 2 + 1) * LANES)
        s = jnp.dot(kam_ref[:, pair], qat_ref[0, 0, h * LANES:(h + 1) * LANES, :],
                    preferred_element_type=F32)
        s = jnp.where(meta_ok, s + jnp.where(i == 0, metab_s[h], 0.0), MASKED)
        m0 = jnp.max(s, axis=0, keepdims=True)
        p = jnp.exp2(s - m0)
        ms0.append(m0)
        ls0.append(jnp.sum(p, axis=0, keepdims=True))
        rows = slice(h * A_HEAD_DIM, (h + 1) * A_HEAD_DIM)
        acc_s[rows, :] = jnp.dot(vatm_ref[rows, :], p.astype(BF16), preferred_element_type=F32)

    set_mask(0)
    init = (tuple(ms0), tuple(ls0), tuple(scores(0, h, near=True) for h in range(HEADS)))
    n_far = jnp.maximum(nch - 3, 0)
    carry = lax.fori_loop(0, n_far, functools.partial(att_body, lookahead=False), init)
    carry = lax.fori_loop(n_far, nch - 1, functools.partial(att_body, lookahead=True), carry)
    ms, ls, _ = att_body(nch - 1, carry, lookahead=None)

    outs = []
    for h in range(HEADS):
        rows = slice(h * A_HEAD_DIM, (h + 1) * A_HEAD_DIM)
        outs.append(acc_s[rows, :] * (1.0 / ls[h]))
    oa_ref[0] = jnp.concatenate(outs, axis=0).T


def _mla_kernel(kb_ref, vbt_ref, qbt_ref, kbm_ref, vbtm_ref, ob_ref, s_scr, acc_s):
    i = pl.program_id(1)
    nch = i + 1
    row = lax.broadcasted_iota(I32, (TQ, TQ), 0)
    pos_q = i * TQ + lax.broadcasted_iota(I32, (TQ, TQ), 1)

    def scores(c, h, diagonal):
        k0 = pl.multiple_of(c * TQ, TQ)
        grp = slice(h * LANES, (h + 1) * LANES)
        s = jnp.dot(kb_ref[0, pl.ds(k0, TQ), grp], qbt_ref[0, 0, grp, :],
                    preferred_element_type=F32)
        if diagonal:
            s = jnp.where(k0 + row <= pos_q, s, MASKED)
        s_scr[h] = s
        return jnp.max(s, axis=0, keepdims=True)

    def body(c, carry, lookahead):
        ms, ls, cms = carry
        new_ms, new_ls, new_cms = [], [], []
        for h in range(HEADS):
            m_new = jnp.maximum(ms[h], cms[h])
            alpha = jnp.exp2(ms[h] - m_new)
            p = jnp.exp2(s_scr[h] - m_new)
            new_ls.append(alpha * ls[h] + jnp.sum(p, axis=0, keepdims=True))
            new_ms.append(m_new)
            rows = slice(h * B_V, (h + 1) * B_V)
            pv = jnp.dot(vbt_ref[0, c, rows, :], p.astype(BF16), preferred_element_type=F32)
            acc_s[rows, :] = alpha * acc_s[rows, :] + pv
            if lookahead is not None:
                new_cms.append(scores(c + 1, h, diagonal=lookahead))
        return tuple(new_ms), tuple(new_ls), tuple(new_cms)

    meta_ok = lax.broadcasted_iota(I32, (META_ROWS, TQ), 0) < N_META
    ms0, ls0 = [], []
    for h in range(HEADS):
        grp = slice(h * LANES, (h + 1) * LANES)
        s = jnp.dot(kbm_ref[:, grp], qbt_ref[0, 0, grp, :], preferred_element_type=F32)
        s = jnp.where(meta_ok, s, MASKED)
        m0 = jnp.max(s, axis=0, keepdims=True)
        p = jnp.exp2(s - m0)
        ms0.append(m0)
        ls0.append(jnp.sum(p, axis=0, keepdims=True))
        rows = slice(h * B_V, (h + 1) * B_V)
        acc_s[rows, :] = jnp.dot(vbtm_ref[rows, :], p.astype(BF16), preferred_element_type=F32)

    cms0 = tuple(scores(0, h, diagonal=True) for h in range(HEADS))
    init = (tuple(ms0), tuple(ls0), cms0)
    n_full = jnp.maximum(nch - 2, 0)
    carry = lax.fori_loop(0, n_full, functools.partial(body, lookahead=False), init)
    carry = lax.fori_loop(n_full, nch - 1, functools.partial(body, lookahead=True), carry)
    ms, ls, _ = body(nch - 1, carry, lookahead=None)

    outs = []
    for h in range(HEADS):
        rows = slice(h * B_V, (h + 1) * B_V)
        outs.append(acc_s[rows, :] * (1.0 / ls[h]))
    ob_ref[0] = jnp.concatenate(outs, axis=0).T


def _silu(x):
    return x * (1.0 / (1.0 + jnp.exp(-x)))


def _out_kernel(x_ref, lng_ref, lnb_ref, oa_ref, ob_ref, g_ref, wout_ref, pg_ref, pb_ref, out_ref):
    h = _layer_norm(x_ref[0], lng_ref[...], lnb_ref[...])
    g = g_ref[0]
    mixed = jnp.concatenate([oa_ref[0] * _silu(g[:, :A_WIDTH]), ob_ref[0] * _silu(g[:, A_WIDTH:])],
                            axis=-1)
    out = jnp.dot(mixed.astype(BF16), wout_ref[...], preferred_element_type=F32)
    out_ref[0] = _layer_norm(ALPHA * h + out, pg_ref[...], pb_ref[...])


def _rot_cols(w):
    half = w.shape[-1] // 2
    return jnp.concatenate([-w[:, half:], w[:, :half]], axis=-1)


def _full(shape):
    return pl.BlockSpec(shape, lambda *_: (0,) * len(shape))


def _rope_tables(pos, scale):
    d_q = B_NOPE + B_ROPE
    n = pos.shape[0]
    inv_freq = ROPE_THETA ** (-jnp.arange(0, B_ROPE, 2, dtype=F32) / B_ROPE)
    ang = pos[:, None] * inv_freq[None, :]
    cos2 = jnp.concatenate([jnp.cos(ang)] * 2, axis=1)
    sin2 = jnp.concatenate([jnp.sin(ang)] * 2, axis=1)
    zl = lambda k: jnp.zeros((n, k), F32)
    cos128 = jnp.concatenate([zl(B_NOPE), cos2, zl(LANES - d_q)], axis=1)
    sin128 = jnp.concatenate([zl(B_NOPE), sin2, zl(LANES - d_q)], axis=1)
    ct = (jnp.concatenate([jnp.ones((n, B_NOPE), F32), cos2, zl(LANES - d_q)], axis=1) * scale).T
    st = (sin128 * scale).T
    return cos128, sin128, ct, st


def kernel(x, meta_tokens, ln_emb_g, ln_emb_b, w_in, w_uq, q_norm_g, w_ukv, kv_norm_g, rel_bias,
           w_out, ln_post_g, ln_post_b):
    B, S, D = x.shape
    assert w_in.shape[0] == DEPTH == 1
    assert S % TQ == 0 and S < (1 << 13) and meta_tokens.shape[0] == N_META <= META_ROWS
    nq = S // TQ
    topk = min(TOPK_MAX, S // 4) - N_META
    assert 0 < topk <= TQ

    w = w_in[0]
    zeros = lambda n: jnp.zeros((D, n), F32)
    w_kpe = w[:, O_KPE:O_GB]
    pad_pe = lambda m: jnp.concatenate([zeros(B_NOPE), m, zeros(LANES - B_NOPE - B_ROPE)], axis=1)
    wnn = jnp.concatenate([w[:, O_KA:O_VA], w[:, O_GA:O_QI], w[:, O_GB:O_END], w[:, O_CQ:O_CKV],
                           w[:, O_CKV:O_KPE], pad_pe(w_kpe), pad_pe(_rot_cols(w_kpe)),
                           w[:, O_KI:O_WI]], axis=1).astype(BF16)
    assert wnn.shape[1] == NN_END
    qa_groups = []
    for h in range(HEADS):
        wh = w[:, O_QA + h * A_HEAD_DIM:O_QA + (h + 1) * A_HEAD_DIM]
        qa_groups += [zeros(A_HEAD_DIM), wh] if h % 2 else [wh, zeros(A_HEAD_DIM)]
    wnt = jnp.concatenate(qa_groups + [w[:, O_VA:O_GA], w[:, O_QI:O_KI], w[:, O_WI:O_CQ], zeros(8)],
                          axis=1).T.astype(BF16)
    assert wnt.shape[0] == NT_END

    wq = w_uq[0]
    d_q = B_NOPE + B_ROPE
    uq, uqr = [], []
    zq = lambda n: jnp.zeros((Q_LORA, n), F32)
    for h in range(HEADS):
        wh = wq[:, h * d_q:(h + 1) * d_q]
        uq += [wh, zq(LANES - d_q)]
        uqr += [zq(B_NOPE), _rot_cols(wh[:, B_NOPE:]), zq(LANES - d_q)]
    wuq_t = jnp.concatenate(uq, axis=1).T.astype(BF16)
    wuqr_t = jnp.concatenate(uqr, axis=1).T.astype(BF16)
    wkv = w_ukv[0]
    d_kv = B_NOPE + B_V
    wk_groups, wv_groups = [], []
    for h in range(HEADS):
        wk_groups += [wkv[:, h * d_kv:h * d_kv + B_NOPE], jnp.zeros((KV_LORA, LANES - B_NOPE), F32)]
        wv_groups.append(wkv[:, h * d_kv + B_NOPE:(h + 1) * d_kv])
    wk = jnp.concatenate(wk_groups, axis=1).astype(BF16)
    wv_t = jnp.concatenate(wv_groups, axis=1).T.astype(BF16)

    scale_b = d_q ** -0.5 * LOG2_E
    tabs_x = _rope_tables(N_META + jnp.arange(S, dtype=F32), scale_b)
    tabs_m = _rope_tables(jnp.arange(META_ROWS, dtype=F32), scale_b)

    row2 = lambda v: v.reshape(1, -1).astype(F32)
    meta_pad = jnp.concatenate([meta_tokens.astype(x.dtype),
                                jnp.zeros((META_ROWS - N_META, D), x.dtype)], axis=0)[None]

    params = pltpu.CompilerParams(dimension_semantics=("arbitrary", "arbitrary"),
                                  vmem_limit_bytes=VMEM_LIMIT)
    sds = jax.ShapeDtypeStruct

    def project(rows_in, tabs, nb, nblk, rows, n_valid):
        row_blk = lambda n: pl.BlockSpec((1, rows, n), lambda b, i: (b, i, 0))
        t_blk = lambda r, n: pl.BlockSpec((1, 1, r, n), lambda b, i: (b, i, 0, 0))
        return pl.pallas_call(
            functools.partial(_proj_kernel, n_valid=n_valid),
            grid=(nb, nblk),
            in_specs=[row_blk(D), _full((1, D)), _full((1, D)), _full((D, NN_END)),
                      _full((NT_END, D)), _full((HEADS * LANES, Q_LORA)),
                      _full((HEADS * LANES, Q_LORA)), _full((KV_LORA, HEADS * LANES)),
                      _full((B_WIDTH, KV_LORA)), _full((1, Q_LORA)), _full((1, KV_LORA)),
                      pl.BlockSpec((rows, LANES), lambda b, i: (i, 0)),
                      pl.BlockSpec((rows, LANES), lambda b, i: (i, 0)),
                      pl.BlockSpec((LANES, rows), lambda b, i: (0, i)),
                      pl.BlockSpec((LANES, rows), lambda b, i: (0, i))],
            out_specs=[row_blk(A_WIDTH), row_blk(IDX_DIM), row_blk(A_WIDTH + B_WIDTH),
                       row_blk(HEADS * LANES), t_blk(HEADS * LANES, rows),
                       t_blk(IDX_DIM, HEADS * rows), t_blk(HEADS, rows), t_blk(A_WIDTH, rows),
                       t_blk(HEADS * LANES, rows), t_blk(B_WIDTH, rows)],
            out_shape=[sds((nb, nblk * rows, A_WIDTH), BF16), sds((nb, nblk * rows, IDX_DIM), BF16),
                       sds((nb, nblk * rows, A_WIDTH + B_WIDTH), F32),
                       sds((nb, nblk * rows, HEADS * LANES), BF16),
                       sds((nb, nblk, HEADS * LANES, rows), BF16),
                       sds((nb, nblk, IDX_DIM, HEADS * rows), BF16),
                       sds((nb, nblk, HEADS, rows), F32), sds((nb, nblk, A_WIDTH, rows), BF16),
                       sds((nb, nblk, HEADS * LANES, rows), BF16),
                       sds((nb, nblk, B_WIDTH, rows), BF16)],
            compiler_params=params, name="proj",
        )(rows_in, row2(ln_emb_g), row2(ln_emb_b), wnn, wnt, wuq_t, wuqr_t, wk, wv_t,
          row2(q_norm_g[0]), row2(kv_norm_g[0]), *tabs)

    ka, ki, gates, kb, qat, qit, wit, vat, qbt, vbt = project(x, tabs_x, B, nq, TQ, TQ)
    ka_m, _, _, kb_m, _, _, _, vat_m, _, vbt_m = project(meta_pad, tabs_m, 1, 1, META_ROWS, N_META)
    ka_m, kb_m, vat_m, vbt_m = ka_m[0], kb_m[0], vat_m[0, 0], vbt_m[0, 0]

    batch_rows = lambda n: pl.BlockSpec((1, S, n), lambda b, i: (b, 0, 0))
    batch_t = pl.BlockSpec((1, nq, A_WIDTH, TQ), lambda b, i: (b, 0, 0, 0))
    blk = lambda r, n: pl.BlockSpec((1, 1, r, n), lambda b, i: (b, i, 0, 0))
    out_blk = pl.BlockSpec((1, TQ, A_WIDTH), lambda b, i: (b, i, 0))
    oa = pl.pallas_call(
        functools.partial(_dsa_kernel, topk=topk, seq=S),
        grid=(B, nq),
        in_specs=[pl.BlockSpec(memory_space=pltpu.SMEM), batch_rows(A_WIDTH), batch_rows(IDX_DIM),
                  batch_t, blk(HEADS * LANES, TQ), blk(IDX_DIM, HEADS * TQ), blk(HEADS, TQ),
                  _full((META_ROWS, A_WIDTH)), _full((A_WIDTH, META_ROWS))],
        out_specs=out_blk,
        out_shape=sds((B, S, A_WIDTH), F32),
        scratch_shapes=[pltpu.VMEM((S, TQ), I32),
                        pltpu.VMEM((HEADS, 2, TQ, TQ), F32),
                        pltpu.VMEM((HEADS, META_ROWS, TQ), F32),
                        pltpu.VMEM((A_WIDTH, TQ), F32),
                        pltpu.VMEM((1, TQ), I32),
                        pltpu.VMEM((HEADS, TQ, TQ), F32),
                        pltpu.VMEM((TQ, TQ), F32),
                        pltpu.VMEM((S // 2, TQ), I32),
                        pltpu.VMEM((S // 2, TQ), I32),
                        pltpu.VMEM((S // 2, TQ), I32)],
        compiler_params=params, name="dsa",
    )(rel_bias.astype(F32), ka, ki, vat, qat, qit, wit, ka_m, vat_m)

    ob = pl.pallas_call(
        _mla_kernel,
        grid=(B, nq),
        in_specs=[batch_rows(HEADS * LANES), batch_t, blk(HEADS * LANES, TQ),
                  _full((META_ROWS, HEADS * LANES)), _full((B_WIDTH, META_ROWS))],
        out_specs=out_blk,
        out_shape=sds((B, S, B_WIDTH), F32),
        scratch_shapes=[pltpu.VMEM((HEADS, TQ, TQ), F32),
                        pltpu.VMEM((B_WIDTH, TQ), F32)],
        compiler_params=params, name="mla",
    )(kb, vbt, qbt, kb_m, vbt_m)

    row_blk = lambda n: pl.BlockSpec((1, TQ, n), lambda b, i: (b, i, 0))
    return pl.pallas_call(
        _out_kernel,
        grid=(B, nq),
        in_specs=[row_blk(D), _full((1, D)), _full((1, D)), row_blk(A_WIDTH), row_blk(B_WIDTH),
                  row_blk(A_WIDTH + B_WIDTH), _full((A_WIDTH + B_WIDTH, D)), _full((1, D)),
                  _full((1, D))],
        out_specs=row_blk(D),
        out_shape=sds((B, S, D), x.dtype),
        compiler_params=params, name="out",
    )(x, row2(ln_emb_g), row2(ln_emb_b), oa, ob, gates, w_out[0].astype(BF16),
      row2(ln_post_g[0]), row2(ln_post_b[0]))
```

```python
import functools
import math

import jax
import jax.numpy as jnp
from jax import lax
from jax.experimental import pallas as pl
from jax.experimental.pallas import tpu as pltpu

F32 = jnp.float32
BF16 = jnp.bfloat16
I32 = jnp.int32
I16 = jnp.int16

N_META = 16
HEADS = 8
A_HEAD_DIM = 64
A_WIDTH = HEADS * A_HEAD_DIM
IDX_DIM = 64
TOPK_MAX = 256
B_NOPE = 64
B_ROPE = 32
B_V = 64
B_WIDTH = HEADS * B_V
Q_LORA = 256
KV_LORA = 128
ROPE_THETA = 10000.0
REL_BUCKETS = 32
REL_MAX_DIST = 128
LN_EPS = 1e-5
RMS_EPS = 1e-6
DEPTH = 1
ALPHA = (2.0 * DEPTH) ** 0.25

LANES = 128
TQ = 256
HALF = TQ // 2
META_ROWS = LANES
MASKED = -1e30
LOG2_E = math.log2(math.e)
INT_MIN = -2147483648
KEY_NEG_INF = -0x7F800000
LOWEST_I16_PAIR = -0x7FFF8000
VMEM_LIMIT = 56 * 1024 * 1024

_SPLITS = (A_WIDTH, A_WIDTH, A_WIDTH, A_WIDTH, HEADS * IDX_DIM, IDX_DIM, HEADS,
           Q_LORA, KV_LORA, B_ROPE, B_WIDTH)
_OFF = [0]
for _s in _SPLITS:
    _OFF.append(_OFF[-1] + _s)
(O_QA, O_KA, O_VA, O_GA, O_QI, O_KI, O_WI, O_CQ, O_CKV, O_KPE, O_GB, O_END) = _OFF

NN_KA = 0
NN_G = NN_KA + A_WIDTH
NN_CQ = NN_G + A_WIDTH + B_WIDTH
NN_CKV = NN_CQ + Q_LORA
NN_PE = NN_CKV + KV_LORA
NN_PER = NN_PE + LANES
NN_END = NN_PER + LANES
NT_QA = 0
NT_VA = NT_QA + A_WIDTH
NT_END = NT_VA + A_WIDTH
SEL_WI = HEADS * IDX_DIM
SEL_END = SEL_WI + 16

_NT_DIMS = (((1,), (1,)), ((), ()))


def _dot_nt(a, b):
    return lax.dot_general(a, b, _NT_DIMS, preferred_element_type=F32)


def _layer_norm(x, g, b):
    mu = jnp.mean(x, axis=-1, keepdims=True)
    xc = x - mu
    var = jnp.mean(xc * xc, axis=-1, keepdims=True)
    return xc * lax.rsqrt(var + LN_EPS) * g + b


def _rms_norm(x, g):
    return x * lax.rsqrt(jnp.mean(x * x, axis=-1, keepdims=True) + RMS_EPS) * g


def _proj_kernel(x_ref, lng_ref, lnb_ref, wnn_ref, wnt_ref, wki_ref, wsel_ref, wuq_ref, wk_ref,
                 wv_ref, qg_ref, kvg_ref, cos_ref, sin_ref, ct_ref, st_ref,
                 ka_ref, ki_ref, g_ref, kb_ref, qat_ref, qit_ref, wit_ref, vat_ref, qbt_ref,
                 vbt_ref, *, n_valid):
    rows = x_ref.shape[1]
    y = _layer_norm(x_ref[0], lng_ref[...], lnb_ref[...])
    if n_valid < rows:
        y = jnp.where(lax.broadcasted_iota(I32, (rows, 1), 0) < n_valid, y, 0.0)
    hb = y.astype(BF16)

    def nn(lo, hi):
        return jnp.dot(hb, wnn_ref[:, lo:hi], preferred_element_type=F32)

    def nt(lo, hi):
        return _dot_nt(wnt_ref[lo:hi, :], hb)

    ka_ref[0] = nn(NN_KA, NN_G).astype(BF16)
    g_ref[0] = nn(NN_G, NN_CQ)

    qat_ref[0, 0] = (nt(NT_QA, NT_VA) * (A_HEAD_DIM ** -0.5 * LOG2_E)).astype(BF16)
    vat_ref[0, 0] = nt(NT_VA, NT_END).astype(BF16)

    ki_ref[0] = jnp.dot(y, wki_ref[...], preferred_element_type=F32)
    qit = _dot_nt(wsel_ref[0:SEL_WI, :], y)
    for h in range(HEADS):
        qit_ref[0, 0, :, h * rows:(h + 1) * rows] = qit[h * IDX_DIM:(h + 1) * IDX_DIM, :]
    wit = _dot_nt(wsel_ref[SEL_WI:SEL_END, :], y) * (HEADS ** -0.5)
    wit = wit * (IDX_DIM ** -0.5)
    wit_ref[0, 0] = wit[0:HEADS, :]

    cqn = _rms_norm(nn(NN_CQ, NN_CKV), qg_ref[...]).astype(BF16)
    ckvn = _rms_norm(nn(NN_CKV, NN_PE), kvg_ref[...]).astype(BF16)
    qbt = _dot_nt(wuq_ref[...], cqn)
    ct = ct_ref[...]
    st = st_ref[...]
    half = B_ROPE // 2
    for h in range(HEADS):
        q = qbt[h * LANES:(h + 1) * LANES]
        x1, x2 = q[B_NOPE:B_NOPE + half], q[B_NOPE + half:B_NOPE + B_ROPE]
        rot = jnp.concatenate([jnp.zeros((B_NOPE, rows), F32), -x2, x1,
                               jnp.zeros((LANES - B_NOPE - B_ROPE, rows), F32)], axis=0)
        qbt_ref[0, 0, h * LANES:(h + 1) * LANES, :] = (q * ct + rot * st).astype(BF16)
    kn = jnp.dot(ckvn, wk_ref[...], preferred_element_type=F32)
    kpe = nn(NN_PE, NN_PER) * cos_ref[...] + nn(NN_PER, NN_END) * sin_ref[...]
    for h in range(HEADS):
        cols = slice(h * LANES, (h + 1) * LANES)
        kb_ref[0, :, cols] = (kn[:, cols] + kpe).astype(BF16)
    vbt_ref[0, 0] = _dot_nt(wv_ref[...], ckvn).astype(BF16)


_MAX_EXACT = REL_BUCKETS // 2
_BUCKET_STARTS = tuple(math.ceil(_MAX_EXACT * (REL_MAX_DIST / _MAX_EXACT) ** (k / (REL_BUCKETS - _MAX_EXACT)))
                       for k in range(1, REL_BUCKETS - _MAX_EXACT))


def _t5_bucket(d):
    d = jnp.maximum(d, 0)
    large = jnp.full(d.shape, _MAX_EXACT, I32)
    for start in _BUCKET_STARTS:
        large = large + jnp.where(d >= start, 1, 0)
    return jnp.where(d < _MAX_EXACT, d, large)


def _dsa_kernel(relb_ref, ka_ref, ki_ref, vat_ref, qat_ref, qit_ref, wit_ref, kam_ref, vatm_ref,
                oa_ref, keys_s, bias_s, metab_s, acc_s, qpad_s, s_scr, madd_s, hi_s, lo_s, lo2_s,
                *, topk, seq):
    b = pl.program_id(0)
    i = pl.program_id(1)
    nch = i + 1
    row = lax.broadcasted_iota(I32, (TQ, TQ), 0)
    pos_q = i * TQ + lax.broadcasted_iota(I32, (TQ, TQ), 1)

    @pl.when((b == 0) & (i == 0))
    def _():
        sub = lax.broadcasted_iota(I32, (8, TQ), 0)
        lane = lax.broadcasted_iota(I32, (8, TQ), 1)

        def fill(dst, n_rows, offset):
            def step(r8, carry):
                r0 = pl.multiple_of(r8 * 8, 8)
                bucket = _t5_bucket(offset + lane - (r0 + sub))
                for h in range(HEADS):
                    t = jnp.zeros((8, TQ), F32)
                    for bk in range(REL_BUCKETS):
                        t = jnp.where(bucket == bk, relb_ref[bk, h], t)
                    dst[h, pl.ds(r0, 8), :] = (t - relb_ref[REL_BUCKETS - 1, h]) * LOG2_E
                return carry
            lax.fori_loop(0, n_rows // 8, step, 0)

        for dl in range(2):
            fill(bias_s.at[:, dl], TQ, dl * TQ)
        fill(metab_s, META_ROWS, N_META)

    def idx_body(c, carry, diagonal):
        k0 = pl.multiple_of(c * TQ, TQ)
        rel = jnp.dot(ki_ref[0, pl.ds(k0, TQ), :], qit_ref[0, 0],
                      preferred_element_type=F32)
        s = jnp.zeros((TQ, TQ), F32)
        for h in range(HEADS):
            s = s + jnp.maximum(rel[:, h * TQ:(h + 1) * TQ], 0.0) * wit_ref[0, 0, h:h + 1, :]
        if diagonal:
            s = jnp.where(k0 + row <= pos_q, s, -jnp.inf)
        bits = lax.bitcast_convert_type(s, I32)
        key = jnp.where(bits < 0, INT_MIN - bits, bits)
        keys_s[pl.ds(k0, TQ), :] = key
        h0 = pl.multiple_of(c * HALF, HALF)
        ka_, kb_ = key[:HALF], key[HALF:]
        hi_s[pl.ds(h0, HALF), :] = ((ka_ >> 16) & 0xFFFF) | (kb_ & -0x10000)
        lo_s[pl.ds(h0, HALF), :] = ((ka_ & 0xFFFF) ^ 0x8000) | ((kb_ << 16) ^ INT_MIN)
        return carry

    lax.fori_loop(0, nch - 1, functools.partial(idx_body, diagonal=False), 0)
    idx_body(nch - 1, 0, diagonal=True)

    npair = (nch + 1) // 2

    @pl.when(nch < seq // TQ)
    def _():
        h0 = pl.multiple_of(nch * HALF, HALF)
        lowest = jnp.full((HALF, TQ), LOWEST_I16_PAIR, I32)
        hi_s[pl.ds(h0, HALF), :] = lowest
        lo_s[pl.ds(h0, HALF), :] = lowest

    def count(pred):
        def body(c, acc):
            k0 = pl.multiple_of(c * TQ, TQ)
            hit = pred(keys_s[pl.ds(k0, TQ), :], k0 + row)
            return acc + jnp.sum(jnp.where(hit, 1, 0).reshape(TQ // 8, 8, TQ), axis=0)
        acc = lax.fori_loop(0, nch, body, jnp.zeros((8, TQ), I32))
        return jnp.sum(acc, axis=0, keepdims=True)

    def splat16(pat):
        word = pat | (pat << 16)
        return pltpu.bitcast(jnp.broadcast_to(word, (8, TQ)), I16)

    def count16(scr, pat, strict=False):
        thr_p = splat16(pat)
        one, zero = jnp.int16(1), jnp.int16(0)

        def body(c2, acc):
            h0 = pl.multiple_of(c2 * TQ, TQ)
            k = pltpu.bitcast(scr[pl.ds(h0, TQ), :], I16)
            parts = []
            for j in range(2 * TQ // 16):
                kj = k[16 * j:16 * (j + 1)]
                parts.append(jnp.where((kj > thr_p) if strict else (kj >= thr_p), one, zero))
            while len(parts) > 1:
                parts = [x + y for x, y in zip(parts[::2], parts[1::2])] + parts[len(parts) & ~1:]
            return acc + parts[0].astype(I32)
        acc = lax.fori_loop(0, npair, body, jnp.zeros((16, TQ), I32))
        return jnp.sum(acc, axis=0, keepdims=True)

    def hi_bit(it, carry):
        t_hi, cnt_t = carry
        cand = t_hi | jnp.left_shift(jnp.int32(1), 15 - it)
        cnt = count16(hi_s, cand ^ 0x8000)
        ok = cnt >= topk
        return jnp.where(ok, cand, t_hi), jnp.where(ok, cnt, cnt_t)

    zero_q = jnp.zeros((1, TQ), I32)
    t_hi, cnt_t = lax.fori_loop(0, 16, hi_bit, (zero_q, jnp.full((1, TQ), seq, I32)))
    thr_hi_p = splat16(t_hi ^ 0x8000)
    cnt_gt = count16(hi_s, t_hi ^ 0x8000, strict=True)

    def lo_prep(c2, carry):
        h0 = pl.multiple_of(c2 * TQ, TQ)
        hi = pltpu.bitcast(hi_s[pl.ds(h0, TQ), :], I16)
        lo = pltpu.bitcast(lo_s[pl.ds(h0, TQ), :], I16)
        parts = [jnp.where(hi[16 * j:16 * (j + 1)] == thr_hi_p, lo[16 * j:16 * (j + 1)],
                           jnp.int16(-0x8000)) for j in range(2 * TQ // 16)]
        lo2_s[pl.ds(h0, TQ), :] = pltpu.bitcast(jnp.concatenate(parts, axis=0), I32)
        return carry

    lax.fori_loop(0, npair, lo_prep, 0)

    def lo_bit(it, carry):
        t_lo, cnt_t = carry
        cand = t_lo | jnp.left_shift(jnp.int32(1), 15 - it)
        cnt = cnt_gt + count16(lo2_s, cand ^ 0x8000)
        ok = cnt >= topk
        return jnp.where(ok, cand, t_lo), jnp.where(ok, cnt, cnt_t)

    t_lo, cnt_t = lax.fori_loop(0, 16, lo_bit, (zero_q, cnt_t))
    thr = ((t_hi << 16) | t_lo) ^ INT_MIN

    tie_cut = (cnt_t > topk) & (thr > KEY_NEG_INF)
    any_cut = jnp.max(jnp.where(tie_cut, 1, 0))

    @pl.when(any_cut > 0)
    def _():
        need = topk - count(lambda k, p: k > thr)

        def jbit(it, j):
            cand = j | jnp.left_shift(jnp.int32(1), 12 - it)
            cnt = count(lambda k, p: (k == thr) & (p < cand))
            return jnp.where(cnt < need, cand, j)

        j = lax.fori_loop(0, 13, jbit, jnp.zeros((1, TQ), I32))
        j_last = jnp.where(tie_cut, j, seq)

        def demote(c, carry):
            k0 = pl.multiple_of(c * TQ, TQ)
            key = keys_s[pl.ds(k0, TQ), :]
            keys_s[pl.ds(k0, TQ), :] = jnp.where((key == thr) & (k0 + row > j_last), key - 1, key)
            return carry

        lax.fori_loop(0, nch, demote, 0)

    thr_sel = jnp.maximum(thr, KEY_NEG_INF + 1)

    for h in range(HEADS):
        qh = qat_ref[0, 0, h * A_HEAD_DIM:(h + 1) * A_HEAD_DIM, :]
        zero_half = jnp.zeros_like(qh)
        qpad_s[h] = jnp.concatenate([zero_half, qh] if h % 2 else [qh, zero_half], axis=0)

    def set_mask(c):
        k0 = pl.multiple_of(c * TQ, TQ)
        madd_s[...] = jnp.where(keys_s[pl.ds(k0, TQ), :] >= thr_sel, 0.0, MASKED)

    def scores(c, h, near):
        k0 = pl.multiple_of(c * TQ, TQ)
        pair = slice((h // 2) * LANES, (h // 2 + 1) * LANES)
        s = jnp.dot(ka_ref[0, pl.ds(k0, TQ), pair], qpad_s[h],
                    preferred_element_type=F32) + madd_s[...]
        if near:
            dl = i - c
            s = s + jnp.where(dl == 0, bias_s[h, 0], jnp.where(dl == 1, bias_s[h, 1], 0.0))
        s_scr[h] = s
        return jnp.max(s, axis=0, keepdims=True)

    def att_body(c, carry, lookahead):
        ms, ls, cms = carry
        if lookahead is not None:
            set_mask(c + 1)
        new_ms, new_ls, new_cms = [], [], []
        for h in range(HEADS):
            m_new = jnp.maximum(ms[h], cms[h])
            alpha = jnp.exp2(ms[h] - m_new)
            p = jnp.exp2(s_scr[h] - m_new)
            new_ls.append(alpha * ls[h] + jnp.sum(p, axis=0, keepdims=True))
            new_ms.append(m_new)
            rows = slice(h * A_HEAD_DIM, (h + 1) * A_HEAD_DIM)
            pv = jnp.dot(vat_ref[0, c, rows, :], p.astype(BF16), preferred_element_type=F32)
            acc_s[rows, :] = alpha * acc_s[rows, :] + pv
            if lookahead is not None:
                new_cms.append(scores(c + 1, h, near=lookahead))
        return tuple(new_ms), tuple(new_ls), tuple(new_cms)

    meta_ok = lax.broadcasted_iota(I32, (META_ROWS, TQ), 0) < N_META
    ms0, ls0 = [], []
    for h in range(HEADS):
        pair = slice((h // 2) * LANES, (h // 2 + 1) * LANES)
        s = jnp.dot(kam_ref[:, pair], qpad_s[h], preferred_element_type=F32)
        s = jnp.where(meta_ok, s + jnp.where(i == 0, metab_s[h], 0.0), MASKED)
        m0 = jnp.max(s, axis=0, keepdims=True)
        p = jnp.exp2(s - m0)
        ms0.append(m0)
        ls0.append(jnp.sum(p, axis=0, keepdims=True))
        rows = slice(h * A_HEAD_DIM, (h + 1) * A_HEAD_DIM)
        acc_s[rows, :] = jnp.dot(vatm_ref[rows, :], p.astype(BF16), preferred_element_type=F32)

    set_mask(0)
    init = (tuple(ms0), tuple(ls0), tuple(scores(0, h, near=True) for h in range(HEADS)))
    n_far = jnp.maximum(nch - 3, 0)
    carry = lax.fori_loop(0, n_far, functools.partial(att_body, lookahead=False), init)
    carry = lax.fori_loop(n_far, nch - 1, functools.partial(att_body, lookahead=True), carry)
    ms, ls, _ = att_body(nch - 1, carry, lookahead=None)

    outs = []
    for h in range(HEADS):
        rows = slice(h * A_HEAD_DIM, (h + 1) * A_HEAD_DIM)
        outs.append(acc_s[rows, :] * (1.0 / ls[h]))
    oa_ref[0] = jnp.concatenate(outs, axis=0).T


def _mla_kernel(kb_ref, vbt_ref, qbt_ref, kbm_ref, vbtm_ref, ob_ref, s_scr, acc_s):
    i = pl.program_id(1)
    nch = i + 1
    row = lax.broadcasted_iota(I32, (TQ, TQ), 0)
    pos_q = i * TQ + lax.broadcasted_iota(I32, (TQ, TQ), 1)

    def scores(c, h, diagonal):
        k0 = pl.multiple_of(c * TQ, TQ)
        grp = slice(h * LANES, (h + 1) * LANES)
        s = jnp.dot(kb_ref[0, pl.ds(k0, TQ), grp], qbt_ref[0, 0, grp, :],
                    preferred_element_type=F32)
        if diagonal:
            s = jnp.where(k0 + row <= pos_q, s, MASKED)
        s_scr[h] = s
        return jnp.max(s, axis=0, keepdims=True)

    def body(c, carry, lookahead):
        ms, ls, cms = carry
        new_ms, new_ls, new_cms = [], [], []
        for h in range(HEADS):
            m_new = jnp.maximum(ms[h], cms[h])
            alpha = jnp.exp2(ms[h] - m_new)
            p = jnp.exp2(s_scr[h] - m_new)
            new_ls.append(alpha * ls[h] + jnp.sum(p, axis=0, keepdims=True))
            new_ms.append(m_new)
            rows = slice(h * B_V, (h + 1) * B_V)
            pv = jnp.dot(vbt_ref[0, c, rows, :], p.astype(BF16), preferred_element_type=F32)
            acc_s[rows, :] = alpha * acc_s[rows, :] + pv
            if lookahead is not None:
                new_cms.append(scores(c + 1, h, diagonal=lookahead))
        return tuple(new_ms), tuple(new_ls), tuple(new_cms)

    meta_ok = lax.broadcasted_iota(I32, (META_ROWS, TQ), 0) < N_META
    ms0, ls0 = [], []
    for h in range(HEADS):
        grp = slice(h * LANES, (h + 1) * LANES)
        s = jnp.dot(kbm_ref[:, grp], qbt_ref[0, 0, grp, :], preferred_element_type=F32)
        s = jnp.where(meta_ok, s, MASKED)
        m0 = jnp.max(s, axis=0, keepdims=True)
        p = jnp.exp2(s - m0)
        ms0.append(m0)
        ls0.append(jnp.sum(p, axis=0, keepdims=True))
        rows = slice(h * B_V, (h + 1) * B_V)
        acc_s[rows, :] = jnp.dot(vbtm_ref[rows, :], p.astype(BF16), preferred_element_type=F32)

    cms0 = tuple(scores(0, h, diagonal=True) for h in range(HEADS))
    init = (tuple(ms0), tuple(ls0), cms0)
    n_full = jnp.maximum(nch - 2, 0)
    carry = lax.fori_loop(0, n_full, functools.partial(body, lookahead=False), init)
    carry = lax.fori_loop(n_full, nch - 1, functools.partial(body, lookahead=True), carry)
    ms, ls, _ = body(nch - 1, carry, lookahead=None)

    outs = []
    for h in range(HEADS):
        rows = slice(h * B_V, (h + 1) * B_V)
        outs.append(acc_s[rows, :] * (1.0 / ls[h]))
    ob_ref[0] = jnp.concatenate(outs, axis=0).T


def _silu(x):
    return x * (1.0 / (1.0 + jnp.exp(-x)))


def _out_kernel(x_ref, lng_ref, lnb_ref, oa_ref, ob_ref, g_ref, wout_ref, pg_ref, pb_ref, out_ref):
    h = _layer_norm(x_ref[0], lng_ref[...], lnb_ref[...])
    g = g_ref[0]
    mixed = jnp.concatenate([oa_ref[0] * _silu(g[:, :A_WIDTH]), ob_ref[0] * _silu(g[:, A_WIDTH:])],
                            axis=-1)
    out = jnp.dot(mixed.astype(BF16), wout_ref[...], preferred_element_type=F32)
    out_ref[0] = _layer_norm(ALPHA * h + out, pg_ref[...], pb_ref[...])


def _rot_cols(w):
    half = w.shape[-1] // 2
    return jnp.concatenate([-w[:, half:], w[:, :half]], axis=-1)


def _full(shape):
    return pl.BlockSpec(shape, lambda *_: (0,) * len(shape))


def _rope_tables(pos, scale):
    d_q = B_NOPE + B_ROPE
    n = pos.shape[0]
    inv_freq = ROPE_THETA ** (-jnp.arange(0, B_ROPE, 2, dtype=F32) / B_ROPE)
    ang = pos[:, None] * inv_freq[None, :]
    cos2 = jnp.concatenate([jnp.cos(ang)] * 2, axis=1)
    sin2 = jnp.concatenate([jnp.sin(ang)] * 2, axis=1)
    zl = lambda k: jnp.zeros((n, k), F32)
    cos128 = jnp.concatenate([zl(B_NOPE), cos2, zl(LANES - d_q)], axis=1)
    sin128 = jnp.concatenate([zl(B_NOPE), sin2, zl(LANES - d_q)], axis=1)
    ct = (jnp.concatenate([jnp.ones((n, B_NOPE), F32), cos2, zl(LANES - d_q)], axis=1) * scale).T
    st = (sin128 * scale).T
    return cos128, sin128, ct, st


def kernel(x, meta_tokens, ln_emb_g, ln_emb_b, w_in, w_uq, q_norm_g, w_ukv, kv_norm_g, rel_bias,
           w_out, ln_post_g, ln_post_b):
    B, S, D = x.shape
    assert w_in.shape[0] == DEPTH == 1
    assert S % TQ == 0 and S < (1 << 13) and meta_tokens.shape[0] == N_META <= META_ROWS
    nq = S // TQ
    assert nq % 2 == 0
    topk = min(TOPK_MAX, S // 4) - N_META
    assert 0 < topk <= TQ

    w = w_in[0]
    zeros = lambda n: jnp.zeros((D, n), F32)
    w_kpe = w[:, O_KPE:O_GB]
    pad_pe = lambda m: jnp.concatenate([zeros(B_NOPE), m, zeros(LANES - B_NOPE - B_ROPE)], axis=1)
    wnn = jnp.concatenate([w[:, O_KA:O_VA], w[:, O_GA:O_QI], w[:, O_GB:O_END], w[:, O_CQ:O_CKV],
                           w[:, O_CKV:O_KPE], pad_pe(w_kpe), pad_pe(_rot_cols(w_kpe))],
                          axis=1).astype(BF16)
    assert wnn.shape[1] == NN_END
    wnt = jnp.concatenate([w[:, O_QA:O_KA], w[:, O_VA:O_GA]], axis=1).T.astype(BF16)
    assert wnt.shape[0] == NT_END
    wki = w[:, O_KI:O_WI].astype(F32)
    wsel = jnp.concatenate([w[:, O_QI:O_KI], w[:, O_WI:O_CQ], zeros(8)], axis=1).T.astype(F32)
    assert wsel.shape[0] == SEL_END

    wq = w_uq[0]
    d_q = B_NOPE + B_ROPE
    uq = []
    for h in range(HEADS):
        uq += [wq[:, h * d_q:(h + 1) * d_q], jnp.zeros((Q_LORA, LANES - d_q), F32)]
    wuq_t = jnp.concatenate(uq, axis=1).T.astype(BF16)
    wkv = w_ukv[0]
    d_kv = B_NOPE + B_V
    wk_groups, wv_groups = [], []
    for h in range(HEADS):
        wk_groups += [wkv[:, h * d_kv:h * d_kv + B_NOPE], jnp.zeros((KV_LORA, LANES - B_NOPE), F32)]
        wv_groups.append(wkv[:, h * d_kv + B_NOPE:(h + 1) * d_kv])
    wk = jnp.concatenate(wk_groups, axis=1).astype(BF16)
    wv_t = jnp.concatenate(wv_groups, axis=1).T.astype(BF16)

    scale_b = d_q ** -0.5 * LOG2_E
    tabs_x = _rope_tables(N_META + jnp.arange(S, dtype=F32), scale_b)
    tabs_m = _rope_tables(jnp.arange(META_ROWS, dtype=F32), scale_b)

    row2 = lambda v: v.reshape(1, -1).astype(F32)
    meta_pad = jnp.concatenate([meta_tokens.astype(x.dtype),
                                jnp.zeros((META_ROWS - N_META, D), x.dtype)], axis=0)[None]

    params = pltpu.CompilerParams(dimension_semantics=("arbitrary", "arbitrary"),
                                  vmem_limit_bytes=VMEM_LIMIT)
    sds = jax.ShapeDtypeStruct

    def project(rows_in, tabs, nb, nblk, rows, n_valid):
        row_blk = lambda n: pl.BlockSpec((1, rows, n), lambda b, i: (b, i, 0))
        t_blk = lambda r, n: pl.BlockSpec((1, 1, r, n), lambda b, i: (b, i, 0, 0))
        return pl.pallas_call(
            functools.partial(_proj_kernel, n_valid=n_valid),
            grid=(nb, nblk),
            in_specs=[row_blk(D), _full((1, D)), _full((1, D)), _full((D, NN_END)),
                      _full((NT_END, D)), _full((D, IDX_DIM)), _full((SEL_END, D)),
                      _full((HEADS * LANES, Q_LORA)), _full((KV_LORA, HEADS * LANES)),
                      _full((B_WIDTH, KV_LORA)), _full((1, Q_LORA)), _full((1, KV_LORA)),
                      pl.BlockSpec((rows, LANES), lambda b, i: (i, 0)),
                      pl.BlockSpec((rows, LANES), lambda b, i: (i, 0)),
                      pl.BlockSpec((LANES, rows), lambda b, i: (0, i)),
                      pl.BlockSpec((LANES, rows), lambda b, i: (0, i))],
            out_specs=[row_blk(A_WIDTH), row_blk(IDX_DIM), row_blk(A_WIDTH + B_WIDTH),
                       row_blk(HEADS * LANES), t_blk(A_WIDTH, rows),
                       t_blk(IDX_DIM, HEADS * rows), t_blk(HEADS, rows), t_blk(A_WIDTH, rows),
                       t_blk(HEADS * LANES, rows), t_blk(B_WIDTH, rows)],
            out_shape=[sds((nb, nblk * rows, A_WIDTH), BF16), sds((nb, nblk * rows, IDX_DIM), F32),
                       sds((nb, nblk * rows, A_WIDTH + B_WIDTH), F32),
                       sds((nb, nblk * rows, HEADS * LANES), BF16),
                       sds((nb, nblk, A_WIDTH, rows), BF16),
                       sds((nb, nblk, IDX_DIM, HEADS * rows), F32),
                       sds((nb, nblk, HEADS, rows), F32), sds((nb, nblk, A_WIDTH, rows), BF16),
                       sds((nb, nblk, HEADS * LANES, rows), BF16),
                       sds((nb, nblk, B_WIDTH, rows), BF16)],
            compiler_params=params, name="proj",
        )(rows_in, row2(ln_emb_g), row2(ln_emb_b), wnn, wnt, wki, wsel, wuq_t, wk, wv_t,
          row2(q_norm_g[0]), row2(kv_norm_g[0]), *tabs)

    ka, ki, gates, kb, qat, qit, wit, vat, qbt, vbt = project(x, tabs_x, B, nq, TQ, TQ)
    ka_m, _, _, kb_m, _, _, _, vat_m, _, vbt_m = project(meta_pad, tabs_m, 1, 1, META_ROWS, N_META)
    ka_m, kb_m, vat_m, vbt_m = ka_m[0], kb_m[0], vat_m[0, 0], vbt_m[0, 0]

    batch_rows = lambda n: pl.BlockSpec((1, S, n), lambda b, i: (b, 0, 0))
    batch_t = pl.BlockSpec((1, nq, A_WIDTH, TQ), lambda b, i: (b, 0, 0, 0))
    blk = lambda r, n: pl.BlockSpec((1, 1, r, n), lambda b, i: (b, i, 0, 0))
    out_blk = pl.BlockSpec((1, TQ, A_WIDTH), lambda b, i: (b, i, 0))
    oa = pl.pallas_call(
        functools.partial(_dsa_kernel, topk=topk, seq=S),
        grid=(B, nq),
        in_specs=[pl.BlockSpec(memory_space=pltpu.SMEM), batch_rows(A_WIDTH), batch_rows(IDX_DIM),
                  batch_t, blk(A_WIDTH, TQ), blk(IDX_DIM, HEADS * TQ), blk(HEADS, TQ),
                  _full((META_ROWS, A_WIDTH)), _full((A_WIDTH, META_ROWS))],
        out_specs=out_blk,
        out_shape=sds((B, S, A_WIDTH), F32),
        scratch_shapes=[pltpu.VMEM((S, TQ), I32),
                        pltpu.VMEM((HEADS, 2, TQ, TQ), F32),
                        pltpu.VMEM((HEADS, META_ROWS, TQ), F32),
                        pltpu.VMEM((A_WIDTH, TQ), F32),
                        pltpu.VMEM((HEADS, LANES, TQ), BF16),
                        pltpu.VMEM((HEADS, TQ, TQ), F32),
                        pltpu.VMEM((TQ, TQ), F32),
                        pltpu.VMEM((S // 2, TQ), I32),
                        pltpu.VMEM((S // 2, TQ), I32),
                        pltpu.VMEM((S // 2, TQ), I32)],
        compiler_params=params, name="dsa",
    )(rel_bias.astype(F32), ka, ki, vat, qat, qit, wit, ka_m, vat_m)

    ob = pl.pallas_call(
        _mla_kernel,
        grid=(B, nq),
        in_specs=[batch_rows(HEADS * LANES), batch_t, blk(HEADS * LANES, TQ),
                  _full((META_ROWS, HEADS * LANES)), _full((B_WIDTH, META_ROWS))],
        out_specs=out_blk,
        out_shape=sds((B, S, B_WIDTH), F32),
        scratch_shapes=[pltpu.VMEM((HEADS, TQ, TQ), F32),
                        pltpu.VMEM((B_WIDTH, TQ), F32)],
        compiler_params=params, name="mla",
    )(kb, vbt, qbt, kb_m, vbt_m)

    row_blk = lambda n: pl.BlockSpec((1, TQ, n), lambda b, i: (b, i, 0))
    return pl.pallas_call(
        _out_kernel,
        grid=(B, nq),
        in_specs=[row_blk(D), _full((1, D)), _full((1, D)), row_blk(A_WIDTH), row_blk(B_WIDTH),
                  row_blk(A_WIDTH + B_WIDTH), _full((A_WIDTH + B_WIDTH, D)), _full((1, D)),
                  _full((1, D))],
        out_specs=row_blk(D),
        out_shape=sds((B, S, D), x.dtype),
        compiler_params=params, name="out",
    )(x, row2(ln_emb_g), row2(ln_emb_b), oa, ob, gates, w_out[0].astype(BF16),
      row2(ln_post_g[0]), row2(ln_post_b[0]))
```

```python
import functools
import math

import jax
import jax.numpy as jnp
from jax import lax
from jax.experimental import pallas as pl
from jax.experimental.pallas import tpu as pltpu

F32 = jnp.float32
BF16 = jnp.bfloat16
I32 = jnp.int32
I16 = jnp.int16

N_META = 16
HEADS = 8
A_HEAD_DIM = 64
A_WIDTH = HEADS * A_HEAD_DIM
IDX_DIM = 64
TOPK_MAX = 256
B_NOPE = 64
B_ROPE = 32
B_V = 64
B_WIDTH = HEADS * B_V
Q_LORA = 256
KV_LORA = 128
ROPE_THETA = 10000.0
REL_BUCKETS = 32
REL_MAX_DIST = 128
LN_EPS = 1e-5
RMS_EPS = 1e-6
DEPTH = 1
ALPHA = (2.0 * DEPTH) ** 0.25

LANES = 128
TQ = 256
HALF = TQ // 2
META_ROWS = 16
META_PROJ_ROWS = LANES
MASKED = -1e30
LOG2_E = math.log2(math.e)
INT_MIN = -2147483648
KEY_NEG_INF = -0x7F800000
LOWEST_I16_PAIR = -0x7FFF8000
VMEM_LIMIT = 56 * 1024 * 1024

_SPLITS = (A_WIDTH, A_WIDTH, A_WIDTH, A_WIDTH, HEADS * IDX_DIM, IDX_DIM, HEADS,
           Q_LORA, KV_LORA, B_ROPE, B_WIDTH)
_OFF = [0]
for _s in _SPLITS:
    _OFF.append(_OFF[-1] + _s)
(O_QA, O_KA, O_VA, O_GA, O_QI, O_KI, O_WI, O_CQ, O_CKV, O_KPE, O_GB, O_END) = _OFF

NN_KA = 0
NN_G = NN_KA + A_WIDTH
NN_CQ = NN_G + A_WIDTH + B_WIDTH
NN_CKV = NN_CQ + Q_LORA
NN_PE = NN_CKV + KV_LORA
NN_PER = NN_PE + LANES
NN_END = NN_PER + LANES
NT_QA = 0
NT_VA = NT_QA + A_WIDTH
NT_END = NT_VA + A_WIDTH
SEL_WI = HEADS * IDX_DIM
SEL_END = SEL_WI + 16

_NT_DIMS = (((1,), (1,)), ((), ()))


def _dot_nt(a, b):
    return lax.dot_general(a, b, _NT_DIMS, preferred_element_type=F32)


def _layer_norm(x, g, b):
    mu = jnp.mean(x, axis=-1, keepdims=True)
    xc = x - mu
    var = jnp.mean(xc * xc, axis=-1, keepdims=True)
    return xc * lax.rsqrt(var + LN_EPS) * g + b


def _rms_norm(x, g):
    return x * lax.rsqrt(jnp.mean(x * x, axis=-1, keepdims=True) + RMS_EPS) * g


def _proj_kernel(x_ref, lng_ref, lnb_ref, wnn_ref, wnt_ref, wki_ref, wsel_ref, wuq_ref, wk_ref,
                 wv_ref, qg_ref, kvg_ref, cos_ref, sin_ref, ct_ref, st_ref,
                 ka_ref, ki_ref, g_ref, kb_ref, qat_ref, qit_ref, wit_ref, vat_ref, qbt_ref,
                 vbt_ref, *, n_valid):
    rows = x_ref.shape[1]
    y = _layer_norm(x_ref[0], lng_ref[...], lnb_ref[...])
    if n_valid < rows:
        y = jnp.where(lax.broadcasted_iota(I32, (rows, 1), 0) < n_valid, y, 0.0)
    hb = y.astype(BF16)

    def nn(lo, hi):
        return jnp.dot(hb, wnn_ref[:, lo:hi], preferred_element_type=F32)

    def nt(lo, hi):
        return _dot_nt(wnt_ref[lo:hi, :], hb)

    ka_ref[0] = nn(NN_KA, NN_G).astype(BF16)
    g_ref[0] = nn(NN_G, NN_CQ)

    qat_ref[0, 0] = (nt(NT_QA, NT_VA) * (A_HEAD_DIM ** -0.5 * LOG2_E)).astype(BF16)
    vat_ref[0, 0] = nt(NT_VA, NT_END).astype(BF16)

    ki_ref[0] = jnp.dot(y, wki_ref[...], preferred_element_type=F32)
    qit = _dot_nt(wsel_ref[0:SEL_WI, :], y)
    for h in range(HEADS):
        qit_ref[0, 0, :, h * rows:(h + 1) * rows] = qit[h * IDX_DIM:(h + 1) * IDX_DIM, :]
    wit = _dot_nt(wsel_ref[SEL_WI:SEL_END, :], y) * (HEADS ** -0.5)
    wit = wit * (IDX_DIM ** -0.5)
    wit_ref[0, 0] = wit[0:HEADS, :]

    cqn = _rms_norm(nn(NN_CQ, NN_CKV), qg_ref[...]).astype(BF16)
    ckvn = _rms_norm(nn(NN_CKV, NN_PE), kvg_ref[...]).astype(BF16)
    qbt = _dot_nt(wuq_ref[...], cqn)
    ct = ct_ref[...]
    st = st_ref[...]
    half = B_ROPE // 2
    for h in range(HEADS):
        q = qbt[h * LANES:(h + 1) * LANES]
        x1, x2 = q[B_NOPE:B_NOPE + half], q[B_NOPE + half:B_NOPE + B_ROPE]
        rot = jnp.concatenate([jnp.zeros((B_NOPE, rows), F32), -x2, x1,
                               jnp.zeros((LANES - B_NOPE - B_ROPE, rows), F32)], axis=0)
        qbt_ref[0, 0, h * LANES:(h + 1) * LANES, :] = (q * ct + rot * st).astype(BF16)
    kn = jnp.dot(ckvn, wk_ref[...], preferred_element_type=F32)
    kpe = nn(NN_PE, NN_PER) * cos_ref[...] + nn(NN_PER, NN_END) * sin_ref[...]
    for h in range(HEADS):
        cols = slice(h * LANES, (h + 1) * LANES)
        kb_ref[0, :, cols] = (kn[:, cols] + kpe).astype(BF16)
    vbt_ref[0, 0] = _dot_nt(wv_ref[...], ckvn).astype(BF16)


_MAX_EXACT = REL_BUCKETS // 2
_BUCKET_STARTS = tuple(math.ceil(_MAX_EXACT * (REL_MAX_DIST / _MAX_EXACT) ** (k / (REL_BUCKETS - _MAX_EXACT)))
                       for k in range(1, REL_BUCKETS - _MAX_EXACT))


def _t5_bucket(d):
    d = jnp.maximum(d, 0)
    large = jnp.full(d.shape, _MAX_EXACT, I32)
    for start in _BUCKET_STARTS:
        large = large + jnp.where(d >= start, 1, 0)
    return jnp.where(d < _MAX_EXACT, d, large)


def _dsa_kernel(relb_ref, ka_ref, ki_ref, vat_ref, qat_ref, qit_ref, wit_ref, kam_ref, vatm_ref,
                oa_ref, keys_s, bias_s, metab_s, acc_s, qpad_s, s_scr, madd_s, hi_s, lo_s, lo2_s,
                *, topk, seq):
    b = pl.program_id(0)
    i = pl.program_id(1)
    nch = i + 1
    row = lax.broadcasted_iota(I32, (TQ, TQ), 0)
    pos_q = i * TQ + lax.broadcasted_iota(I32, (TQ, TQ), 1)

    @pl.when((b == 0) & (i == 0))
    def _():
        sub = lax.broadcasted_iota(I32, (8, TQ), 0)
        lane = lax.broadcasted_iota(I32, (8, TQ), 1)

        def fill(dst, n_rows, offset):
            def step(r8, carry):
                r0 = pl.multiple_of(r8 * 8, 8)
                bucket = _t5_bucket(offset + lane - (r0 + sub))
                for h in range(HEADS):
                    t = jnp.zeros((8, TQ), F32)
                    for bk in range(REL_BUCKETS):
                        t = jnp.where(bucket == bk, relb_ref[bk, h], t)
                    dst[h, pl.ds(r0, 8), :] = (t - relb_ref[REL_BUCKETS - 1, h]) * LOG2_E
                return carry
            lax.fori_loop(0, n_rows // 8, step, 0)

        for dl in range(2):
            fill(bias_s.at[:, dl], TQ, dl * TQ)
        fill(metab_s, META_ROWS, N_META)

    def idx_body(c, carry, diagonal):
        k0 = pl.multiple_of(c * TQ, TQ)
        rel = jnp.dot(ki_ref[0, pl.ds(k0, TQ), :], qit_ref[0, 0],
                      preferred_element_type=F32)
        s = jnp.zeros((TQ, TQ), F32)
        for h in range(HEADS):
            s = s + jnp.maximum(rel[:, h * TQ:(h + 1) * TQ], 0.0) * wit_ref[0, 0, h:h + 1, :]
        if diagonal:
            s = jnp.where(k0 + row <= pos_q, s, -jnp.inf)
        bits = lax.bitcast_convert_type(s, I32)
        key = jnp.where(bits < 0, INT_MIN - bits, bits)
        keys_s[pl.ds(k0, TQ), :] = key
        h0 = pl.multiple_of(c * HALF, HALF)
        ka_, kb_ = key[:HALF], key[HALF:]
        hi_s[pl.ds(h0, HALF), :] = ((ka_ >> 16) & 0xFFFF) | (kb_ & -0x10000)
        lo_s[pl.ds(h0, HALF), :] = ((ka_ & 0xFFFF) ^ 0x8000) | ((kb_ << 16) ^ INT_MIN)
        return carry

    lax.fori_loop(0, nch - 1, functools.partial(idx_body, diagonal=False), 0)
    idx_body(nch - 1, 0, diagonal=True)

    npair = (nch + 1) // 2

    @pl.when(nch < seq // TQ)
    def _():
        h0 = pl.multiple_of(nch * HALF, HALF)
        lowest = jnp.full((HALF, TQ), LOWEST_I16_PAIR, I32)
        hi_s[pl.ds(h0, HALF), :] = lowest
        lo_s[pl.ds(h0, HALF), :] = lowest

    def count(pred):
        def body(c, acc):
            k0 = pl.multiple_of(c * TQ, TQ)
            hit = pred(keys_s[pl.ds(k0, TQ), :], k0 + row)
            return acc + jnp.sum(jnp.where(hit, 1, 0).reshape(TQ // 8, 8, TQ), axis=0)
        acc = lax.fori_loop(0, nch, body, jnp.zeros((8, TQ), I32))
        return jnp.sum(acc, axis=0, keepdims=True)

    def splat16(pat):
        word = pat | (pat << 16)
        return pltpu.bitcast(jnp.broadcast_to(word, (8, TQ)), I16)

    def count16(scr, pat, strict=False):
        thr_p = splat16(pat)
        one, zero = jnp.int16(1), jnp.int16(0)

        def body(c2, acc):
            h0 = pl.multiple_of(c2 * TQ, TQ)
            k = pltpu.bitcast(scr[pl.ds(h0, TQ), :], I16)
            parts = []
            for j in range(2 * TQ // 16):
                kj = k[16 * j:16 * (j + 1)]
                parts.append(jnp.where((kj > thr_p) if strict else (kj >= thr_p), one, zero))
            while len(parts) > 1:
                parts = [x + y for x, y in zip(parts[::2], parts[1::2])] + parts[len(parts) & ~1:]
            return acc + parts[0].astype(I32)
        acc = lax.fori_loop(0, npair, body, jnp.zeros((16, TQ), I32))
        return jnp.sum(acc, axis=0, keepdims=True)

    def hi_bit(it, carry):
        t_hi, cnt_t = carry
        cand = t_hi | jnp.left_shift(jnp.int32(1), 15 - it)
        cnt = count16(hi_s, cand ^ 0x8000)
        ok = cnt >= topk
        return jnp.where(ok, cand, t_hi), jnp.where(ok, cnt, cnt_t)

    zero_q = jnp.zeros((1, TQ), I32)
    t_hi, cnt_t = lax.fori_loop(0, 16, hi_bit, (zero_q, jnp.full((1, TQ), seq, I32)))
    thr_hi_p = splat16(t_hi ^ 0x8000)
    cnt_gt = count16(hi_s, t_hi ^ 0x8000, strict=True)

    def lo_prep(c2, carry):
        h0 = pl.multiple_of(c2 * TQ, TQ)
        hi = pltpu.bitcast(hi_s[pl.ds(h0, TQ), :], I16)
        lo = pltpu.bitcast(lo_s[pl.ds(h0, TQ), :], I16)
        parts = [jnp.where(hi[16 * j:16 * (j + 1)] == thr_hi_p, lo[16 * j:16 * (j + 1)],
                           jnp.int16(-0x8000)) for j in range(2 * TQ // 16)]
        lo2_s[pl.ds(h0, TQ), :] = pltpu.bitcast(jnp.concatenate(parts, axis=0), I32)
        return carry

    lax.fori_loop(0, npair, lo_prep, 0)

    def lo_bit(it, carry):
        t_lo, cnt_t = carry
        cand = t_lo | jnp.left_shift(jnp.int32(1), 15 - it)
        cnt = cnt_gt + count16(lo2_s, cand ^ 0x8000)
        ok = cnt >= topk
        return jnp.where(ok, cand, t_lo), jnp.where(ok, cnt, cnt_t)

    t_lo, cnt_t = lax.fori_loop(0, 16, lo_bit, (zero_q, cnt_t))
    thr = ((t_hi << 16) | t_lo) ^ INT_MIN

    tie_cut = (cnt_t > topk) & (thr > KEY_NEG_INF)
    any_cut = jnp.max(jnp.where(tie_cut, 1, 0))

    @pl.when(any_cut > 0)
    def _():
        need = topk - count(lambda k, p: k > thr)

        def jbit(it, j):
            cand = j | jnp.left_shift(jnp.int32(1), 12 - it)
            cnt = count(lambda k, p: (k == thr) & (p < cand))
            return jnp.where(cnt < need, cand, j)

        j = lax.fori_loop(0, 13, jbit, jnp.zeros((1, TQ), I32))
        j_last = jnp.where(tie_cut, j, seq)

        def demote(c, carry):
            k0 = pl.multiple_of(c * TQ, TQ)
            key = keys_s[pl.ds(k0, TQ), :]
            keys_s[pl.ds(k0, TQ), :] = jnp.where((key == thr) & (k0 + row > j_last), key - 1, key)
            return carry

        lax.fori_loop(0, nch, demote, 0)

    thr_sel = jnp.maximum(thr, KEY_NEG_INF + 1)

    for h in range(HEADS):
        qh = qat_ref[0, 0, h * A_HEAD_DIM:(h + 1) * A_HEAD_DIM, :]
        zero_half = jnp.zeros_like(qh)
        qpad_s[h] = jnp.concatenate([zero_half, qh] if h % 2 else [qh, zero_half], axis=0)

    def set_mask(c):
        k0 = pl.multiple_of(c * TQ, TQ)
        madd_s[...] = jnp.where(keys_s[pl.ds(k0, TQ), :] >= thr_sel, 0.0, MASKED)

    def scores(c, h, near):
        k0 = pl.multiple_of(c * TQ, TQ)
        pair = slice((h // 2) * LANES, (h // 2 + 1) * LANES)
        s = jnp.dot(ka_ref[0, pl.ds(k0, TQ), pair], qpad_s[h],
                    preferred_element_type=F32) + madd_s[...]
        if near:
            dl = i - c
            s = s + jnp.where(dl == 0, bias_s[h, 0], jnp.where(dl == 1, bias_s[h, 1], 0.0))
        s_scr[h] = s
        return jnp.max(s, axis=0, keepdims=True)

    def att_body(c, carry, lookahead):
        ms, ls, cms = carry
        if lookahead is not None:
            set_mask(c + 1)
        new_ms, new_ls, new_cms = [], [], []
        for h in range(HEADS):
            m_new = jnp.maximum(ms[h], cms[h])
            alpha = jnp.exp2(ms[h] - m_new)
            p = jnp.exp2(s_scr[h] - m_new)
            new_ls.append(alpha * ls[h] + jnp.sum(p, axis=0, keepdims=True))
            new_ms.append(m_new)
            rows = slice(h * A_HEAD_DIM, (h + 1) * A_HEAD_DIM)
            pv = jnp.dot(vat_ref[0, c, rows, :], p.astype(BF16), preferred_element_type=F32)
            acc_s[rows, :] = alpha * acc_s[rows, :] + pv
            if lookahead is not None:
                new_cms.append(scores(c + 1, h, near=lookahead))
        return tuple(new_ms), tuple(new_ls), tuple(new_cms)

    meta_ok = lax.broadcasted_iota(I32, (META_ROWS, TQ), 0) < N_META
    ms0, ls0 = [], []
    for h in range(HEADS):
        pair = slice((h // 2) * LANES, (h // 2 + 1) * LANES)
        s = jnp.dot(kam_ref[:, pair], qpad_s[h], preferred_element_type=F32)
        s = jnp.where(meta_ok, s + jnp.where(i == 0, metab_s[h], 0.0), MASKED)
        m0 = jnp.max(s, axis=0, keepdims=True)
        p = jnp.exp2(s - m0)
        ms0.append(m0)
        ls0.append(jnp.sum(p, axis=0, keepdims=True))
        rows = slice(h * A_HEAD_DIM, (h + 1) * A_HEAD_DIM)
        acc_s[rows, :] = jnp.dot(vatm_ref[rows, :], p.astype(BF16), preferred_element_type=F32)

    set_mask(0)
    init = (tuple(ms0), tuple(ls0), tuple(scores(0, h, near=True) for h in range(HEADS)))
    n_far = jnp.maximum(nch - 3, 0)
    carry = lax.fori_loop(0, n_far, functools.partial(att_body, lookahead=False), init)
    carry = lax.fori_loop(n_far, nch - 1, functools.partial(att_body, lookahead=True), carry)
    ms, ls, _ = att_body(nch - 1, carry, lookahead=None)

    outs = []
    for h in range(HEADS):
        rows = slice(h * A_HEAD_DIM, (h + 1) * A_HEAD_DIM)
        outs.append(acc_s[rows, :] * (1.0 / ls[h]))
    oa_ref[0] = jnp.concatenate(outs, axis=0).T


def _silu(x):
    return x * (1.0 / (1.0 + jnp.exp(-x)))


def _mla_out_kernel(kb_ref, vbt_ref, qbt_ref, kbm_ref, vbtm_ref, x_ref, lng_ref, lnb_ref, oa_ref,
                    g_ref, wout_ref, pg_ref, pb_ref, out_ref, s_scr, acc_s):
    i = pl.program_id(1)
    nch = i + 1
    row = lax.broadcasted_iota(I32, (TQ, TQ), 0)
    pos_q = i * TQ + lax.broadcasted_iota(I32, (TQ, TQ), 1)

    def scores(c, h, diagonal):
        k0 = pl.multiple_of(c * TQ, TQ)
        grp = slice(h * LANES, (h + 1) * LANES)
        s = jnp.dot(kb_ref[0, pl.ds(k0, TQ), grp], qbt_ref[0, 0, grp, :],
                    preferred_element_type=F32)
        if diagonal:
            s = jnp.where(k0 + row <= pos_q, s, MASKED)
        s_scr[h] = s
        return jnp.max(s, axis=0, keepdims=True)

    def body(c, carry, lookahead):
        ms, ls, cms = carry
        new_ms, new_ls, new_cms = [], [], []
        for h in range(HEADS):
            m_new = jnp.maximum(ms[h], cms[h])
            alpha = jnp.exp2(ms[h] - m_new)
            p = jnp.exp2(s_scr[h] - m_new)
            new_ls.append(alpha * ls[h] + jnp.sum(p, axis=0, keepdims=True))
            new_ms.append(m_new)
            rows = slice(h * B_V, (h + 1) * B_V)
            pv = jnp.dot(vbt_ref[0, c, rows, :], p.astype(BF16), preferred_element_type=F32)
            acc_s[rows, :] = alpha * acc_s[rows, :] + pv
            if lookahead is not None:
                new_cms.append(scores(c + 1, h, diagonal=lookahead))
        return tuple(new_ms), tuple(new_ls), tuple(new_cms)

    meta_ok = lax.broadcasted_iota(I32, (META_ROWS, TQ), 0) < N_META
    ms0, ls0 = [], []
    for h in range(HEADS):
        grp = slice(h * LANES, (h + 1) * LANES)
        s = jnp.dot(kbm_ref[:, grp], qbt_ref[0, 0, grp, :], preferred_element_type=F32)
        s = jnp.where(meta_ok, s, MASKED)
        m0 = jnp.max(s, axis=0, keepdims=True)
        p = jnp.exp2(s - m0)
        ms0.append(m0)
        ls0.append(jnp.sum(p, axis=0, keepdims=True))
        rows = slice(h * B_V, (h + 1) * B_V)
        acc_s[rows, :] = jnp.dot(vbtm_ref[rows, :], p.astype(BF16), preferred_element_type=F32)

    cms0 = tuple(scores(0, h, diagonal=True) for h in range(HEADS))
    init = (tuple(ms0), tuple(ls0), cms0)
    n_full = jnp.maximum(nch - 2, 0)
    carry = lax.fori_loop(0, n_full, functools.partial(body, lookahead=False), init)
    carry = lax.fori_loop(n_full, nch - 1, functools.partial(body, lookahead=True), carry)
    ms, ls, _ = body(nch - 1, carry, lookahead=None)

    outs = []
    for h in range(HEADS):
        rows = slice(h * B_V, (h + 1) * B_V)
        outs.append(acc_s[rows, :] * (1.0 / ls[h]))
    ob = jnp.concatenate(outs, axis=0).T

    h = _layer_norm(x_ref[0], lng_ref[...], lnb_ref[...])
    g = g_ref[0]
    mixed = jnp.concatenate([oa_ref[0] * _silu(g[:, :A_WIDTH]), ob * _silu(g[:, A_WIDTH:])], axis=-1)
    out = jnp.dot(mixed.astype(BF16), wout_ref[...], preferred_element_type=F32)
    out_ref[0] = _layer_norm(ALPHA * h + out, pg_ref[...], pb_ref[...])


def _rot_cols(w):
    half = w.shape[-1] // 2
    return jnp.concatenate([-w[:, half:], w[:, :half]], axis=-1)


def _full(shape):
    return pl.BlockSpec(shape, lambda *_: (0,) * len(shape))


def _rope_tables(pos, scale):
    d_q = B_NOPE + B_ROPE
    n = pos.shape[0]
    inv_freq = ROPE_THETA ** (-jnp.arange(0, B_ROPE, 2, dtype=F32) / B_ROPE)
    ang = pos[:, None] * inv_freq[None, :]
    cos2 = jnp.concatenate([jnp.cos(ang)] * 2, axis=1)
    sin2 = jnp.concatenate([jnp.sin(ang)] * 2, axis=1)
    zl = lambda k: jnp.zeros((n, k), F32)
    cos128 = jnp.concatenate([zl(B_NOPE), cos2, zl(LANES - d_q)], axis=1)
    sin128 = jnp.concatenate([zl(B_NOPE), sin2, zl(LANES - d_q)], axis=1)
    ct = (jnp.concatenate([jnp.ones((n, B_NOPE), F32), cos2, zl(LANES - d_q)], axis=1) * scale).T
    st = (sin128 * scale).T
    return cos128, sin128, ct, st


def kernel(x, meta_tokens, ln_emb_g, ln_emb_b, w_in, w_uq, q_norm_g, w_ukv, kv_norm_g, rel_bias,
           w_out, ln_post_g, ln_post_b):
    B, S, D = x.shape
    assert w_in.shape[0] == DEPTH == 1
    assert S % TQ == 0 and S < (1 << 13) and meta_tokens.shape[0] == N_META <= META_ROWS
    nq = S // TQ
    assert nq % 2 == 0
    topk = min(TOPK_MAX, S // 4) - N_META
    assert 0 < topk <= TQ

    w = w_in[0]
    zeros = lambda n: jnp.zeros((D, n), F32)
    w_kpe = w[:, O_KPE:O_GB]
    pad_pe = lambda m: jnp.concatenate([zeros(B_NOPE), m, zeros(LANES - B_NOPE - B_ROPE)], axis=1)
    wnn = jnp.concatenate([w[:, O_KA:O_VA], w[:, O_GA:O_QI], w[:, O_GB:O_END], w[:, O_CQ:O_CKV],
                           w[:, O_CKV:O_KPE], pad_pe(w_kpe), pad_pe(_rot_cols(w_kpe))],
                          axis=1).astype(BF16)
    assert wnn.shape[1] == NN_END
    wnt = jnp.concatenate([w[:, O_QA:O_KA], w[:, O_VA:O_GA]], axis=1).T.astype(BF16)
    assert wnt.shape[0] == NT_END
    wki = w[:, O_KI:O_WI].astype(F32)
    wsel = jnp.concatenate([w[:, O_QI:O_KI], w[:, O_WI:O_CQ], zeros(8)], axis=1).T.astype(F32)
    assert wsel.shape[0] == SEL_END

    wq = w_uq[0]
    d_q = B_NOPE + B_ROPE
    uq = []
    for h in range(HEADS):
        uq += [wq[:, h * d_q:(h + 1) * d_q], jnp.zeros((Q_LORA, LANES - d_q), F32)]
    wuq_t = jnp.concatenate(uq, axis=1).T.astype(BF16)
    wkv = w_ukv[0]
    d_kv = B_NOPE + B_V
    wk_groups, wv_groups = [], []
    for h in range(HEADS):
        wk_groups += [wkv[:, h * d_kv:h * d_kv + B_NOPE], jnp.zeros((KV_LORA, LANES - B_NOPE), F32)]
        wv_groups.append(wkv[:, h * d_kv + B_NOPE:(h + 1) * d_kv])
    wk = jnp.concatenate(wk_groups, axis=1).astype(BF16)
    wv_t = jnp.concatenate(wv_groups, axis=1).T.astype(BF16)

    scale_b = d_q ** -0.5 * LOG2_E
    tabs_x = _rope_tables(N_META + jnp.arange(S, dtype=F32), scale_b)
    tabs_m = _rope_tables(jnp.arange(META_PROJ_ROWS, dtype=F32), scale_b)

    row2 = lambda v: v.reshape(1, -1).astype(F32)
    meta_pad = jnp.concatenate([meta_tokens.astype(x.dtype),
                                jnp.zeros((META_PROJ_ROWS - N_META, D), x.dtype)], axis=0)[None]

    params = pltpu.CompilerParams(dimension_semantics=("arbitrary", "arbitrary"),
                                  vmem_limit_bytes=VMEM_LIMIT)
    sds = jax.ShapeDtypeStruct

    def project(rows_in, tabs, nb, nblk, rows, n_valid):
        row_blk = lambda n: pl.BlockSpec((1, rows, n), lambda b, i: (b, i, 0))
        t_blk = lambda r, n: pl.BlockSpec((1, 1, r, n), lambda b, i: (b, i, 0, 0))
        return pl.pallas_call(
            functools.partial(_proj_kernel, n_valid=n_valid),
            grid=(nb, nblk),
            in_specs=[row_blk(D), _full((1, D)), _full((1, D)), _full((D, NN_END)),
                      _full((NT_END, D)), _full((D, IDX_DIM)), _full((SEL_END, D)),
                      _full((HEADS * LANES, Q_LORA)), _full((KV_LORA, HEADS * LANES)),
                      _full((B_WIDTH, KV_LORA)), _full((1, Q_LORA)), _full((1, KV_LORA)),
                      pl.BlockSpec((rows, LANES), lambda b, i: (i, 0)),
                      pl.BlockSpec((rows, LANES), lambda b, i: (i, 0)),
                      pl.BlockSpec((LANES, rows), lambda b, i: (0, i)),
                      pl.BlockSpec((LANES, rows), lambda b, i: (0, i))],
            out_specs=[row_blk(A_WIDTH), row_blk(IDX_DIM), row_blk(A_WIDTH + B_WIDTH),
                       row_blk(HEADS * LANES), t_blk(A_WIDTH, rows),
                       t_blk(IDX_DIM, HEADS * rows), t_blk(HEADS, rows), t_blk(A_WIDTH, rows),
                       t_blk(HEADS * LANES, rows), t_blk(B_WIDTH, rows)],
            out_shape=[sds((nb, nblk * rows, A_WIDTH), BF16), sds((nb, nblk * rows, IDX_DIM), F32),
                       sds((nb, nblk * rows, A_WIDTH + B_WIDTH), F32),
                       sds((nb, nblk * rows, HEADS * LANES), BF16),
                       sds((nb, nblk, A_WIDTH, rows), BF16),
                       sds((nb, nblk, IDX_DIM, HEADS * rows), F32),
                       sds((nb, nblk, HEADS, rows), F32), sds((nb, nblk, A_WIDTH, rows), BF16),
                       sds((nb, nblk, HEADS * LANES, rows), BF16),
                       sds((nb, nblk, B_WIDTH, rows), BF16)],
            compiler_params=params, name="proj",
        )(rows_in, row2(ln_emb_g), row2(ln_emb_b), wnn, wnt, wki, wsel, wuq_t, wk, wv_t,
          row2(q_norm_g[0]), row2(kv_norm_g[0]), *tabs)

    ka, ki, gates, kb, qat, qit, wit, vat, qbt, vbt = project(x, tabs_x, B, nq, TQ, TQ)
    ka_m, _, _, kb_m, _, _, _, vat_m, _, vbt_m = project(meta_pad, tabs_m, 1, 1, META_PROJ_ROWS,
                                                         N_META)
    ka_m, kb_m = ka_m[0, :META_ROWS], kb_m[0, :META_ROWS]
    vat_m, vbt_m = vat_m[0, 0, :, :META_ROWS], vbt_m[0, 0, :, :META_ROWS]

    batch_rows = lambda n: pl.BlockSpec((1, S, n), lambda b, i: (b, 0, 0))
    batch_t = pl.BlockSpec((1, nq, A_WIDTH, TQ), lambda b, i: (b, 0, 0, 0))
    blk = lambda r, n: pl.BlockSpec((1, 1, r, n), lambda b, i: (b, i, 0, 0))
    out_blk = pl.BlockSpec((1, TQ, A_WIDTH), lambda b, i: (b, i, 0))
    oa = pl.pallas_call(
        functools.partial(_dsa_kernel, topk=topk, seq=S),
        grid=(B, nq),
        in_specs=[pl.BlockSpec(memory_space=pltpu.SMEM), batch_rows(A_WIDTH), batch_rows(IDX_DIM),
                  batch_t, blk(A_WIDTH, TQ), blk(IDX_DIM, HEADS * TQ), blk(HEADS, TQ),
                  _full((META_ROWS, A_WIDTH)), _full((A_WIDTH, META_ROWS))],
        out_specs=out_blk,
        out_shape=sds((B, S, A_WIDTH), F32),
        scratch_shapes=[pltpu.VMEM((S, TQ), I32),
                        pltpu.VMEM((HEADS, 2, TQ, TQ), F32),
                        pltpu.VMEM((HEADS, META_ROWS, TQ), F32),
                        pltpu.VMEM((A_WIDTH, TQ), F32),
                        pltpu.VMEM((HEADS, LANES, TQ), BF16),
                        pltpu.VMEM((HEADS, TQ, TQ), F32),
                        pltpu.VMEM((TQ, TQ), F32),
                        pltpu.VMEM((S // 2, TQ), I32),
                        pltpu.VMEM((S // 2, TQ), I32),
                        pltpu.VMEM((S // 2, TQ), I32)],
        compiler_params=params, name="dsa",
    )(rel_bias.astype(F32), ka, ki, vat, qat, qit, wit, ka_m, vat_m)

    row_blk = lambda n: pl.BlockSpec((1, TQ, n), lambda b, i: (b, i, 0))
    return pl.pallas_call(
        _mla_out_kernel,
        grid=(B, nq),
        in_specs=[batch_rows(HEADS * LANES), batch_t, blk(HEADS * LANES, TQ),
                  _full((META_ROWS, HEADS * LANES)), _full((B_WIDTH, META_ROWS)),
                  row_blk(D), _full((1, D)), _full((1, D)), row_blk(A_WIDTH),
                  row_blk(A_WIDTH + B_WIDTH), _full((A_WIDTH + B_WIDTH, D)), _full((1, D)),
                  _full((1, D))],
        out_specs=row_blk(D),
        out_shape=sds((B, S, D), x.dtype),
        scratch_shapes=[pltpu.VMEM((HEADS, TQ, TQ), F32),
                        pltpu.VMEM((B_WIDTH, TQ), F32)],
        compiler_params=params, name="mla_out",
    )(kb, vbt, qbt, kb_m, vbt_m, x, row2(ln_emb_g), row2(ln_emb_b), oa, gates,
      w_out[0].astype(BF16), row2(ln_post_g[0]), row2(ln_post_b[0]))
```

```python
import functools
import math

import jax
import jax.numpy as jnp
from jax import lax
from jax.experimental import pallas as pl
from jax.experimental.pallas import tpu as pltpu

F32 = jnp.float32
BF16 = jnp.bfloat16
I32 = jnp.int32
I16 = jnp.int16

N_META = 16
HEADS = 8
A_HEAD_DIM = 64
A_WIDTH = HEADS * A_HEAD_DIM
IDX_DIM = 64
TOPK_MAX = 256
B_NOPE = 64
B_ROPE = 32
B_V = 64
B_WIDTH = HEADS * B_V
Q_LORA = 256
KV_LORA = 128
ROPE_THETA = 10000.0
REL_BUCKETS = 32
REL_MAX_DIST = 128
LN_EPS = 1e-5
RMS_EPS = 1e-6
DEPTH = 1
ALPHA = (2.0 * DEPTH) ** 0.25

LANES = 128
TQ = 256
HALF = TQ // 2
SCAN = 2
META_ROWS = 16
META_PROJ_ROWS = LANES
MASKED = -1e30
LOG2_E = math.log2(math.e)
INT_MIN = -2147483648
KEY_NEG_INF = -0x7F800000
LOWEST_I16_PAIR = -0x7FFF8000
VMEM_LIMIT = 56 * 1024 * 1024

_SPLITS = (A_WIDTH, A_WIDTH, A_WIDTH, A_WIDTH, HEADS * IDX_DIM, IDX_DIM, HEADS,
           Q_LORA, KV_LORA, B_ROPE, B_WIDTH)
_OFF = [0]
for _s in _SPLITS:
    _OFF.append(_OFF[-1] + _s)
(O_QA, O_KA, O_VA, O_GA, O_QI, O_KI, O_WI, O_CQ, O_CKV, O_KPE, O_GB, O_END) = _OFF

NN_KA = 0
NN_G = NN_KA + A_WIDTH
NN_CQ = NN_G + A_WIDTH + B_WIDTH
NN_CKV = NN_CQ + Q_LORA
NN_PE = NN_CKV + KV_LORA
NN_PER = NN_PE + LANES
NN_END = NN_PER + LANES
NT_QA = 0
NT_VA = NT_QA + A_WIDTH
NT_END = NT_VA + A_WIDTH
SEL_WI = HEADS * IDX_DIM
SEL_END = SEL_WI + 16

_NT_DIMS = (((1,), (1,)), ((), ()))


def _dot_nt(a, b):
    return lax.dot_general(a, b, _NT_DIMS, preferred_element_type=F32)


def _layer_norm(x, g, b):
    mu = jnp.mean(x, axis=-1, keepdims=True)
    xc = x - mu
    var = jnp.mean(xc * xc, axis=-1, keepdims=True)
    return xc * lax.rsqrt(var + LN_EPS) * g + b


def _rms_norm(x, g):
    return x * lax.rsqrt(jnp.mean(x * x, axis=-1, keepdims=True) + RMS_EPS) * g


def _proj_kernel(x_ref, lng_ref, lnb_ref, wnn_ref, wnt_ref, wki_ref, wsel_ref, wuq_ref, wk_ref,
                 wv_ref, qg_ref, kvg_ref, cos_ref, sin_ref, ct_ref, st_ref,
                 ka_ref, ki_ref, g_ref, kb_ref, qat_ref, qit_ref, wit_ref, vat_ref, qbt_ref,
                 vbt_ref, *, n_valid):
    rows = x_ref.shape[1]
    y = _layer_norm(x_ref[0], lng_ref[...], lnb_ref[...])
    if n_valid < rows:
        y = jnp.where(lax.broadcasted_iota(I32, (rows, 1), 0) < n_valid, y, 0.0)
    hb = y.astype(BF16)

    def nn(lo, hi):
        return jnp.dot(hb, wnn_ref[:, lo:hi], preferred_element_type=F32)

    def nt(lo, hi):
        return _dot_nt(wnt_ref[lo:hi, :], hb)

    ka_ref[0] = nn(NN_KA, NN_G).astype(BF16)
    g_ref[0] = nn(NN_G, NN_CQ)

    qat_ref[0, 0] = (nt(NT_QA, NT_VA) * (A_HEAD_DIM ** -0.5 * LOG2_E)).astype(BF16)
    vat_ref[0, 0] = nt(NT_VA, NT_END).astype(BF16)

    ki_ref[0] = jnp.dot(y, wki_ref[...], preferred_element_type=F32)
    qit = _dot_nt(wsel_ref[0:SEL_WI, :], y)
    for h in range(HEADS):
        qit_ref[0, 0, :, h * rows:(h + 1) * rows] = qit[h * IDX_DIM:(h + 1) * IDX_DIM, :]
    wit = _dot_nt(wsel_ref[SEL_WI:SEL_END, :], y) * (HEADS ** -0.5)
    wit = wit * (IDX_DIM ** -0.5)
    wit_ref[0, 0] = wit[0:HEADS, :]

    cqn = _rms_norm(nn(NN_CQ, NN_CKV), qg_ref[...]).astype(BF16)
    ckvn = _rms_norm(nn(NN_CKV, NN_PE), kvg_ref[...]).astype(BF16)
    qbt = _dot_nt(wuq_ref[...], cqn)
    ct = ct_ref[...]
    st = st_ref[...]
    half = B_ROPE // 2
    for h in range(HEADS):
        q = qbt[h * LANES:(h + 1) * LANES]
        x1, x2 = q[B_NOPE:B_NOPE + half], q[B_NOPE + half:B_NOPE + B_ROPE]
        rot = jnp.concatenate([jnp.zeros((B_NOPE, rows), F32), -x2, x1,
                               jnp.zeros((LANES - B_NOPE - B_ROPE, rows), F32)], axis=0)
        qbt_ref[0, 0, h * LANES:(h + 1) * LANES, :] = (q * ct + rot * st).astype(BF16)
    kn = jnp.dot(ckvn, wk_ref[...], preferred_element_type=F32)
    kpe = nn(NN_PE, NN_PER) * cos_ref[...] + nn(NN_PER, NN_END) * sin_ref[...]
    for h in range(HEADS):
        cols = slice(h * LANES, (h + 1) * LANES)
        kb_ref[0, :, cols] = (kn[:, cols] + kpe).astype(BF16)
    vbt_ref[0, 0] = _dot_nt(wv_ref[...], ckvn).astype(BF16)


_MAX_EXACT = REL_BUCKETS // 2
_BUCKET_STARTS = tuple(math.ceil(_MAX_EXACT * (REL_MAX_DIST / _MAX_EXACT) ** (k / (REL_BUCKETS - _MAX_EXACT)))
                       for k in range(1, REL_BUCKETS - _MAX_EXACT))


def _t5_bucket(d):
    d = jnp.maximum(d, 0)
    large = jnp.full(d.shape, _MAX_EXACT, I32)
    for start in _BUCKET_STARTS:
        large = large + jnp.where(d >= start, 1, 0)
    return jnp.where(d < _MAX_EXACT, d, large)


def _dsa_kernel(relb_ref, ka_ref, ki_ref, vat_ref, qat_ref, qit_ref, wit_ref, kam_ref, vatm_ref,
                oa_ref, keys_s, bias_s, metab_s, acc_s, qpad_s, s_scr, madd_s, hi_s, lo_s, lo2_s,
                *, topk, seq):
    b = pl.program_id(0)
    i = pl.program_id(1)
    nch = i + 1
    row = lax.broadcasted_iota(I32, (TQ, TQ), 0)
    pos_q = i * TQ + lax.broadcasted_iota(I32, (TQ, TQ), 1)

    @pl.when((b == 0) & (i == 0))
    def _():
        sub = lax.broadcasted_iota(I32, (8, TQ), 0)
        lane = lax.broadcasted_iota(I32, (8, TQ), 1)

        def fill(dst, n_rows, offset):
            def step(r8, carry):
                r0 = pl.multiple_of(r8 * 8, 8)
                bucket = _t5_bucket(offset + lane - (r0 + sub))
                for h in range(HEADS):
                    t = jnp.zeros((8, TQ), F32)
                    for bk in range(REL_BUCKETS):
                        t = jnp.where(bucket == bk, relb_ref[bk, h], t)
                    dst[h, pl.ds(r0, 8), :] = (t - relb_ref[REL_BUCKETS - 1, h]) * LOG2_E
                return carry
            lax.fori_loop(0, n_rows // 8, step, 0)

        for dl in range(2):
            fill(bias_s.at[:, dl], TQ, dl * TQ)
        fill(metab_s, META_ROWS, N_META)

    def idx_body(c, carry, diagonal):
        k0 = pl.multiple_of(c * TQ, TQ)
        rel = jnp.dot(ki_ref[0, pl.ds(k0, TQ), :], qit_ref[0, 0],
                      preferred_element_type=F32)
        s = jnp.zeros((TQ, TQ), F32)
        for h in range(HEADS):
            s = s + jnp.maximum(rel[:, h * TQ:(h + 1) * TQ], 0.0) * wit_ref[0, 0, h:h + 1, :]
        if diagonal:
            s = jnp.where(k0 + row <= pos_q, s, -jnp.inf)
        bits = lax.bitcast_convert_type(s, I32)
        key = jnp.where(bits < 0, INT_MIN - bits, bits)
        keys_s[pl.ds(k0, TQ), :] = key
        h0 = pl.multiple_of(c * HALF, HALF)
        ka_, kb_ = key[:HALF], key[HALF:]
        hi_s[pl.ds(h0, HALF), :] = ((ka_ >> 16) & 0xFFFF) | (kb_ & -0x10000)
        lo_s[pl.ds(h0, HALF), :] = ((ka_ & 0xFFFF) ^ 0x8000) | ((kb_ << 16) ^ INT_MIN)
        return carry

    def idx_four(j, carry):
        for u in range(4):
            idx_body(4 * j + u, carry, diagonal=False)
        return carry

    lax.fori_loop(0, (nch - 1) // 4, idx_four, 0)
    lax.fori_loop((nch - 1) // 4 * 4, nch - 1, functools.partial(idx_body, diagonal=False), 0)
    idx_body(nch - 1, 0, diagonal=True)

    nscan = (nch + SCAN - 1) // SCAN

    def fill_lowest(c, carry):
        h0 = pl.multiple_of(c * HALF, HALF)
        lowest = jnp.full((HALF, TQ), LOWEST_I16_PAIR, I32)
        hi_s[pl.ds(h0, HALF), :] = lowest
        lo_s[pl.ds(h0, HALF), :] = lowest
        return carry

    lax.fori_loop(nch, nscan * SCAN, fill_lowest, 0)

    def count(pred):
        def body(c, acc):
            k0 = pl.multiple_of(c * TQ, TQ)
            hit = pred(keys_s[pl.ds(k0, TQ), :], k0 + row)
            return acc + jnp.sum(jnp.where(hit, 1, 0).reshape(TQ // 8, 8, TQ), axis=0)
        acc = lax.fori_loop(0, nch, body, jnp.zeros((8, TQ), I32))
        return jnp.sum(acc, axis=0, keepdims=True)

    def splat16(pat):
        word = pat | (pat << 16)
        return pltpu.bitcast(jnp.broadcast_to(word, (8, TQ)), I16)

    def count16(scr, pat, strict=False):
        thr_p = splat16(pat)
        one, zero = jnp.int16(1), jnp.int16(0)

        def body(cs, acc):
            h0 = pl.multiple_of(cs * (SCAN * HALF), SCAN * HALF)
            k = pltpu.bitcast(scr[pl.ds(h0, SCAN * HALF), :], I16)
            parts = []
            for j in range(SCAN * TQ // 16):
                kj = k[16 * j:16 * (j + 1)]
                parts.append(jnp.where((kj > thr_p) if strict else (kj >= thr_p), one, zero))
            while len(parts) > 1:
                parts = [x + y for x, y in zip(parts[::2], parts[1::2])] + parts[len(parts) & ~1:]
            return acc + parts[0].astype(I32)
        acc = lax.fori_loop(0, nscan, body, jnp.zeros((16, TQ), I32))
        return jnp.sum(acc, axis=0, keepdims=True)

    def hi_bit(it, carry):
        t_hi, cnt_t = carry
        cand = t_hi | jnp.left_shift(jnp.int32(1), 15 - it)
        cnt = count16(hi_s, cand ^ 0x8000)
        ok = cnt >= topk
        return jnp.where(ok, cand, t_hi), jnp.where(ok, cnt, cnt_t)

    zero_q = jnp.zeros((1, TQ), I32)
    t_hi, cnt_t = lax.fori_loop(0, 16, hi_bit, (zero_q, jnp.full((1, TQ), seq, I32)))
    thr_hi_p = splat16(t_hi ^ 0x8000)
    cnt_gt = count16(hi_s, t_hi ^ 0x8000, strict=True)

    def lo_prep(c2, carry):
        h0 = pl.multiple_of(c2 * TQ, TQ)
        hi = pltpu.bitcast(hi_s[pl.ds(h0, TQ), :], I16)
        lo = pltpu.bitcast(lo_s[pl.ds(h0, TQ), :], I16)
        parts = [jnp.where(hi[16 * j:16 * (j + 1)] == thr_hi_p, lo[16 * j:16 * (j + 1)],
                           jnp.int16(-0x8000)) for j in range(2 * TQ // 16)]
        lo2_s[pl.ds(h0, TQ), :] = pltpu.bitcast(jnp.concatenate(parts, axis=0), I32)
        return carry

    lax.fori_loop(0, nscan * (SCAN // 2), lo_prep, 0)

    def lo_bit(it, carry):
        t_lo, cnt_t = carry
        cand = t_lo | jnp.left_shift(jnp.int32(1), 15 - it)
        cnt = cnt_gt + count16(lo2_s, cand ^ 0x8000)
        ok = cnt >= topk
        return jnp.where(ok, cand, t_lo), jnp.where(ok, cnt, cnt_t)

    t_lo, cnt_t = lax.fori_loop(0, 16, lo_bit, (zero_q, cnt_t))
    thr = ((t_hi << 16) | t_lo) ^ INT_MIN

    tie_cut = (cnt_t > topk) & (thr > KEY_NEG_INF)
    any_cut = jnp.max(jnp.where(tie_cut, 1, 0))

    @pl.when(any_cut > 0)
    def _():
        need = topk - count(lambda k, p: k > thr)

        def jbit(it, j):
            cand = j | jnp.left_shift(jnp.int32(1), 12 - it)
            cnt = count(lambda k, p: (k == thr) & (p < cand))
            return jnp.where(cnt < need, cand, j)

        j = lax.fori_loop(0, 13, jbit, jnp.zeros((1, TQ), I32))
        j_last = jnp.where(tie_cut, j, seq)

        def demote(c, carry):
            k0 = pl.multiple_of(c * TQ, TQ)
            key = keys_s[pl.ds(k0, TQ), :]
            keys_s[pl.ds(k0, TQ), :] = jnp.where((key == thr) & (k0 + row > j_last), key - 1, key)
            return carry

        lax.fori_loop(0, nch, demote, 0)

    thr_sel = jnp.maximum(thr, KEY_NEG_INF + 1)

    for h in range(HEADS):
        qh = qat_ref[0, 0, h * A_HEAD_DIM:(h + 1) * A_HEAD_DIM, :]
        zero_half = jnp.zeros_like(qh)
        qpad_s[h] = jnp.concatenate([zero_half, qh] if h % 2 else [qh, zero_half], axis=0)

    def set_mask(c):
        k0 = pl.multiple_of(c * TQ, TQ)
        madd_s[...] = jnp.where(keys_s[pl.ds(k0, TQ), :] >= thr_sel, 0.0, MASKED)

    def scores(c, h, near):
        k0 = pl.multiple_of(c * TQ, TQ)
        pair = slice((h // 2) * LANES, (h // 2 + 1) * LANES)
        s = jnp.dot(ka_ref[0, pl.ds(k0, TQ), pair], qpad_s[h],
                    preferred_element_type=F32) + madd_s[...]
        if near:
            dl = i - c
            s = s + jnp.where(dl == 0, bias_s[h, 0], jnp.where(dl == 1, bias_s[h, 1], 0.0))
        s_scr[h] = s
        return jnp.max(s, axis=0, keepdims=True)

    def att_body(c, carry, lookahead):
        ms, ls, cms = carry
        if lookahead is not None:
            set_mask(c + 1)
        new_ms, new_ls, new_cms = [], [], []
        for h in range(HEADS):
            m_new = jnp.maximum(ms[h], cms[h])
            alpha = jnp.exp2(ms[h] - m_new)
            p = jnp.exp2(s_scr[h] - m_new)
            new_ls.append(alpha * ls[h] + jnp.sum(p, axis=0, keepdims=True))
            new_ms.append(m_new)
            rows = slice(h * A_HEAD_DIM, (h + 1) * A_HEAD_DIM)
            pv = jnp.dot(vat_ref[0, c, rows, :], p.astype(BF16), preferred_element_type=F32)
            acc_s[rows, :] = alpha * acc_s[rows, :] + pv
            if lookahead is not None:
                new_cms.append(scores(c + 1, h, near=lookahead))
        return tuple(new_ms), tuple(new_ls), tuple(new_cms)

    meta_ok = lax.broadcasted_iota(I32, (META_ROWS, TQ), 0) < N_META
    ms0, ls0 = [], []
    for h in range(HEADS):
        pair = slice((h // 2) * LANES, (h // 2 + 1) * LANES)
        s = jnp.dot(kam_ref[:, pair], qpad_s[h], preferred_element_type=F32)
        s = jnp.where(meta_ok, s + jnp.where(i == 0, metab_s[h], 0.0), MASKED)
        m0 = jnp.max(s, axis=0, keepdims=True)
        p = jnp.exp2(s - m0)
        ms0.append(m0)
        ls0.append(jnp.sum(p, axis=0, keepdims=True))
        rows = slice(h * A_HEAD_DIM, (h + 1) * A_HEAD_DIM)
        acc_s[rows, :] = jnp.dot(vatm_ref[rows, :], p.astype(BF16), preferred_element_type=F32)

    set_mask(0)
    init = (tuple(ms0), tuple(ls0), tuple(scores(0, h, near=True) for h in range(HEADS)))
    n_far = jnp.maximum(nch - 3, 0)

    def four_steps(j, carry):
        for u in range(4):
            carry = att_body(4 * j + u, carry, lookahead=False)
        return carry

    carry = lax.fori_loop(0, n_far // 4, four_steps, init)
    carry = lax.fori_loop(n_far // 4 * 4, n_far, functools.partial(att_body, lookahead=False), carry)
    carry = lax.fori_loop(n_far, nch - 1, functools.partial(att_body, lookahead=True), carry)
    ms, ls, _ = att_body(nch - 1, carry, lookahead=None)

    outs = []
    for h in range(HEADS):
        rows = slice(h * A_HEAD_DIM, (h + 1) * A_HEAD_DIM)
        outs.append(acc_s[rows, :] * (1.0 / ls[h]))
    oa_ref[0] = jnp.concatenate(outs, axis=0).T


def _silu(x):
    return x * (1.0 / (1.0 + jnp.exp(-x)))


def _mla_out_kernel(kb_ref, vbt_ref, qbt_ref, kbm_ref, vbtm_ref, x_ref, lng_ref, lnb_ref, oa_ref,
                    g_ref, wout_ref, pg_ref, pb_ref, out_ref, s_scr, acc_s):
    i = pl.program_id(1)
    nch = i + 1
    row = lax.broadcasted_iota(I32, (TQ, TQ), 0)
    pos_q = i * TQ + lax.broadcasted_iota(I32, (TQ, TQ), 1)

    def scores(c, h, diagonal):
        k0 = pl.multiple_of(c * TQ, TQ)
        grp = slice(h * LANES, (h + 1) * LANES)
        s = jnp.dot(kb_ref[0, pl.ds(k0, TQ), grp], qbt_ref[0, 0, grp, :],
                    preferred_element_type=F32)
        if diagonal:
            s = jnp.where(k0 + row <= pos_q, s, MASKED)
        s_scr[h] = s
        return jnp.max(s, axis=0, keepdims=True)

    def body(c, carry, lookahead):
        ms, ls, cms = carry
        new_ms, new_ls, new_cms = [], [], []
        for h in range(HEADS):
            m_new = jnp.maximum(ms[h], cms[h])
            alpha = jnp.exp2(ms[h] - m_new)
            p = jnp.exp2(s_scr[h] - m_new)
            new_ls.append(alpha * ls[h] + jnp.sum(p, axis=0, keepdims=True))
            new_ms.append(m_new)
            rows = slice(h * B_V, (h + 1) * B_V)
            pv = jnp.dot(vbt_ref[0, c, rows, :], p.astype(BF16), preferred_element_type=F32)
            acc_s[rows, :] = alpha * acc_s[rows, :] + pv
            if lookahead is not None:
                new_cms.append(scores(c + 1, h, diagonal=lookahead))
        return tuple(new_ms), tuple(new_ls), tuple(new_cms)

    meta_ok = lax.broadcasted_iota(I32, (META_ROWS, TQ), 0) < N_META
    ms0, ls0 = [], []
    for h in range(HEADS):
        grp = slice(h * LANES, (h + 1) * LANES)
        s = jnp.dot(kbm_ref[:, grp], qbt_ref[0, 0, grp, :], preferred_element_type=F32)
        s = jnp.where(meta_ok, s, MASKED)
        m0 = jnp.max(s, axis=0, keepdims=True)
        p = jnp.exp2(s - m0)
        ms0.append(m0)
        ls0.append(jnp.sum(p, axis=0, keepdims=True))
        rows = slice(h * B_V, (h + 1) * B_V)
        acc_s[rows, :] = jnp.dot(vbtm_ref[rows, :], p.astype(BF16), preferred_element_type=F32)

    cms0 = tuple(scores(0, h, diagonal=True) for h in range(HEADS))
    init = (tuple(ms0), tuple(ls0), cms0)
    n_full = jnp.maximum(nch - 2, 0)

    def four_steps(j, carry):
        for u in range(4):
            carry = body(4 * j + u, carry, lookahead=False)
        return carry

    carry = lax.fori_loop(0, n_full // 4, four_steps, init)
    carry = lax.fori_loop(n_full // 4 * 4, n_full, functools.partial(body, lookahead=False), carry)
    carry = lax.fori_loop(n_full, nch - 1, functools.partial(body, lookahead=True), carry)
    ms, ls, _ = body(nch - 1, carry, lookahead=None)

    outs = []
    for h in range(HEADS):
        rows = slice(h * B_V, (h + 1) * B_V)
        outs.append(acc_s[rows, :] * (1.0 / ls[h]))
    ob = jnp.concatenate(outs, axis=0).T

    h = _layer_norm(x_ref[0], lng_ref[...], lnb_ref[...])
    g = g_ref[0]
    mixed = jnp.concatenate([oa_ref[0] * _silu(g[:, :A_WIDTH]), ob * _silu(g[:, A_WIDTH:])], axis=-1)
    out = jnp.dot(mixed.astype(BF16), wout_ref[...], preferred_element_type=F32)
    out_ref[0] = _layer_norm(ALPHA * h + out, pg_ref[...], pb_ref[...])


def _rot_cols(w):
    half = w.shape[-1] // 2
    return jnp.concatenate([-w[:, half:], w[:, :half]], axis=-1)


def _full(shape):
    return pl.BlockSpec(shape, lambda *_: (0,) * len(shape))


def _rope_tables(pos, scale):
    d_q = B_NOPE + B_ROPE
    n = pos.shape[0]
    inv_freq = ROPE_THETA ** (-jnp.arange(0, B_ROPE, 2, dtype=F32) / B_ROPE)
    ang = pos[:, None] * inv_freq[None, :]
    cos2 = jnp.concatenate([jnp.cos(ang)] * 2, axis=1)
    sin2 = jnp.concatenate([jnp.sin(ang)] * 2, axis=1)
    zl = lambda k: jnp.zeros((n, k), F32)
    cos128 = jnp.concatenate([zl(B_NOPE), cos2, zl(LANES - d_q)], axis=1)
    sin128 = jnp.concatenate([zl(B_NOPE), sin2, zl(LANES - d_q)], axis=1)
    ct = (jnp.concatenate([jnp.ones((n, B_NOPE), F32), cos2, zl(LANES - d_q)], axis=1) * scale).T
    st = (sin128 * scale).T
    return cos128, sin128, ct, st


def kernel(x, meta_tokens, ln_emb_g, ln_emb_b, w_in, w_uq, q_norm_g, w_ukv, kv_norm_g, rel_bias,
           w_out, ln_post_g, ln_post_b):
    B, S, D = x.shape
    assert w_in.shape[0] == DEPTH == 1
    assert S % TQ == 0 and S < (1 << 13) and meta_tokens.shape[0] == N_META <= META_ROWS
    nq = S // TQ
    assert nq % SCAN == 0 and SCAN % 2 == 0
    topk = min(TOPK_MAX, S // 4) - N_META
    assert 0 < topk <= TQ

    w = w_in[0]
    zeros = lambda n: jnp.zeros((D, n), F32)
    w_kpe = w[:, O_KPE:O_GB]
    pad_pe = lambda m: jnp.concatenate([zeros(B_NOPE), m, zeros(LANES - B_NOPE - B_ROPE)], axis=1)
    wnn = jnp.concatenate([w[:, O_KA:O_VA], w[:, O_GA:O_QI], w[:, O_GB:O_END], w[:, O_CQ:O_CKV],
                           w[:, O_CKV:O_KPE], pad_pe(w_kpe), pad_pe(_rot_cols(w_kpe))],
                          axis=1).astype(BF16)
    assert wnn.shape[1] == NN_END
    wnt = jnp.concatenate([w[:, O_QA:O_KA], w[:, O_VA:O_GA]], axis=1).T.astype(BF16)
    assert wnt.shape[0] == NT_END
    wki = w[:, O_KI:O_WI].astype(F32)
    wsel = jnp.concatenate([w[:, O_QI:O_KI], w[:, O_WI:O_CQ], zeros(8)], axis=1).T.astype(F32)
    assert wsel.shape[0] == SEL_END

    wq = w_uq[0]
    d_q = B_NOPE + B_ROPE
    uq = []
    for h in range(HEADS):
        uq += [wq[:, h * d_q:(h + 1) * d_q], jnp.zeros((Q_LORA, LANES - d_q), F32)]
    wuq_t = jnp.concatenate(uq, axis=1).T.astype(BF16)
    wkv = w_ukv[0]
    d_kv = B_NOPE + B_V
    wk_groups, wv_groups = [], []
    for h in range(HEADS):
        wk_groups += [wkv[:, h * d_kv:h * d_kv + B_NOPE], jnp.zeros((KV_LORA, LANES - B_NOPE), F32)]
        wv_groups.append(wkv[:, h * d_kv + B_NOPE:(h + 1) * d_kv])
    wk = jnp.concatenate(wk_groups, axis=1).astype(BF16)
    wv_t = jnp.concatenate(wv_groups, axis=1).T.astype(BF16)

    scale_b = d_q ** -0.5 * LOG2_E
    tabs_x = _rope_tables(N_META + jnp.arange(S, dtype=F32), scale_b)
    tabs_m = _rope_tables(jnp.arange(META_PROJ_ROWS, dtype=F32), scale_b)

    row2 = lambda v: v.reshape(1, -1).astype(F32)
    meta_pad = jnp.concatenate([meta_tokens.astype(x.dtype),
                                jnp.zeros((META_PROJ_ROWS - N_META, D), x.dtype)], axis=0)[None]

    params = pltpu.CompilerParams(dimension_semantics=("arbitrary", "arbitrary"),
                                  vmem_limit_bytes=VMEM_LIMIT)
    sds = jax.ShapeDtypeStruct

    def project(rows_in, tabs, nb, nblk, rows, n_valid):
        row_blk = lambda n: pl.BlockSpec((1, rows, n), lambda b, i: (b, i, 0))
        t_blk = lambda r, n: pl.BlockSpec((1, 1, r, n), lambda b, i: (b, i, 0, 0))
        return pl.pallas_call(
            functools.partial(_proj_kernel, n_valid=n_valid),
            grid=(nb, nblk),
            in_specs=[row_blk(D), _full((1, D)), _full((1, D)), _full((D, NN_END)),
                      _full((NT_END, D)), _full((D, IDX_DIM)), _full((SEL_END, D)),
                      _full((HEADS * LANES, Q_LORA)), _full((KV_LORA, HEADS * LANES)),
                      _full((B_WIDTH, KV_LORA)), _full((1, Q_LORA)), _full((1, KV_LORA)),
                      pl.BlockSpec((rows, LANES), lambda b, i: (i, 0)),
                      pl.BlockSpec((rows, LANES), lambda b, i: (i, 0)),
                      pl.BlockSpec((LANES, rows), lambda b, i: (0, i)),
                      pl.BlockSpec((LANES, rows), lambda b, i: (0, i))],
            out_specs=[row_blk(A_WIDTH), row_blk(IDX_DIM), row_blk(A_WIDTH + B_WIDTH),
                       row_blk(HEADS * LANES), t_blk(A_WIDTH, rows),
                       t_blk(IDX_DIM, HEADS * rows), t_blk(HEADS, rows), t_blk(A_WIDTH, rows),
                       t_blk(HEADS * LANES, rows), t_blk(B_WIDTH, rows)],
            out_shape=[sds((nb, nblk * rows, A_WIDTH), BF16), sds((nb, nblk * rows, IDX_DIM), F32),
                       sds((nb, nblk * rows, A_WIDTH + B_WIDTH), F32),
                       sds((nb, nblk * rows, HEADS * LANES), BF16),
                       sds((nb, nblk, A_WIDTH, rows), BF16),
                       sds((nb, nblk, IDX_DIM, HEADS * rows), F32),
                       sds((nb, nblk, HEADS, rows), F32), sds((nb, nblk, A_WIDTH, rows), BF16),
                       sds((nb, nblk, HEADS * LANES, rows), BF16),
                       sds((nb, nblk, B_WIDTH, rows), BF16)],
            compiler_params=params, name="proj",
        )(rows_in, row2(ln_emb_g), row2(ln_emb_b), wnn, wnt, wki, wsel, wuq_t, wk, wv_t,
          row2(q_norm_g[0]), row2(kv_norm_g[0]), *tabs)

    ka, ki, gates, kb, qat, qit, wit, vat, qbt, vbt = project(x, tabs_x, B, nq, TQ, TQ)
    ka_m, _, _, kb_m, _, _, _, vat_m, _, vbt_m = project(meta_pad, tabs_m, 1, 1, META_PROJ_ROWS,
                                                         N_META)
    ka_m, kb_m = ka_m[0, :META_ROWS], kb_m[0, :META_ROWS]
    vat_m, vbt_m = vat_m[0, 0, :, :META_ROWS], vbt_m[0, 0, :, :META_ROWS]

    batch_rows = lambda n: pl.BlockSpec((1, S, n), lambda b, i: (b, 0, 0))
    batch_t = pl.BlockSpec((1, nq, A_WIDTH, TQ), lambda b, i: (b, 0, 0, 0))
    blk = lambda r, n: pl.BlockSpec((1, 1, r, n), lambda b, i: (b, i, 0, 0))
    out_blk = pl.BlockSpec((1, TQ, A_WIDTH), lambda b, i: (b, i, 0))
    oa = pl.pallas_call(
        functools.partial(_dsa_kernel, topk=topk, seq=S),
        grid=(B, nq),
        in_specs=[pl.BlockSpec(memory_space=pltpu.SMEM), batch_rows(A_WIDTH), batch_rows(IDX_DIM),
                  batch_t, blk(A_WIDTH, TQ), blk(IDX_DIM, HEADS * TQ), blk(HEADS, TQ),
                  _full((META_ROWS, A_WIDTH)), _full((A_WIDTH, META_ROWS))],
        out_specs=out_blk,
        out_shape=sds((B, S, A_WIDTH), F32),
        scratch_shapes=[pltpu.VMEM((S, TQ), I32),
                        pltpu.VMEM((HEADS, 2, TQ, TQ), F32),
                        pltpu.VMEM((HEADS, META_ROWS, TQ), F32),
                        pltpu.VMEM((A_WIDTH, TQ), F32),
                        pltpu.VMEM((HEADS, LANES, TQ), BF16),
                        pltpu.VMEM((HEADS, TQ, TQ), F32),
                        pltpu.VMEM((TQ, TQ), F32),
                        pltpu.VMEM((S // 2, TQ), I32),
                        pltpu.VMEM((S // 2, TQ), I32),
                        pltpu.VMEM((S // 2, TQ), I32)],
        compiler_params=params, name="dsa",
    )(rel_bias.astype(F32), ka, ki, vat, qat, qit, wit, ka_m, vat_m)

    row_blk = lambda n: pl.BlockSpec((1, TQ, n), lambda b, i: (b, i, 0))
    return pl.pallas_call(
        _mla_out_kernel,
        grid=(B, nq),
        in_specs=[batch_rows(HEADS * LANES), batch_t, blk(HEADS * LANES, TQ),
                  _full((META_ROWS, HEADS * LANES)), _full((B_WIDTH, META_ROWS)),
                  row_blk(D), _full((1, D)), _full((1, D)), row_blk(A_WIDTH),
                  row_blk(A_WIDTH + B_WIDTH), _full((A_WIDTH + B_WIDTH, D)), _full((1, D)),
                  _full((1, D))],
        out_specs=row_blk(D),
        out_shape=sds((B, S, D), x.dtype),
        scratch_shapes=[pltpu.VMEM((HEADS, TQ, TQ), F32),
                        pltpu.VMEM((B_WIDTH, TQ), F32)],
        compiler_params=params, name="mla_out",
    )(kb, vbt, qbt, kb_m, vbt_m, x, row2(ln_emb_g), row2(ln_emb_b), oa, gates,
      w_out[0].astype(BF16), row2(ln_post_g[0]), row2(ln_post_b[0]))
```

```python
import functools
import math

import jax
import jax.numpy as jnp
from jax import lax
from jax.experimental import pallas as pl
from jax.experimental.pallas import tpu as pltpu

F32 = jnp.float32
BF16 = jnp.bfloat16
I32 = jnp.int32
I16 = jnp.int16

N_META = 16
HEADS = 8
A_HEAD_DIM = 64
A_WIDTH = HEADS * A_HEAD_DIM
IDX_DIM = 64
TOPK_MAX = 256
B_NOPE = 64
B_ROPE = 32
B_V = 64
B_WIDTH = HEADS * B_V
Q_LORA = 256
KV_LORA = 128
ROPE_THETA = 10000.0
REL_BUCKETS = 32
REL_MAX_DIST = 128
LN_EPS = 1e-5
RMS_EPS = 1e-6
DEPTH = 1
ALPHA = (2.0 * DEPTH) ** 0.25

LANES = 128
TQ = 256
V_EXT = 80
L_ROW = 64
VT_WIDTH = HEADS * V_EXT
HALF = TQ // 2
SCAN = 2
PROJ_BLOCKS = 2
META_ROWS = 16
META_PROJ_ROWS = LANES
MASKED = -1e30
LOG2_E = math.log2(math.e)
INT_MIN = -2147483648
KEY_NEG_INF = -0x7F800000
LOWEST_I16_PAIR = -0x7FFF8000
VMEM_LIMIT = 56 * 1024 * 1024

_SPLITS = (A_WIDTH, A_WIDTH, A_WIDTH, A_WIDTH, HEADS * IDX_DIM, IDX_DIM, HEADS,
           Q_LORA, KV_LORA, B_ROPE, B_WIDTH)
_OFF = [0]
for _s in _SPLITS:
    _OFF.append(_OFF[-1] + _s)
(O_QA, O_KA, O_VA, O_GA, O_QI, O_KI, O_WI, O_CQ, O_CKV, O_KPE, O_GB, O_END) = _OFF

NN_KA = 0
NN_G = NN_KA + A_WIDTH
NN_CQ = NN_G + A_WIDTH + B_WIDTH
NN_CKV = NN_CQ + Q_LORA
NN_PE = NN_CKV + KV_LORA
NN_PER = NN_PE + LANES
NN_END = NN_PER + LANES
NT_QA = 0
NT_VA = NT_QA + A_WIDTH
NT_END = NT_VA + VT_WIDTH
SEL_WI = HEADS * IDX_DIM
SEL_END = SEL_WI + 16

_NT_DIMS = (((1,), (1,)), ((), ()))


def _dot_nt(a, b):
    return lax.dot_general(a, b, _NT_DIMS, preferred_element_type=F32)


def _layer_norm(x, g, b):
    mu = jnp.mean(x, axis=-1, keepdims=True)
    xc = x - mu
    var = jnp.mean(xc * xc, axis=-1, keepdims=True)
    return xc * lax.rsqrt(var + LN_EPS) * g + b


def _rms_norm(x, g):
    return x * lax.rsqrt(jnp.mean(x * x, axis=-1, keepdims=True) + RMS_EPS) * g


def _proj_kernel(x_ref, lng_ref, lnb_ref, wnn_ref, wnt_ref, wki_ref, wsel_ref, wuq_ref, wk_ref,
                 wv_ref, qg_ref, kvg_ref, cos_ref, sin_ref, ct_ref, st_ref,
                 ka_ref, ki_ref, g_ref, kb_ref, qat_ref, qit_ref, wit_ref, vat_ref, qbt_ref,
                 vbt_ref, *, n_valid):
    rows = x_ref.shape[1]
    nsub = qat_ref.shape[1]
    blk = rows // nsub
    y = _layer_norm(x_ref[0], lng_ref[...], lnb_ref[...])
    if n_valid < rows:
        y = jnp.where(lax.broadcasted_iota(I32, (rows, 1), 0) < n_valid, y, 0.0)
    hb = y.astype(BF16)

    def nn(lo, hi):
        return jnp.dot(hb, wnn_ref[:, lo:hi], preferred_element_type=F32)

    def nt(lo, hi):
        return _dot_nt(wnt_ref[lo:hi, :], hb)

    ka_ref[0] = nn(NN_KA, NN_G).astype(BF16)
    g_ref[0] = nn(NN_G, NN_CQ)

    qat = (nt(NT_QA, NT_VA) * (A_HEAD_DIM ** -0.5 * LOG2_E)).astype(BF16)
    ones_row = jnp.where(lax.broadcasted_iota(I32, (VT_WIDTH, 1), 0) % V_EXT == L_ROW, 1.0, 0.0)
    vat = (nt(NT_VA, NT_END) + ones_row).astype(BF16)

    ki_ref[0] = jnp.dot(y, wki_ref[...], preferred_element_type=F32)
    qit = _dot_nt(wsel_ref[0:SEL_WI, :], y)
    wit = _dot_nt(wsel_ref[SEL_WI:SEL_END, :], y) * (HEADS ** -0.5)
    wit = wit * (IDX_DIM ** -0.5)
    for s in range(nsub):
        cols = slice(s * blk, (s + 1) * blk)
        qat_ref[0, s] = qat[:, cols]
        vat_ref[0, s] = vat[:, cols]
        wit_ref[0, s] = wit[0:HEADS, cols]
        for h in range(HEADS):
            qit_ref[0, s, :, h * blk:(h + 1) * blk] = qit[h * IDX_DIM:(h + 1) * IDX_DIM, cols]

    cqn = _rms_norm(nn(NN_CQ, NN_CKV), qg_ref[...]).astype(BF16)
    ckvn = _rms_norm(nn(NN_CKV, NN_PE), kvg_ref[...]).astype(BF16)
    qbt = _dot_nt(wuq_ref[...], cqn)
    ct = ct_ref[...]
    st = st_ref[...]
    half = B_ROPE // 2
    for h in range(HEADS):
        q = qbt[h * LANES:(h + 1) * LANES]
        x1, x2 = q[B_NOPE:B_NOPE + half], q[B_NOPE + half:B_NOPE + B_ROPE]
        rot = jnp.concatenate([jnp.zeros((B_NOPE, rows), F32), -x2, x1,
                               jnp.zeros((LANES - B_NOPE - B_ROPE, rows), F32)], axis=0)
        qh = (q * ct + rot * st).astype(BF16)
        for s in range(nsub):
            qbt_ref[0, s, h * LANES:(h + 1) * LANES, :] = qh[:, s * blk:(s + 1) * blk]
    kn = jnp.dot(ckvn, wk_ref[...], preferred_element_type=F32)
    kpe = nn(NN_PE, NN_PER) * cos_ref[...] + nn(NN_PER, NN_END) * sin_ref[...]
    for h in range(HEADS):
        cols = slice(h * LANES, (h + 1) * LANES)
        kb_ref[0, :, cols] = (kn[:, cols] + kpe).astype(BF16)
    vbt = _dot_nt(wv_ref[...], ckvn).astype(BF16)
    for s in range(nsub):
        vbt_ref[0, s] = vbt[:, s * blk:(s + 1) * blk]


_MAX_EXACT = REL_BUCKETS // 2
_BUCKET_STARTS = tuple(math.ceil(_MAX_EXACT * (REL_MAX_DIST / _MAX_EXACT) ** (k / (REL_BUCKETS - _MAX_EXACT)))
                       for k in range(1, REL_BUCKETS - _MAX_EXACT))


def _t5_bucket(d):
    d = jnp.maximum(d, 0)
    large = jnp.full(d.shape, _MAX_EXACT, I32)
    for start in _BUCKET_STARTS:
        large = large + jnp.where(d >= start, 1, 0)
    return jnp.where(d < _MAX_EXACT, d, large)


def _dsa_kernel(relb_ref, ka_ref, ki_ref, vat_ref, qat_ref, qit_ref, wit_ref, kam_ref, vatm_ref,
                oa_ref, keys_s, bias_s, metab_s, acc_s, qpad_s, s_scr, madd_s, hi_s, lo_s, lo2_s,
                *, topk, seq):
    b = pl.program_id(0)
    i = pl.program_id(1)
    nch = i + 1
    row = lax.broadcasted_iota(I32, (TQ, TQ), 0)
    pos_q = i * TQ + lax.broadcasted_iota(I32, (TQ, TQ), 1)

    @pl.when((b == 0) & (i == 0))
    def _():
        sub = lax.broadcasted_iota(I32, (8, TQ), 0)
        lane = lax.broadcasted_iota(I32, (8, TQ), 1)

        def fill(dst, n_rows, offset):
            def step(r8, carry):
                r0 = pl.multiple_of(r8 * 8, 8)
                bucket = _t5_bucket(offset + lane - (r0 + sub))
                for h in range(HEADS):
                    t = jnp.zeros((8, TQ), F32)
                    for bk in range(REL_BUCKETS):
                        t = jnp.where(bucket == bk, relb_ref[bk, h], t)
                    dst[h, pl.ds(r0, 8), :] = (t - relb_ref[REL_BUCKETS - 1, h]) * LOG2_E
                return carry
            lax.fori_loop(0, n_rows // 8, step, 0)

        for dl in range(2):
            fill(bias_s.at[:, dl], TQ, dl * TQ)
        fill(metab_s, META_ROWS, N_META)

    def idx_body(c, carry, diagonal):
        k0 = pl.multiple_of(c * TQ, TQ)
        rel = jnp.dot(ki_ref[0, pl.ds(k0, TQ), :], qit_ref[0, 0],
                      preferred_element_type=F32)
        s = jnp.zeros((TQ, TQ), F32)
        for h in range(HEADS):
            s = s + jnp.maximum(rel[:, h * TQ:(h + 1) * TQ], 0.0) * wit_ref[0, 0, h:h + 1, :]
        if diagonal:
            s = jnp.where(k0 + row <= pos_q, s, -jnp.inf)
        bits = lax.bitcast_convert_type(s, I32)
        key = jnp.where(bits < 0, INT_MIN - bits, bits)
        keys_s[pl.ds(k0, TQ), :] = key
        h0 = pl.multiple_of(c * HALF, HALF)
        ka_, kb_ = key[:HALF], key[HALF:]
        hi_s[pl.ds(h0, HALF), :] = ((ka_ >> 16) & 0xFFFF) | (kb_ & -0x10000)
        lo_s[pl.ds(h0, HALF), :] = ((ka_ & 0xFFFF) ^ 0x8000) | ((kb_ << 16) ^ INT_MIN)
        return carry

    def idx_four(j, carry):
        for u in range(4):
            idx_body(4 * j + u, carry, diagonal=False)
        return carry

    lax.fori_loop(0, (nch - 1) // 4, idx_four, 0)
    lax.fori_loop((nch - 1) // 4 * 4, nch - 1, functools.partial(idx_body, diagonal=False), 0)
    idx_body(nch - 1, 0, diagonal=True)

    nscan = (nch + SCAN - 1) // SCAN

    def fill_lowest(c, carry):
        h0 = pl.multiple_of(c * HALF, HALF)
        lowest = jnp.full((HALF, TQ), LOWEST_I16_PAIR, I32)
        hi_s[pl.ds(h0, HALF), :] = lowest
        lo_s[pl.ds(h0, HALF), :] = lowest
        return carry

    lax.fori_loop(nch, nscan * SCAN, fill_lowest, 0)

    def count(pred):
        def body(c, acc):
            k0 = pl.multiple_of(c * TQ, TQ)
            hit = pred(keys_s[pl.ds(k0, TQ), :], k0 + row)
            return acc + jnp.sum(jnp.where(hit, 1, 0).reshape(TQ // 8, 8, TQ), axis=0)
        acc = lax.fori_loop(0, nch, body, jnp.zeros((8, TQ), I32))
        return jnp.sum(acc, axis=0, keepdims=True)

    def splat16(pat):
        word = pat | (pat << 16)
        return pltpu.bitcast(jnp.broadcast_to(word, (8, TQ)), I16)

    def count16(scr, pat, strict=False):
        thr_p = splat16(pat)
        one, zero = jnp.int16(1), jnp.int16(0)

        def body(cs, acc):
            h0 = pl.multiple_of(cs * (SCAN * HALF), SCAN * HALF)
            k = pltpu.bitcast(scr[pl.ds(h0, SCAN * HALF), :], I16)
            parts = []
            for j in range(SCAN * TQ // 16):
                kj = k[16 * j:16 * (j + 1)]
                parts.append(jnp.where((kj > thr_p) if strict else (kj >= thr_p), one, zero))
            while len(parts) > 1:
                parts = [x + y for x, y in zip(parts[::2], parts[1::2])] + parts[len(parts) & ~1:]
            return acc + parts[0].astype(I32)
        acc = lax.fori_loop(0, nscan, body, jnp.zeros((16, TQ), I32))
        return jnp.sum(acc, axis=0, keepdims=True)

    def hi_bit(it, carry):
        t_hi, cnt_t = carry
        cand = t_hi | jnp.left_shift(jnp.int32(1), 15 - it)
        cnt = count16(hi_s, cand ^ 0x8000)
        ok = cnt >= topk
        return jnp.where(ok, cand, t_hi), jnp.where(ok, cnt, cnt_t)

    zero_q = jnp.zeros((1, TQ), I32)
    t_hi, cnt_t = lax.fori_loop(0, 16, hi_bit, (zero_q, jnp.full((1, TQ), seq, I32)))
    thr_hi_p = splat16(t_hi ^ 0x8000)
    cnt_gt = count16(hi_s, t_hi ^ 0x8000, strict=True)

    def lo_prep(c2, carry):
        h0 = pl.multiple_of(c2 * TQ, TQ)
        hi = pltpu.bitcast(hi_s[pl.ds(h0, TQ), :], I16)
        lo = pltpu.bitcast(lo_s[pl.ds(h0, TQ), :], I16)
        parts = [jnp.where(hi[16 * j:16 * (j + 1)] == thr_hi_p, lo[16 * j:16 * (j + 1)],
                           jnp.int16(-0x8000)) for j in range(2 * TQ // 16)]
        lo2_s[pl.ds(h0, TQ), :] = pltpu.bitcast(jnp.concatenate(parts, axis=0), I32)
        return carry

    lax.fori_loop(0, nscan * (SCAN // 2), lo_prep, 0)

    def lo_bit(it, carry):
        t_lo, cnt_t = carry
        cand = t_lo | jnp.left_shift(jnp.int32(1), 15 - it)
        cnt = cnt_gt + count16(lo2_s, cand ^ 0x8000)
        ok = cnt >= topk
        return jnp.where(ok, cand, t_lo), jnp.where(ok, cnt, cnt_t)

    t_lo, cnt_t = lax.fori_loop(0, 16, lo_bit, (zero_q, cnt_t))
    thr = ((t_hi << 16) | t_lo) ^ INT_MIN

    tie_cut = (cnt_t > topk) & (thr > KEY_NEG_INF)
    any_cut = jnp.max(jnp.where(tie_cut, 1, 0))

    @pl.when(any_cut > 0)
    def _():
        need = topk - count(lambda k, p: k > thr)

        def jbit(it, j):
            cand = j | jnp.left_shift(jnp.int32(1), 12 - it)
            cnt = count(lambda k, p: (k == thr) & (p < cand))
            return jnp.where(cnt < need, cand, j)

        j = lax.fori_loop(0, 13, jbit, jnp.zeros((1, TQ), I32))
        j_last = jnp.where(tie_cut, j, seq)

        def demote(c, carry):
            k0 = pl.multiple_of(c * TQ, TQ)
            key = keys_s[pl.ds(k0, TQ), :]
            keys_s[pl.ds(k0, TQ), :] = jnp.where((key == thr) & (k0 + row > j_last), key - 1, key)
            return carry

        lax.fori_loop(0, nch, demote, 0)

    thr_sel = jnp.maximum(thr, KEY_NEG_INF + 1)

    for h in range(HEADS):
        qh = qat_ref[0, 0, h * A_HEAD_DIM:(h + 1) * A_HEAD_DIM, :]
        zero_half = jnp.zeros_like(qh)
        qpad_s[h] = jnp.concatenate([zero_half, qh] if h % 2 else [qh, zero_half], axis=0)

    def set_mask(c):
        k0 = pl.multiple_of(c * TQ, TQ)
        madd_s[...] = jnp.where(keys_s[pl.ds(k0, TQ), :] >= thr_sel, 0.0, MASKED)

    def scores(c, h, near):
        k0 = pl.multiple_of(c * TQ, TQ)
        pair = slice((h // 2) * LANES, (h // 2 + 1) * LANES)
        s = jnp.dot(ka_ref[0, pl.ds(k0, TQ), pair], qpad_s[h],
                    preferred_element_type=F32) + madd_s[...]
        if near:
            dl = i - c
            s = s + jnp.where(dl == 0, bias_s[h, 0], jnp.where(dl == 1, bias_s[h, 1], 0.0))
        s_scr[h] = s
        return jnp.max(s, axis=0, keepdims=True)

    def att_body(c, carry, lookahead):
        ms, cms = carry
        if lookahead is not None:
            set_mask(c + 1)
        new_ms, new_cms = [], []
        for h in range(HEADS):
            m_new = jnp.maximum(ms[h], cms[h])
            alpha = jnp.exp2(ms[h] - m_new)
            p = jnp.exp2((s_scr[h] - m_new).astype(BF16))
            new_ms.append(m_new)
            rows = slice(h * V_EXT, (h + 1) * V_EXT)
            pv = jnp.dot(vat_ref[0, c, rows, :], p, preferred_element_type=F32)
            acc_s[rows, :] = alpha * acc_s[rows, :] + pv
            if lookahead is not None:
                new_cms.append(scores(c + 1, h, near=lookahead))
        return tuple(new_ms), tuple(new_cms)

    meta_ok = lax.broadcasted_iota(I32, (META_ROWS, TQ), 0) < N_META
    ms0 = []
    for h in range(HEADS):
        pair = slice((h // 2) * LANES, (h // 2 + 1) * LANES)
        s = jnp.dot(kam_ref[:, pair], qpad_s[h], preferred_element_type=F32)
        s = jnp.where(meta_ok, s + jnp.where(i == 0, metab_s[h], 0.0), MASKED)
        m0 = jnp.max(s, axis=0, keepdims=True)
        p = jnp.exp2((s - m0).astype(BF16))
        ms0.append(m0)
        rows = slice(h * V_EXT, (h + 1) * V_EXT)
        acc_s[rows, :] = jnp.dot(vatm_ref[rows, :], p, preferred_element_type=F32)

    set_mask(0)
    init = (tuple(ms0), tuple(scores(0, h, near=True) for h in range(HEADS)))
    n_far = jnp.maximum(nch - 3, 0)

    def four_steps(j, carry):
        for u in range(4):
            carry = att_body(4 * j + u, carry, lookahead=False)
        return carry

    carry = lax.fori_loop(0, n_far // 4, four_steps, init)
    carry = lax.fori_loop(n_far // 4 * 4, n_far, functools.partial(att_body, lookahead=False), carry)
    carry = lax.fori_loop(n_far, nch - 1, functools.partial(att_body, lookahead=True), carry)
    att_body(nch - 1, carry, lookahead=None)

    outs = []
    for h in range(HEADS):
        num = acc_s[h * V_EXT:h * V_EXT + A_HEAD_DIM, :]
        den = acc_s[h * V_EXT + L_ROW:h * V_EXT + L_ROW + 1, :]
        outs.append(num * (1.0 / den))
    oa_ref[0] = jnp.concatenate(outs, axis=0).T


def _silu(x):
    return x * (1.0 / (1.0 + jnp.exp(-x)))


def _mla_out_kernel(kb_ref, vbt_ref, qbt_ref, kbm_ref, vbtm_ref, x_ref, lng_ref, lnb_ref, oa_ref,
                    g_ref, wout_ref, pg_ref, pb_ref, out_ref, s_scr, acc_s):
    i = pl.program_id(1)
    nch = i + 1
    row = lax.broadcasted_iota(I32, (TQ, TQ), 0)
    pos_q = i * TQ + lax.broadcasted_iota(I32, (TQ, TQ), 1)

    def scores(c, h, diagonal):
        k0 = pl.multiple_of(c * TQ, TQ)
        grp = slice(h * LANES, (h + 1) * LANES)
        s = jnp.dot(kb_ref[0, pl.ds(k0, TQ), grp], qbt_ref[0, 0, grp, :],
                    preferred_element_type=F32)
        if diagonal:
            s = jnp.where(k0 + row <= pos_q, s, MASKED)
        s_scr[h] = s
        return jnp.max(s, axis=0, keepdims=True)

    def body(c, carry, lookahead):
        ms, ls, cms = carry
        new_ms, new_ls, new_cms = [], [], []
        for h in range(HEADS):
            m_new = jnp.maximum(ms[h], cms[h])
            alpha = jnp.exp2(ms[h] - m_new)
            p = jnp.exp2(s_scr[h] - m_new)
            new_ls.append(alpha * ls[h] + jnp.sum(p, axis=0, keepdims=True))
            new_ms.append(m_new)
            rows = slice(h * B_V, (h + 1) * B_V)
            pv = jnp.dot(vbt_ref[0, c, rows, :], p.astype(BF16), preferred_element_type=F32)
            acc_s[rows, :] = alpha * acc_s[rows, :] + pv
            if lookahead is not None:
                new_cms.append(scores(c + 1, h, diagonal=lookahead))
        return tuple(new_ms), tuple(new_ls), tuple(new_cms)

    meta_ok = lax.broadcasted_iota(I32, (META_ROWS, TQ), 0) < N_META
    ms0, ls0 = [], []
    for h in range(HEADS):
        grp = slice(h * LANES, (h + 1) * LANES)
        s = jnp.dot(kbm_ref[:, grp], qbt_ref[0, 0, grp, :], preferred_element_type=F32)
        s = jnp.where(meta_ok, s, MASKED)
        m0 = jnp.max(s, axis=0, keepdims=True)
        p = jnp.exp2(s - m0)
        ms0.append(m0)
        ls0.append(jnp.sum(p, axis=0, keepdims=True))
        rows = slice(h * B_V, (h + 1) * B_V)
        acc_s[rows, :] = jnp.dot(vbtm_ref[rows, :], p.astype(BF16), preferred_element_type=F32)

    cms0 = tuple(scores(0, h, diagonal=True) for h in range(HEADS))
    init = (tuple(ms0), tuple(ls0), cms0)
    n_full = jnp.maximum(nch - 2, 0)

    def four_steps(j, carry):
        for u in range(4):
            carry = body(4 * j + u, carry, lookahead=False)
        return carry

    carry = lax.fori_loop(0, n_full // 4, four_steps, init)
    carry = lax.fori_loop(n_full // 4 * 4, n_full, functools.partial(body, lookahead=False), carry)
    carry = lax.fori_loop(n_full, nch - 1, functools.partial(body, lookahead=True), carry)
    ms, ls, _ = body(nch - 1, carry, lookahead=None)

    outs = []
    for h in range(HEADS):
        rows = slice(h * B_V, (h + 1) * B_V)
        outs.append(acc_s[rows, :] * (1.0 / ls[h]))
    ob = jnp.concatenate(outs, axis=0).T

    h = _layer_norm(x_ref[0], lng_ref[...], lnb_ref[...])
    g = g_ref[0]
    mixed = jnp.concatenate([oa_ref[0] * _silu(g[:, :A_WIDTH]), ob * _silu(g[:, A_WIDTH:])], axis=-1)
    out = jnp.dot(mixed.astype(BF16), wout_ref[...], preferred_element_type=F32)
    out_ref[0] = _layer_norm(ALPHA * h + out, pg_ref[...], pb_ref[...])


def _rot_cols(w):
    half = w.shape[-1] // 2
    return jnp.concatenate([-w[:, half:], w[:, :half]], axis=-1)


def _full(shape):
    return pl.BlockSpec(shape, lambda *_: (0,) * len(shape))


def _rope_tables(pos, scale):
    d_q = B_NOPE + B_ROPE
    n = pos.shape[0]
    inv_freq = ROPE_THETA ** (-jnp.arange(0, B_ROPE, 2, dtype=F32) / B_ROPE)
    ang = pos[:, None] * inv_freq[None, :]
    cos2 = jnp.concatenate([jnp.cos(ang)] * 2, axis=1)
    sin2 = jnp.concatenate([jnp.sin(ang)] * 2, axis=1)
    zl = lambda k: jnp.zeros((n, k), F32)
    cos128 = jnp.concatenate([zl(B_NOPE), cos2, zl(LANES - d_q)], axis=1)
    sin128 = jnp.concatenate([zl(B_NOPE), sin2, zl(LANES - d_q)], axis=1)
    ct = (jnp.concatenate([jnp.ones((n, B_NOPE), F32), cos2, zl(LANES - d_q)], axis=1) * scale).T
    st = (sin128 * scale).T
    return cos128, sin128, ct, st


def kernel(x, meta_tokens, ln_emb_g, ln_emb_b, w_in, w_uq, q_norm_g, w_ukv, kv_norm_g, rel_bias,
           w_out, ln_post_g, ln_post_b):
    B, S, D = x.shape
    assert w_in.shape[0] == DEPTH == 1
    assert S % TQ == 0 and S < (1 << 13) and meta_tokens.shape[0] == N_META <= META_ROWS
    nq = S // TQ
    assert nq % SCAN == 0 and SCAN % 2 == 0
    topk = min(TOPK_MAX, S // 4) - N_META
    assert 0 < topk <= TQ

    w = w_in[0]
    zeros = lambda n: jnp.zeros((D, n), F32)
    w_kpe = w[:, O_KPE:O_GB]
    pad_pe = lambda m: jnp.concatenate([zeros(B_NOPE), m, zeros(LANES - B_NOPE - B_ROPE)], axis=1)
    wnn = jnp.concatenate([w[:, O_KA:O_VA], w[:, O_GA:O_QI], w[:, O_GB:O_END], w[:, O_CQ:O_CKV],
                           w[:, O_CKV:O_KPE], pad_pe(w_kpe), pad_pe(_rot_cols(w_kpe))],
                          axis=1).astype(BF16)
    assert wnn.shape[1] == NN_END
    def pad_heads(m):
        m = m.reshape(m.shape[0], HEADS, -1)
        return jnp.pad(m, ((0, 0), (0, 0), (0, V_EXT - m.shape[2]))).reshape(m.shape[0], VT_WIDTH)

    wnt = jnp.concatenate([w[:, O_QA:O_KA], pad_heads(w[:, O_VA:O_GA])], axis=1).T.astype(BF16)
    assert wnt.shape[0] == NT_END
    wki = w[:, O_KI:O_WI].astype(F32)
    wsel = jnp.concatenate([w[:, O_QI:O_KI], w[:, O_WI:O_CQ], zeros(8)], axis=1).T.astype(F32)
    assert wsel.shape[0] == SEL_END

    wq = w_uq[0]
    d_q = B_NOPE + B_ROPE
    uq = []
    for h in range(HEADS):
        uq += [wq[:, h * d_q:(h + 1) * d_q], jnp.zeros((Q_LORA, LANES - d_q), F32)]
    wuq_t = jnp.concatenate(uq, axis=1).T.astype(BF16)
    wkv = w_ukv[0]
    d_kv = B_NOPE + B_V
    wk_groups, wv_groups = [], []
    for h in range(HEADS):
        wk_groups += [wkv[:, h * d_kv:h * d_kv + B_NOPE], jnp.zeros((KV_LORA, LANES - B_NOPE), F32)]
        wv_groups.append(wkv[:, h * d_kv + B_NOPE:(h + 1) * d_kv])
    wk = jnp.concatenate(wk_groups, axis=1).astype(BF16)
    wv_t = jnp.concatenate(wv_groups, axis=1).T.astype(BF16)

    scale_b = d_q ** -0.5 * LOG2_E
    tabs_x = _rope_tables(N_META + jnp.arange(S, dtype=F32), scale_b)
    tabs_m = _rope_tables(jnp.arange(META_PROJ_ROWS, dtype=F32), scale_b)

    row2 = lambda v: v.reshape(1, -1).astype(F32)
    meta_pad = jnp.concatenate([meta_tokens.astype(x.dtype),
                                jnp.zeros((META_PROJ_ROWS - N_META, D), x.dtype)], axis=0)[None]

    params = pltpu.CompilerParams(dimension_semantics=("arbitrary", "arbitrary"),
                                  vmem_limit_bytes=VMEM_LIMIT)
    sds = jax.ShapeDtypeStruct

    def project(rows_in, tabs, nb, nblk, blk, nsub, n_valid):
        rows = nsub * blk
        row_blk = lambda n: pl.BlockSpec((1, rows, n), lambda b, i: (b, i, 0))
        t_blk = lambda r, n: pl.BlockSpec((1, nsub, r, n), lambda b, i: (b, i, 0, 0))
        return pl.pallas_call(
            functools.partial(_proj_kernel, n_valid=n_valid),
            grid=(nb, nblk // nsub),
            in_specs=[row_blk(D), _full((1, D)), _full((1, D)), _full((D, NN_END)),
                      _full((NT_END, D)), _full((D, IDX_DIM)), _full((SEL_END, D)),
                      _full((HEADS * LANES, Q_LORA)), _full((KV_LORA, HEADS * LANES)),
                      _full((B_WIDTH, KV_LORA)), _full((1, Q_LORA)), _full((1, KV_LORA)),
                      pl.BlockSpec((rows, LANES), lambda b, i: (i, 0)),
                      pl.BlockSpec((rows, LANES), lambda b, i: (i, 0)),
                      pl.BlockSpec((LANES, rows), lambda b, i: (0, i)),
                      pl.BlockSpec((LANES, rows), lambda b, i: (0, i))],
            out_specs=[row_blk(A_WIDTH), row_blk(IDX_DIM), row_blk(A_WIDTH + B_WIDTH),
                       row_blk(HEADS * LANES), t_blk(A_WIDTH, blk),
                       t_blk(IDX_DIM, HEADS * blk), t_blk(HEADS, blk), t_blk(VT_WIDTH, blk),
                       t_blk(HEADS * LANES, blk), t_blk(B_WIDTH, blk)],
            out_shape=[sds((nb, nblk * blk, A_WIDTH), BF16), sds((nb, nblk * blk, IDX_DIM), F32),
                       sds((nb, nblk * blk, A_WIDTH + B_WIDTH), F32),
                       sds((nb, nblk * blk, HEADS * LANES), BF16),
                       sds((nb, nblk, A_WIDTH, blk), BF16),
                       sds((nb, nblk, IDX_DIM, HEADS * blk), F32),
                       sds((nb, nblk, HEADS, blk), F32), sds((nb, nblk, VT_WIDTH, blk), BF16),
                       sds((nb, nblk, HEADS * LANES, blk), BF16),
                       sds((nb, nblk, B_WIDTH, blk), BF16)],
            compiler_params=params, name="proj",
        )(rows_in, row2(ln_emb_g), row2(ln_emb_b), wnn, wnt, wki, wsel, wuq_t, wk, wv_t,
          row2(q_norm_g[0]), row2(kv_norm_g[0]), *tabs)

    ka, ki, gates, kb, qat, qit, wit, vat, qbt, vbt = project(x, tabs_x, B, nq, TQ, PROJ_BLOCKS, S)
    ka_m, _, _, kb_m, _, _, _, vat_m, _, vbt_m = project(meta_pad, tabs_m, 1, 1, META_PROJ_ROWS, 1,
                                                         N_META)
    ka_m, kb_m = ka_m[0, :META_ROWS], kb_m[0, :META_ROWS]
    vat_m, vbt_m = vat_m[0, 0, :, :META_ROWS], vbt_m[0, 0, :, :META_ROWS]

    batch_rows = lambda n: pl.BlockSpec((1, S, n), lambda b, i: (b, 0, 0))
    batch_t = lambda r: pl.BlockSpec((1, nq, r, TQ), lambda b, i: (b, 0, 0, 0))
    blk = lambda r, n: pl.BlockSpec((1, 1, r, n), lambda b, i: (b, i, 0, 0))
    out_blk = pl.BlockSpec((1, TQ, A_WIDTH), lambda b, i: (b, i, 0))
    oa = pl.pallas_call(
        functools.partial(_dsa_kernel, topk=topk, seq=S),
        grid=(B, nq),
        in_specs=[pl.BlockSpec(memory_space=pltpu.SMEM), batch_rows(A_WIDTH), batch_rows(IDX_DIM),
                  batch_t(VT_WIDTH), blk(A_WIDTH, TQ), blk(IDX_DIM, HEADS * TQ), blk(HEADS, TQ),
                  _full((META_ROWS, A_WIDTH)), _full((VT_WIDTH, META_ROWS))],
        out_specs=out_blk,
        out_shape=sds((B, S, A_WIDTH), F32),
        scratch_shapes=[pltpu.VMEM((S, TQ), I32),
                        pltpu.VMEM((HEADS, 2, TQ, TQ), F32),
                        pltpu.VMEM((HEADS, META_ROWS, TQ), F32),
                        pltpu.VMEM((VT_WIDTH, TQ), F32),
                        pltpu.VMEM((HEADS, LANES, TQ), BF16),
                        pltpu.VMEM((HEADS, TQ, TQ), F32),
                        pltpu.VMEM((TQ, TQ), F32),
                        pltpu.VMEM((S // 2, TQ), I32),
                        pltpu.VMEM((S // 2, TQ), I32),
                        pltpu.VMEM((S // 2, TQ), I32)],
        compiler_params=params, name="dsa",
    )(rel_bias.astype(F32), ka, ki, vat, qat, qit, wit, ka_m, vat_m)

    row_blk = lambda n: pl.BlockSpec((1, TQ, n), lambda b, i: (b, i, 0))
    return pl.pallas_call(
        _mla_out_kernel,
        grid=(B, nq),
        in_specs=[batch_rows(HEADS * LANES), batch_t(B_WIDTH), blk(HEADS * LANES, TQ),
                  _full((META_ROWS, HEADS * LANES)), _full((B_WIDTH, META_ROWS)),
                  row_blk(D), _full((1, D)), _full((1, D)), row_blk(A_WIDTH),
                  row_blk(A_WIDTH + B_WIDTH), _full((A_WIDTH + B_WIDTH, D)), _full((1, D)),
                  _full((1, D))],
        out_specs=row_blk(D),
        out_shape=sds((B, S, D), x.dtype),
        scratch_shapes=[pltpu.VMEM((HEADS, TQ, TQ), F32),
                        pltpu.VMEM((B_WIDTH, TQ), F32)],
        compiler_params=params, name="mla_out",
    )(kb, vbt, qbt, kb_m, vbt_m, x, row2(ln_emb_g), row2(ln_emb_b), oa, gates,
      w_out[0].astype(BF16), row2(ln_post_g[0]), row2(ln_post_b[0]))
```

```python
import functools
import math

import jax
import jax.numpy as jnp
from jax import lax
from jax.experimental import pallas as pl
from jax.experimental.pallas import tpu as pltpu

F32 = jnp.float32
BF16 = jnp.bfloat16
I32 = jnp.int32
I16 = jnp.int16

N_META = 16
HEADS = 8
A_HEAD_DIM = 64
A_WIDTH = HEADS * A_HEAD_DIM
IDX_DIM = 64
TOPK_MAX = 256
B_NOPE = 64
B_ROPE = 32
B_V = 64
B_WIDTH = HEADS * B_V
Q_LORA = 256
KV_LORA = 128
ROPE_THETA = 10000.0
REL_BUCKETS = 32
REL_MAX_DIST = 128
LN_EPS = 1e-5
RMS_EPS = 1e-6
DEPTH = 1
ALPHA = (2.0 * DEPTH) ** 0.25

LANES = 128
TQ = 256
V_EXT = 80
L_ROW = 64
VT_WIDTH = HEADS * V_EXT
HALF = TQ // 2
SCAN = 2
PROJ_BLOCKS = 2
META_ROWS = 16
META_PROJ_ROWS = LANES
MASKED = -1e30
LOG2_E = math.log2(math.e)
INT_MIN = -2147483648
KEY_NEG_INF = -0x7F800000
LOWEST_I16_PAIR = -0x7FFF8000
VMEM_LIMIT = 56 * 1024 * 1024

_SPLITS = (A_WIDTH, A_WIDTH, A_WIDTH, A_WIDTH, HEADS * IDX_DIM, IDX_DIM, HEADS,
           Q_LORA, KV_LORA, B_ROPE, B_WIDTH)
_OFF = [0]
for _s in _SPLITS:
    _OFF.append(_OFF[-1] + _s)
(O_QA, O_KA, O_VA, O_GA, O_QI, O_KI, O_WI, O_CQ, O_CKV, O_KPE, O_GB, O_END) = _OFF

NN_KA = 0
NN_G = NN_KA + A_WIDTH
NN_CQ = NN_G + A_WIDTH + B_WIDTH
NN_CKV = NN_CQ + Q_LORA
NN_PE = NN_CKV + KV_LORA
NN_PER = NN_PE + LANES
NN_END = NN_PER + LANES
NT_QA = 0
NT_VA = NT_QA + A_WIDTH
NT_END = NT_VA + VT_WIDTH
SEL_WI = HEADS * IDX_DIM
SEL_END = SEL_WI + 16

_NT_DIMS = (((1,), (1,)), ((), ()))


def _dot_nt(a, b):
    return lax.dot_general(a, b, _NT_DIMS, preferred_element_type=F32)


def _layer_norm(x, g, b):
    mu = jnp.mean(x, axis=-1, keepdims=True)
    xc = x - mu
    var = jnp.mean(xc * xc, axis=-1, keepdims=True)
    return xc * lax.rsqrt(var + LN_EPS) * g + b


def _rms_norm(x, g):
    return x * lax.rsqrt(jnp.mean(x * x, axis=-1, keepdims=True) + RMS_EPS) * g


def _proj_kernel(x_ref, lng_ref, lnb_ref, wnn_ref, wnt_ref, wki_ref, wsel_ref, wuq_ref, wk_ref,
                 wv_ref, qg_ref, kvg_ref, cos_ref, sin_ref, ct_ref, st_ref,
                 ka_ref, ki_ref, g_ref, kb_ref, qat_ref, qit_ref, wit_ref, vat_ref, qbt_ref,
                 vbt_ref, *, n_valid):
    rows = x_ref.shape[1]
    nsub = qat_ref.shape[1]
    blk = rows // nsub
    y = _layer_norm(x_ref[0], lng_ref[...], lnb_ref[...])
    if n_valid < rows:
        y = jnp.where(lax.broadcasted_iota(I32, (rows, 1), 0) < n_valid, y, 0.0)
    hb = y.astype(BF16)

    def nn(lo, hi):
        return jnp.dot(hb, wnn_ref[:, lo:hi], preferred_element_type=F32)

    def nt(lo, hi):
        return _dot_nt(wnt_ref[lo:hi, :], hb)

    ka_ref[0] = nn(NN_KA, NN_G).astype(BF16)
    g_ref[0] = nn(NN_G, NN_CQ)

    qat = (nt(NT_QA, NT_VA) * (A_HEAD_DIM ** -0.5 * LOG2_E)).astype(BF16)
    ones_row = jnp.where(lax.broadcasted_iota(I32, (VT_WIDTH, 1), 0) % V_EXT == L_ROW, 1.0, 0.0)
    vat = (nt(NT_VA, NT_END) + ones_row).astype(BF16)

    ki_ref[0] = jnp.dot(y, wki_ref[...], preferred_element_type=F32)
    qit = _dot_nt(wsel_ref[0:SEL_WI, :], y)
    wit = _dot_nt(wsel_ref[SEL_WI:SEL_END, :], y) * (HEADS ** -0.5)
    wit = wit * (IDX_DIM ** -0.5)
    for s in range(nsub):
        cols = slice(s * blk, (s + 1) * blk)
        qat_ref[0, s] = qat[:, cols]
        vat_ref[0, s] = vat[:, cols]
        wit_ref[0, s] = wit[0:HEADS, cols]
        for h in range(HEADS):
            qit_ref[0, s, :, h * blk:(h + 1) * blk] = qit[h * IDX_DIM:(h + 1) * IDX_DIM, cols]

    cqn = _rms_norm(nn(NN_CQ, NN_CKV), qg_ref[...]).astype(BF16)
    ckvn = _rms_norm(nn(NN_CKV, NN_PE), kvg_ref[...]).astype(BF16)
    qbt = _dot_nt(wuq_ref[...], cqn)
    ct = ct_ref[...]
    st = st_ref[...]
    half = B_ROPE // 2
    for h in range(HEADS):
        q = qbt[h * LANES:(h + 1) * LANES]
        x1, x2 = q[B_NOPE:B_NOPE + half], q[B_NOPE + half:B_NOPE + B_ROPE]
        rot = jnp.concatenate([jnp.zeros((B_NOPE, rows), F32), -x2, x1,
                               jnp.zeros((LANES - B_NOPE - B_ROPE, rows), F32)], axis=0)
        qh = (q * ct + rot * st).astype(BF16)
        for s in range(nsub):
            qbt_ref[0, s, h * LANES:(h + 1) * LANES, :] = qh[:, s * blk:(s + 1) * blk]
    kn = jnp.dot(ckvn, wk_ref[...], preferred_element_type=F32)
    kpe = nn(NN_PE, NN_PER) * cos_ref[...] + nn(NN_PER, NN_END) * sin_ref[...]
    for h in range(HEADS):
        cols = slice(h * LANES, (h + 1) * LANES)
        kb_ref[0, :, cols] = (kn[:, cols] + kpe).astype(BF16)
    vbt = _dot_nt(wv_ref[...], ckvn).astype(BF16)
    for s in range(nsub):
        vbt_ref[0, s] = vbt[:, s * blk:(s + 1) * blk]


_MAX_EXACT = REL_BUCKETS // 2
_BUCKET_STARTS = tuple(math.ceil(_MAX_EXACT * (REL_MAX_DIST / _MAX_EXACT) ** (k / (REL_BUCKETS - _MAX_EXACT)))
                       for k in range(1, REL_BUCKETS - _MAX_EXACT))


def _t5_bucket(d):
    d = jnp.maximum(d, 0)
    large = jnp.full(d.shape, _MAX_EXACT, I32)
    for start in _BUCKET_STARTS:
        large = large + jnp.where(d >= start, 1, 0)
    return jnp.where(d < _MAX_EXACT, d, large)


def _dsa_kernel(relb_ref, ka_ref, ki_ref, vat_ref, qat_ref, qit_ref, wit_ref, kam_ref, vatm_ref,
                oa_ref, keys_s, bias_s, metab_s, acc_s, qpad_s, s_scr, madd_s, hi_s, lo_s, lo2_s,
                *, topk, seq):
    b = pl.program_id(0)
    i = pl.program_id(1)
    nch = i + 1
    row = lax.broadcasted_iota(I32, (TQ, TQ), 0)
    pos_q = i * TQ + lax.broadcasted_iota(I32, (TQ, TQ), 1)

    @pl.when((b == 0) & (i == 0))
    def _():
        sub = lax.broadcasted_iota(I32, (8, TQ), 0)
        lane = lax.broadcasted_iota(I32, (8, TQ), 1)

        def fill(dst, n_rows, offset):
            def step(r8, carry):
                r0 = pl.multiple_of(r8 * 8, 8)
                bucket = _t5_bucket(offset + lane - (r0 + sub))
                for h in range(HEADS):
                    t = jnp.zeros((8, TQ), F32)
                    for bk in range(REL_BUCKETS):
                        t = jnp.where(bucket == bk, relb_ref[bk, h], t)
                    dst[h, pl.ds(r0, 8), :] = (t - relb_ref[REL_BUCKETS - 1, h]) * LOG2_E
                return carry
            lax.fori_loop(0, n_rows // 8, step, 0)

        for dl in range(2):
            fill(bias_s.at[:, dl], TQ, dl * TQ)
        fill(metab_s, META_ROWS, N_META)

    def idx_body(c, carry, diagonal):
        k0 = pl.multiple_of(c * TQ, TQ)
        rel = jnp.dot(ki_ref[0, pl.ds(k0, TQ), :], qit_ref[0, 0],
                      preferred_element_type=F32)
        s = jnp.zeros((TQ, TQ), F32)
        for h in range(HEADS):
            s = s + jnp.maximum(rel[:, h * TQ:(h + 1) * TQ], 0.0) * wit_ref[0, 0, h:h + 1, :]
        if diagonal:
            s = jnp.where(k0 + row <= pos_q, s, -jnp.inf)
        bits = lax.bitcast_convert_type(s, I32)
        key = jnp.where(bits < 0, INT_MIN - bits, bits)
        keys_s[pl.ds(k0, TQ), :] = key
        h0 = pl.multiple_of(c * HALF, HALF)
        ka_, kb_ = key[:HALF], key[HALF:]
        hi_s[pl.ds(h0, HALF), :] = ((ka_ >> 16) & 0xFFFF) | (kb_ & -0x10000)
        lo_s[pl.ds(h0, HALF), :] = ((ka_ & 0xFFFF) ^ 0x8000) | ((kb_ << 16) ^ INT_MIN)
        return carry

    def idx_four(j, carry):
        for u in range(4):
            idx_body(4 * j + u, carry, diagonal=False)
        return carry

    lax.fori_loop(0, (nch - 1) // 4, idx_four, 0)
    lax.fori_loop((nch - 1) // 4 * 4, nch - 1, functools.partial(idx_body, diagonal=False), 0)
    idx_body(nch - 1, 0, diagonal=True)

    nscan = (nch + SCAN - 1) // SCAN

    def fill_lowest(c, carry):
        h0 = pl.multiple_of(c * HALF, HALF)
        lowest = jnp.full((HALF, TQ), LOWEST_I16_PAIR, I32)
        hi_s[pl.ds(h0, HALF), :] = lowest
        lo_s[pl.ds(h0, HALF), :] = lowest
        return carry

    lax.fori_loop(nch, nscan * SCAN, fill_lowest, 0)

    def count(pred):
        def body(c, acc):
            k0 = pl.multiple_of(c * TQ, TQ)
            hit = pred(keys_s[pl.ds(k0, TQ), :], k0 + row)
            return acc + jnp.sum(jnp.where(hit, 1, 0).reshape(TQ // 8, 8, TQ), axis=0)
        acc = lax.fori_loop(0, nch, body, jnp.zeros((8, TQ), I32))
        return jnp.sum(acc, axis=0, keepdims=True)

    def splat16(pat):
        word = pat | (pat << 16)
        return pltpu.bitcast(jnp.broadcast_to(word, (8, TQ)), I16)

    def count16(scr, pat):
        thr_p = splat16(pat)
        one, zero = jnp.int16(1), jnp.int16(0)

        def body(cs, acc):
            h0 = pl.multiple_of(cs * (SCAN * HALF), SCAN * HALF)
            k = pltpu.bitcast(scr[pl.ds(h0, SCAN * HALF), :], I16)
            parts = []
            for j in range(SCAN * TQ // 16):
                kj = k[16 * j:16 * (j + 1)]
                parts.append(jnp.where(kj >= thr_p, one, zero))
            while len(parts) > 1:
                parts = [x + y for x, y in zip(parts[::2], parts[1::2])] + parts[len(parts) & ~1:]
            return acc + parts[0].astype(I32)
        acc = lax.fori_loop(0, nscan, body, jnp.zeros((16, TQ), I32))
        return jnp.sum(acc, axis=0, keepdims=True)

    def hi_bit(it, carry):
        t_hi, cnt_t = carry
        cand = t_hi | jnp.left_shift(jnp.int32(1), 15 - it)
        cnt = count16(hi_s, cand ^ 0x8000)
        ok = cnt >= topk
        return jnp.where(ok, cand, t_hi), jnp.where(ok, cnt, cnt_t)

    zero_q = jnp.zeros((1, TQ), I32)
    t_hi, cnt_t = lax.fori_loop(0, 16, hi_bit, (zero_q, jnp.full((1, TQ), seq, I32)))
    thr_hi_p = splat16(t_hi ^ 0x8000)

    def lo_prep(c2, acc):
        h0 = pl.multiple_of(c2 * TQ, TQ)
        hi = pltpu.bitcast(hi_s[pl.ds(h0, TQ), :], I16)
        lo = pltpu.bitcast(lo_s[pl.ds(h0, TQ), :], I16)
        parts, above = [], []
        for j in range(2 * TQ // 16):
            hj = hi[16 * j:16 * (j + 1)]
            parts.append(jnp.where(hj == thr_hi_p, lo[16 * j:16 * (j + 1)], jnp.int16(-0x8000)))
            above.append(jnp.where(hj > thr_hi_p, jnp.int16(1), jnp.int16(0)))
        lo2_s[pl.ds(h0, TQ), :] = pltpu.bitcast(jnp.concatenate(parts, axis=0), I32)
        while len(above) > 1:
            above = [x + y for x, y in zip(above[::2], above[1::2])] + above[len(above) & ~1:]
        return acc + above[0].astype(I32)

    acc_gt = lax.fori_loop(0, nscan * (SCAN // 2), lo_prep, jnp.zeros((16, TQ), I32))
    cnt_gt = jnp.sum(acc_gt, axis=0, keepdims=True)

    def lo_bit(it, carry):
        t_lo, cnt_t = carry
        cand = t_lo | jnp.left_shift(jnp.int32(1), 15 - it)
        cnt = cnt_gt + count16(lo2_s, cand ^ 0x8000)
        ok = cnt >= topk
        return jnp.where(ok, cand, t_lo), jnp.where(ok, cnt, cnt_t)

    t_lo, cnt_t = lax.fori_loop(0, 16, lo_bit, (zero_q, cnt_t))
    thr = ((t_hi << 16) | t_lo) ^ INT_MIN

    tie_cut = (cnt_t > topk) & (thr > KEY_NEG_INF)
    any_cut = jnp.max(jnp.where(tie_cut, 1, 0))

    @pl.when(any_cut > 0)
    def _():
        need = topk - count(lambda k, p: k > thr)

        def jbit(it, j):
            cand = j | jnp.left_shift(jnp.int32(1), 12 - it)
            cnt = count(lambda k, p: (k == thr) & (p < cand))
            return jnp.where(cnt < need, cand, j)

        j = lax.fori_loop(0, 13, jbit, jnp.zeros((1, TQ), I32))
        j_last = jnp.where(tie_cut, j, seq)

        def demote(c, carry):
            k0 = pl.multiple_of(c * TQ, TQ)
            key = keys_s[pl.ds(k0, TQ), :]
            keys_s[pl.ds(k0, TQ), :] = jnp.where((key == thr) & (k0 + row > j_last), key - 1, key)
            return carry

        lax.fori_loop(0, nch, demote, 0)

    thr_sel = jnp.maximum(thr, KEY_NEG_INF + 1)

    for h in range(HEADS):
        qh = qat_ref[0, 0, h * A_HEAD_DIM:(h + 1) * A_HEAD_DIM, :]
        zero_half = jnp.zeros_like(qh)
        qpad_s[h] = jnp.concatenate([zero_half, qh] if h % 2 else [qh, zero_half], axis=0)

    def set_mask(c):
        k0 = pl.multiple_of(c * TQ, TQ)
        madd_s[...] = jnp.where(keys_s[pl.ds(k0, TQ), :] >= thr_sel, 0.0, MASKED)

    def scores(c, h, near):
        k0 = pl.multiple_of(c * TQ, TQ)
        pair = slice((h // 2) * LANES, (h // 2 + 1) * LANES)
        s = jnp.dot(ka_ref[0, pl.ds(k0, TQ), pair], qpad_s[h],
                    preferred_element_type=F32) + madd_s[...]
        if near:
            dl = i - c
            s = s + jnp.where(dl == 0, bias_s[h, 0], jnp.where(dl == 1, bias_s[h, 1], 0.0))
        s_scr[h] = s
        return jnp.max(s, axis=0, keepdims=True)

    def att_body(c, carry, lookahead):
        ms, cms = carry
        if lookahead is not None:
            set_mask(c + 1)
        new_ms, new_cms = [], []
        for h in range(HEADS):
            m_new = jnp.maximum(ms[h], cms[h])
            alpha = jnp.exp2(ms[h] - m_new)
            p = jnp.exp2((s_scr[h] - m_new).astype(BF16))
            new_ms.append(m_new)
            rows = slice(h * V_EXT, (h + 1) * V_EXT)
            pv = jnp.dot(vat_ref[0, c, rows, :], p, preferred_element_type=F32)
            acc_s[rows, :] = alpha * acc_s[rows, :] + pv
            if lookahead is not None:
                new_cms.append(scores(c + 1, h, near=lookahead))
        return tuple(new_ms), tuple(new_cms)

    meta_ok = lax.broadcasted_iota(I32, (META_ROWS, TQ), 0) < N_META
    ms0 = []
    for h in range(HEADS):
        pair = slice((h // 2) * LANES, (h // 2 + 1) * LANES)
        s = jnp.dot(kam_ref[:, pair], qpad_s[h], preferred_element_type=F32)
        s = jnp.where(meta_ok, s + jnp.where(i == 0, metab_s[h], 0.0), MASKED)
        m0 = jnp.max(s, axis=0, keepdims=True)
        p = jnp.exp2((s - m0).astype(BF16))
        ms0.append(m0)
        rows = slice(h * V_EXT, (h + 1) * V_EXT)
        acc_s[rows, :] = jnp.dot(vatm_ref[rows, :], p, preferred_element_type=F32)

    set_mask(0)
    init = (tuple(ms0), tuple(scores(0, h, near=True) for h in range(HEADS)))
    n_far = jnp.maximum(nch - 3, 0)

    def four_steps(j, carry):
        for u in range(4):
            carry = att_body(4 * j + u, carry, lookahead=False)
        return carry

    carry = lax.fori_loop(0, n_far // 4, four_steps, init)
    carry = lax.fori_loop(n_far // 4 * 4, n_far, functools.partial(att_body, lookahead=False), carry)
    carry = lax.fori_loop(n_far, nch - 1, functools.partial(att_body, lookahead=True), carry)
    att_body(nch - 1, carry, lookahead=None)

    outs = []
    for h in range(HEADS):
        num = acc_s[h * V_EXT:h * V_EXT + A_HEAD_DIM, :]
        den = acc_s[h * V_EXT + L_ROW:h * V_EXT + L_ROW + 1, :]
        outs.append(num * (1.0 / den))
    oa_ref[0] = jnp.concatenate(outs, axis=0).T


def _silu(x):
    return x * (1.0 / (1.0 + jnp.exp(-x)))


def _mla_out_kernel(kb_ref, vbt_ref, qbt_ref, kbm_ref, vbtm_ref, x_ref, lng_ref, lnb_ref, oa_ref,
                    g_ref, wout_ref, pg_ref, pb_ref, out_ref, s_scr, acc_s):
    i = pl.program_id(1)
    nch = i + 1
    row = lax.broadcasted_iota(I32, (TQ, TQ), 0)
    pos_q = i * TQ + lax.broadcasted_iota(I32, (TQ, TQ), 1)

    def scores(c, h, diagonal):
        k0 = pl.multiple_of(c * TQ, TQ)
        grp = slice(h * LANES, (h + 1) * LANES)
        s = jnp.dot(kb_ref[0, pl.ds(k0, TQ), grp], qbt_ref[0, 0, grp, :],
                    preferred_element_type=F32)
        if diagonal:
            s = jnp.where(k0 + row <= pos_q, s, MASKED)
        s_scr[h] = s
        return jnp.max(s, axis=0, keepdims=True)

    def body(c, carry, lookahead):
        ms, ls, cms = carry
        new_ms, new_ls, new_cms = [], [], []
        for h in range(HEADS):
            m_new = jnp.maximum(ms[h], cms[h])
            alpha = jnp.exp2(ms[h] - m_new)
            p = jnp.exp2(s_scr[h] - m_new)
            new_ls.append(alpha * ls[h] + jnp.sum(p, axis=0, keepdims=True))
            new_ms.append(m_new)
            rows = slice(h * B_V, (h + 1) * B_V)
            pv = jnp.dot(vbt_ref[0, c, rows, :], p.astype(BF16), preferred_element_type=F32)
            acc_s[rows, :] = alpha * acc_s[rows, :] + pv
            if lookahead is not None:
                new_cms.append(scores(c + 1, h, diagonal=lookahead))
        return tuple(new_ms), tuple(new_ls), tuple(new_cms)

    meta_ok = lax.broadcasted_iota(I32, (META_ROWS, TQ), 0) < N_META
    ms0, ls0 = [], []
    for h in range(HEADS):
        grp = slice(h * LANES, (h + 1) * LANES)
        s = jnp.dot(kbm_ref[:, grp], qbt_ref[0, 0, grp, :], preferred_element_type=F32)
        s = jnp.where(meta_ok, s, MASKED)
        m0 = jnp.max(s, axis=0, keepdims=True)
        p = jnp.exp2(s - m0)
        ms0.append(m0)
        ls0.append(jnp.sum(p, axis=0, keepdims=True))
        rows = slice(h * B_V, (h + 1) * B_V)
        acc_s[rows, :] = jnp.dot(vbtm_ref[rows, :], p.astype(BF16), preferred_element_type=F32)

    cms0 = tuple(scores(0, h, diagonal=True) for h in range(HEADS))
    init = (tuple(ms0), tuple(ls0), cms0)
    n_full = jnp.maximum(nch - 2, 0)

    def four_steps(j, carry):
        for u in range(4):
            carry = body(4 * j + u, carry, lookahead=False)
        return carry

    carry = lax.fori_loop(0, n_full // 4, four_steps, init)
    carry = lax.fori_loop(n_full // 4 * 4, n_full, functools.partial(body, lookahead=False), carry)
    carry = lax.fori_loop(n_full, nch - 1, functools.partial(body, lookahead=True), carry)
    ms, ls, _ = body(nch - 1, carry, lookahead=None)

    outs = []
    for h in range(HEADS):
        rows = slice(h * B_V, (h + 1) * B_V)
        outs.append(acc_s[rows, :] * (1.0 / ls[h]))
    ob = jnp.concatenate(outs, axis=0).T

    h = _layer_norm(x_ref[0], lng_ref[...], lnb_ref[...])
    g = g_ref[0]
    mixed = jnp.concatenate([oa_ref[0] * _silu(g[:, :A_WIDTH]), ob * _silu(g[:, A_WIDTH:])], axis=-1)
    out = jnp.dot(mixed.astype(BF16), wout_ref[...], preferred_element_type=F32)
    out_ref[0] = _layer_norm(ALPHA * h + out, pg_ref[...], pb_ref[...])


def _rot_cols(w):
    half = w.shape[-1] // 2
    return jnp.concatenate([-w[:, half:], w[:, :half]], axis=-1)


def _full(shape):
    return pl.BlockSpec(shape, lambda *_: (0,) * len(shape))


def _rope_tables(n, scale):
    d_q = B_NOPE + B_ROPE
    inv_freq = ROPE_THETA ** (-jnp.arange(0, B_ROPE, 2, dtype=F32) / B_ROPE)
    ang = jnp.arange(n, dtype=F32)[:, None] * inv_freq[None, :]
    pad = ((0, 0), (B_NOPE, LANES - d_q))
    cos128 = jnp.pad(jnp.tile(jnp.cos(ang), (1, 2)), pad)
    sin128 = jnp.pad(jnp.tile(jnp.sin(ang), (1, 2)), pad)
    nope = (lax.broadcasted_iota(I32, (1, LANES), 1) < B_NOPE).astype(F32)
    return cos128, sin128, ((cos128 + nope) * scale).T, (sin128 * scale).T


def kernel(x, meta_tokens, ln_emb_g, ln_emb_b, w_in, w_uq, q_norm_g, w_ukv, kv_norm_g, rel_bias,
           w_out, ln_post_g, ln_post_b):
    B, S, D = x.shape
    assert w_in.shape[0] == DEPTH == 1
    assert S % TQ == 0 and S < (1 << 13) and meta_tokens.shape[0] == N_META <= META_ROWS
    nq = S // TQ
    assert nq % SCAN == 0 and SCAN % 2 == 0
    topk = min(TOPK_MAX, S // 4) - N_META
    assert 0 < topk <= TQ

    w = w_in[0]
    d_q = B_NOPE + B_ROPE
    d_kv = B_NOPE + B_V

    def per_head(m, width):
        m = m.reshape(m.shape[0], HEADS, -1)
        return jnp.pad(m, ((0, 0), (0, 0), (0, width - m.shape[2]))).reshape(m.shape[0], -1)

    w_kpe = w[:, O_KPE:O_GB]
    pe_lanes = ((0, 0), (B_NOPE, LANES - d_q))
    wnn = jnp.concatenate([w[:, O_KA:O_VA], w[:, O_GA:O_QI], w[:, O_GB:O_END], w[:, O_CQ:O_KPE],
                           jnp.pad(w_kpe, pe_lanes), jnp.pad(_rot_cols(w_kpe), pe_lanes)],
                          axis=1).astype(BF16)
    assert wnn.shape[1] == NN_END
    wnt = jnp.concatenate([w[:, O_QA:O_KA], per_head(w[:, O_VA:O_GA], V_EXT)], axis=1).T.astype(BF16)
    assert wnt.shape[0] == NT_END
    wki = w[:, O_KI:O_WI].astype(F32)
    wsel = jnp.concatenate([w[:, O_QI:O_KI], jnp.pad(w[:, O_WI:O_CQ], ((0, 0), (0, 8)))],
                           axis=1).T.astype(F32)
    assert wsel.shape[0] == SEL_END
    wuq_t = per_head(w_uq[0], LANES).T.astype(BF16)
    wkv = w_ukv[0].reshape(KV_LORA, HEADS, d_kv)
    wk = per_head(wkv[:, :, :B_NOPE].reshape(KV_LORA, -1), LANES).astype(BF16)
    wv_t = wkv[:, :, B_NOPE:].reshape(KV_LORA, -1).T.astype(BF16)

    assert N_META + S >= META_PROJ_ROWS
    tabs = _rope_tables(N_META + S, d_q ** -0.5 * LOG2_E)
    tabs_x = (tabs[0][N_META:], tabs[1][N_META:], tabs[2][:, N_META:], tabs[3][:, N_META:])
    tabs_m = (tabs[0][:META_PROJ_ROWS], tabs[1][:META_PROJ_ROWS],
              tabs[2][:, :META_PROJ_ROWS], tabs[3][:, :META_PROJ_ROWS])

    row2 = lambda v: v.reshape(1, -1).astype(F32)
    meta_pad = jnp.pad(meta_tokens.astype(x.dtype), ((0, META_PROJ_ROWS - N_META), (0, 0)))[None]

    params = pltpu.CompilerParams(dimension_semantics=("arbitrary", "arbitrary"),
                                  vmem_limit_bytes=VMEM_LIMIT)
    sds = jax.ShapeDtypeStruct

    def project(rows_in, tabs, nb, nblk, blk, nsub, n_valid):
        rows = nsub * blk
        row_blk = lambda n: pl.BlockSpec((1, rows, n), lambda b, i: (b, i, 0))
        t_blk = lambda r, n: pl.BlockSpec((1, nsub, r, n), lambda b, i: (b, i, 0, 0))
        return pl.pallas_call(
            functools.partial(_proj_kernel, n_valid=n_valid),
            grid=(nb, nblk // nsub),
            in_specs=[row_blk(D), _full((1, D)), _full((1, D)), _full((D, NN_END)),
                      _full((NT_END, D)), _full((D, IDX_DIM)), _full((SEL_END, D)),
                      _full((HEADS * LANES, Q_LORA)), _full((KV_LORA, HEADS * LANES)),
                      _full((B_WIDTH, KV_LORA)), _full((1, Q_LORA)), _full((1, KV_LORA)),
                      pl.BlockSpec((rows, LANES), lambda b, i: (i, 0)),
                      pl.BlockSpec((rows, LANES), lambda b, i: (i, 0)),
                      pl.BlockSpec((LANES, rows), lambda b, i: (0, i)),
                      pl.BlockSpec((LANES, rows), lambda b, i: (0, i))],
            out_specs=[row_blk(A_WIDTH), row_blk(IDX_DIM), row_blk(A_WIDTH + B_WIDTH),
                       row_blk(HEADS * LANES), t_blk(A_WIDTH, blk),
                       t_blk(IDX_DIM, HEADS * blk), t_blk(HEADS, blk), t_blk(VT_WIDTH, blk),
                       t_blk(HEADS * LANES, blk), t_blk(B_WIDTH, blk)],
            out_shape=[sds((nb, nblk * blk, A_WIDTH), BF16), sds((nb, nblk * blk, IDX_DIM), F32),
                       sds((nb, nblk * blk, A_WIDTH + B_WIDTH), F32),
                       sds((nb, nblk * blk, HEADS * LANES), BF16),
                       sds((nb, nblk, A_WIDTH, blk), BF16),
                       sds((nb, nblk, IDX_DIM, HEADS * blk), F32),
                       sds((nb, nblk, HEADS, blk), F32), sds((nb, nblk, VT_WIDTH, blk), BF16),
                       sds((nb, nblk, HEADS * LANES, blk), BF16),
                       sds((nb, nblk, B_WIDTH, blk), BF16)],
            compiler_params=params, name="proj",
        )(rows_in, row2(ln_emb_g), row2(ln_emb_b), wnn, wnt, wki, wsel, wuq_t, wk, wv_t,
          row2(q_norm_g[0]), row2(kv_norm_g[0]), *tabs)

    ka, ki, gates, kb, qat, qit, wit, vat, qbt, vbt = project(x, tabs_x, B, nq, TQ, PROJ_BLOCKS, S)
    ka_m, _, _, kb_m, _, _, _, vat_m, _, vbt_m = project(meta_pad, tabs_m, 1, 1, META_PROJ_ROWS, 1,
                                                         N_META)
    ka_m, kb_m = ka_m[0, :META_ROWS], kb_m[0, :META_ROWS]
    vat_m, vbt_m = vat_m[0, 0, :, :META_ROWS], vbt_m[0, 0, :, :META_ROWS]

    batch_rows = lambda n: pl.BlockSpec((1, S, n), lambda b, i: (b, 0, 0))
    batch_t = lambda r: pl.BlockSpec((1, nq, r, TQ), lambda b, i: (b, 0, 0, 0))
    blk = lambda r, n: pl.BlockSpec((1, 1, r, n), lambda b, i: (b, i, 0, 0))
    out_blk = pl.BlockSpec((1, TQ, A_WIDTH), lambda b, i: (b, i, 0))
    oa = pl.pallas_call(
        functools.partial(_dsa_kernel, topk=topk, seq=S),
        grid=(B, nq),
        in_specs=[pl.BlockSpec(memory_space=pltpu.SMEM), batch_rows(A_WIDTH), batch_rows(IDX_DIM),
                  batch_t(VT_WIDTH), blk(A_WIDTH, TQ), blk(IDX_DIM, HEADS * TQ), blk(HEADS, TQ),
                  _full((META_ROWS, A_WIDTH)), _full((VT_WIDTH, META_ROWS))],
        out_specs=out_blk,
        out_shape=sds((B, S, A_WIDTH), F32),
        scratch_shapes=[pltpu.VMEM((S, TQ), I32),
                        pltpu.VMEM((HEADS, 2, TQ, TQ), F32),
                        pltpu.VMEM((HEADS, META_ROWS, TQ), F32),
                        pltpu.VMEM((VT_WIDTH, TQ), F32),
                        pltpu.VMEM((HEADS, LANES, TQ), BF16),
                        pltpu.VMEM((HEADS, TQ, TQ), F32),
                        pltpu.VMEM((TQ, TQ), F32),
                        pltpu.VMEM((S // 2, TQ), I32),
                        pltpu.VMEM((S // 2, TQ), I32),
                        pltpu.VMEM((S // 2, TQ), I32)],
        compiler_params=params, name="dsa",
    )(rel_bias.astype(F32), ka, ki, vat, qat, qit, wit, ka_m, vat_m)

    row_blk = lambda n: pl.BlockSpec((1, TQ, n), lambda b, i: (b, i, 0))
    return pl.pallas_call(
        _mla_out_kernel,
        grid=(B, nq),
        in_specs=[batch_rows(HEADS * LANES), batch_t(B_WIDTH), blk(HEADS * LANES, TQ),
                  _full((META_ROWS, HEADS * LANES)), _full((B_WIDTH, META_ROWS)),
                  row_blk(D), _full((1, D)), _full((1, D)), row_blk(A_WIDTH),
                  row_blk(A_WIDTH + B_WIDTH), _full((A_WIDTH + B_WIDTH, D)), _full((1, D)),
                  _full((1, D))],
        out_specs=row_blk(D),
        out_shape=sds((B, S, D), x.dtype),
        scratch_shapes=[pltpu.VMEM((HEADS, TQ, TQ), F32),
                        pltpu.VMEM((B_WIDTH, TQ), F32)],
        compiler_params=params, name="mla_out",
    )(kb, vbt, qbt, kb_m, vbt_m, x, row2(ln_emb_g), row2(ln_emb_b), oa, gates,
      w_out[0].astype(BF16), row2(ln_post_g[0]), row2(ln_post_b[0]))
```

```python
import functools
import math

import jax
import jax.numpy as jnp
from jax import lax
from jax.experimental import pallas as pl
from jax.experimental.pallas import tpu as pltpu

F32 = jnp.float32
BF16 = jnp.bfloat16
I32 = jnp.int32
I16 = jnp.int16

N_META = 16
HEADS = 8
A_HEAD_DIM = 64
A_WIDTH = HEADS * A_HEAD_DIM
IDX_DIM = 64
TOPK_MAX = 256
B_NOPE = 64
B_ROPE = 32
B_V = 64
B_WIDTH = HEADS * B_V
Q_LORA = 256
KV_LORA = 128
ROPE_THETA = 10000.0
REL_BUCKETS = 32
REL_MAX_DIST = 128
LN_EPS = 1e-5
RMS_EPS = 1e-6
DEPTH = 1
ALPHA = (2.0 * DEPTH) ** 0.25

LANES = 128
TQ = 256
V_EXT = 80
L_ROW = 64
VT_WIDTH = HEADS * V_EXT
HALF = TQ // 2
SCAN = 2
PROJ_BLOCKS = 2
META_ROWS = 16
META_PROJ_ROWS = LANES
MASKED = -1e30
LOG2_E = math.log2(math.e)
INT_MIN = -2147483648
KEY_NEG_INF = -0x7F800000
LOWEST_I16_PAIR = -0x7FFF8000
BF16_NEG_INF_PAIR = -0x007F0080
VMEM_LIMIT = 56 * 1024 * 1024

_SPLITS = (A_WIDTH, A_WIDTH, A_WIDTH, A_WIDTH, HEADS * IDX_DIM, IDX_DIM, HEADS,
           Q_LORA, KV_LORA, B_ROPE, B_WIDTH)
_OFF = [0]
for _s in _SPLITS:
    _OFF.append(_OFF[-1] + _s)
(O_QA, O_KA, O_VA, O_GA, O_QI, O_KI, O_WI, O_CQ, O_CKV, O_KPE, O_GB, O_END) = _OFF

NN_KA = 0
NN_G = NN_KA + A_WIDTH
NN_CQ = NN_G + A_WIDTH + B_WIDTH
NN_CKV = NN_CQ + Q_LORA
NN_PE = NN_CKV + KV_LORA
NN_PER = NN_PE + LANES
NN_END = NN_PER + LANES
NT_QA = 0
NT_VA = NT_QA + A_WIDTH
NT_END = NT_VA + VT_WIDTH
SEL_WI = HEADS * IDX_DIM
SEL_END = SEL_WI + 16

_NT_DIMS = (((1,), (1,)), ((), ()))


def _dot_nt(a, b):
    return lax.dot_general(a, b, _NT_DIMS, preferred_element_type=F32)


def _layer_norm(x, g, b):
    mu = jnp.mean(x, axis=-1, keepdims=True)
    xc = x - mu
    var = jnp.mean(xc * xc, axis=-1, keepdims=True)
    return xc * lax.rsqrt(var + LN_EPS) * g + b


def _rms_norm(x, g):
    return x * lax.rsqrt(jnp.mean(x * x, axis=-1, keepdims=True) + RMS_EPS) * g


def _proj_kernel(x_ref, lng_ref, lnb_ref, wnn_ref, wnt_ref, wki_ref, wsel_ref, wuq_ref, wk_ref,
                 wv_ref, qg_ref, kvg_ref, cos_ref, sin_ref, ct_ref, st_ref,
                 ka_ref, ki_ref, g_ref, kb_ref, qat_ref, qit_ref, wit_ref, vat_ref, qbt_ref,
                 vbt_ref, *, n_valid):
    rows = x_ref.shape[1]
    nsub = qat_ref.shape[1]
    blk = rows // nsub
    y = _layer_norm(x_ref[0], lng_ref[...], lnb_ref[...])
    if n_valid < rows:
        y = jnp.where(lax.broadcasted_iota(I32, (rows, 1), 0) < n_valid, y, 0.0)
    hb = y.astype(BF16)

    def nn(lo, hi):
        return jnp.dot(hb, wnn_ref[:, lo:hi], preferred_element_type=F32)

    def nt(lo, hi):
        return _dot_nt(wnt_ref[lo:hi, :], hb)

    ka_ref[0] = nn(NN_KA, NN_G).astype(BF16)
    g_ref[0] = nn(NN_G, NN_CQ)

    qat = (nt(NT_QA, NT_VA) * (A_HEAD_DIM ** -0.5 * LOG2_E)).astype(BF16)
    ones_row = jnp.where(lax.broadcasted_iota(I32, (VT_WIDTH, 1), 0) % V_EXT == L_ROW, 1.0, 0.0)
    vat = (nt(NT_VA, NT_END) + ones_row).astype(BF16)

    ki_ref[0] = jnp.dot(y, wki_ref[...], preferred_element_type=F32)
    qit = _dot_nt(wsel_ref[0:SEL_WI, :], y)
    wit = _dot_nt(wsel_ref[SEL_WI:SEL_END, :], y) * (HEADS ** -0.5)
    wit = wit * (IDX_DIM ** -0.5)
    for s in range(nsub):
        cols = slice(s * blk, (s + 1) * blk)
        qat_ref[0, s] = qat[:, cols]
        vat_ref[0, s] = vat[:, cols]
        wit_ref[0, s] = wit[0:HEADS, cols]
        for h in range(HEADS):
            qit_ref[0, s, :, h * blk:(h + 1) * blk] = qit[h * IDX_DIM:(h + 1) * IDX_DIM, cols]

    cqn = _rms_norm(nn(NN_CQ, NN_CKV), qg_ref[...]).astype(BF16)
    ckvn = _rms_norm(nn(NN_CKV, NN_PE), kvg_ref[...]).astype(BF16)
    qbt = _dot_nt(wuq_ref[...], cqn)
    ct = ct_ref[...]
    st = st_ref[...]
    half = B_ROPE // 2
    for h in range(HEADS):
        q = qbt[h * LANES:(h + 1) * LANES]
        x1, x2 = q[B_NOPE:B_NOPE + half], q[B_NOPE + half:B_NOPE + B_ROPE]
        rot = jnp.concatenate([jnp.zeros((B_NOPE, rows), F32), -x2, x1,
                               jnp.zeros((LANES - B_NOPE - B_ROPE, rows), F32)], axis=0)
        qh = (q * ct + rot * st).astype(BF16)
        for s in range(nsub):
            qbt_ref[0, s, h * LANES:(h + 1) * LANES, :] = qh[:, s * blk:(s + 1) * blk]
    kn = jnp.dot(ckvn, wk_ref[...], preferred_element_type=F32)
    kpe = nn(NN_PE, NN_PER) * cos_ref[...] + nn(NN_PER, NN_END) * sin_ref[...]
    for h in range(HEADS):
        cols = slice(h * LANES, (h + 1) * LANES)
        kb_ref[0, :, cols] = (kn[:, cols] + kpe).astype(BF16)
    vbt = _dot_nt(wv_ref[...], ckvn).astype(BF16)
    for s in range(nsub):
        vbt_ref[0, s] = vbt[:, s * blk:(s + 1) * blk]


_MAX_EXACT = REL_BUCKETS // 2
_BUCKET_STARTS = tuple(math.ceil(_MAX_EXACT * (REL_MAX_DIST / _MAX_EXACT) ** (k / (REL_BUCKETS - _MAX_EXACT)))
                       for k in range(1, REL_BUCKETS - _MAX_EXACT))


def _t5_bucket(d):
    d = jnp.maximum(d, 0)
    large = jnp.full(d.shape, _MAX_EXACT, I32)
    for start in _BUCKET_STARTS:
        large = large + jnp.where(d >= start, 1, 0)
    return jnp.where(d < _MAX_EXACT, d, large)


def _bf16_pattern(n):
    return jnp.where(n < 0, 0x8000 - n, n)


def _dsa_kernel(relb_ref, ka_ref, ki_ref, vat_ref, qat_ref, qit_ref, wit_ref, kam_ref, vatm_ref,
                oa_ref, keys_s, bias_s, metab_s, acc_s, qpad_s, s_scr, madd_s, hi_s, lo_s, lo2_s,
                *, topk, seq):
    b = pl.program_id(0)
    i = pl.program_id(1)
    nch = i + 1
    row = lax.broadcasted_iota(I32, (TQ, TQ), 0)
    pos_q = i * TQ + lax.broadcasted_iota(I32, (TQ, TQ), 1)

    @pl.when((b == 0) & (i == 0))
    def _():
        sub = lax.broadcasted_iota(I32, (8, TQ), 0)
        lane = lax.broadcasted_iota(I32, (8, TQ), 1)

        def fill(dst, n_rows, offset):
            def step(r8, carry):
                r0 = pl.multiple_of(r8 * 8, 8)
                bucket = _t5_bucket(offset + lane - (r0 + sub))
                for h in range(HEADS):
                    t = jnp.zeros((8, TQ), F32)
                    for bk in range(REL_BUCKETS):
                        t = jnp.where(bucket == bk, relb_ref[bk, h], t)
                    dst[h, pl.ds(r0, 8), :] = (t - relb_ref[REL_BUCKETS - 1, h]) * LOG2_E
                return carry
            lax.fori_loop(0, n_rows // 8, step, 0)

        for dl in range(2):
            fill(bias_s.at[:, dl], TQ, dl * TQ)
        fill(metab_s, META_ROWS, N_META)

    def idx_body(c, carry, diagonal):
        k0 = pl.multiple_of(c * TQ, TQ)
        rel = jnp.dot(ki_ref[0, pl.ds(k0, TQ), :], qit_ref[0, 0],
                      preferred_element_type=F32)
        s = jnp.zeros((TQ, TQ), F32)
        for h in range(HEADS):
            s = s + jnp.maximum(rel[:, h * TQ:(h + 1) * TQ], 0.0) * wit_ref[0, 0, h:h + 1, :]
        if diagonal:
            s = jnp.where(k0 + row <= pos_q, s, -jnp.inf)
        bits = lax.bitcast_convert_type(s, I32)
        key = jnp.where(bits < 0, INT_MIN - bits, bits)
        keys_s[pl.ds(k0, TQ), :] = key
        h0 = pl.multiple_of(c * HALF, HALF)
        hp = _bf16_pattern(key >> 16)
        hi_s[pl.ds(h0, HALF), :] = hp[:HALF] | (hp[HALF:] << 16)
        ka_, kb_ = key[:HALF], key[HALF:]
        lo_s[pl.ds(h0, HALF), :] = ((ka_ & 0xFFFF) ^ 0x8000) | ((kb_ << 16) ^ INT_MIN)
        return carry

    def idx_four(j, carry):
        for u in range(4):
            idx_body(4 * j + u, carry, diagonal=False)
        return carry

    lax.fori_loop(0, (nch - 1) // 4, idx_four, 0)
    lax.fori_loop((nch - 1) // 4 * 4, nch - 1, functools.partial(idx_body, diagonal=False), 0)
    idx_body(nch - 1, 0, diagonal=True)

    nscan = (nch + SCAN - 1) // SCAN

    def fill_lowest(c, carry):
        h0 = pl.multiple_of(c * HALF, HALF)
        hi_s[pl.ds(h0, HALF), :] = jnp.full((HALF, TQ), BF16_NEG_INF_PAIR, I32)
        lo_s[pl.ds(h0, HALF), :] = jnp.full((HALF, TQ), LOWEST_I16_PAIR, I32)
        return carry

    lax.fori_loop(nch, nscan * SCAN, fill_lowest, 0)

    def count(pred):
        def body(c, acc):
            k0 = pl.multiple_of(c * TQ, TQ)
            hit = pred(keys_s[pl.ds(k0, TQ), :], k0 + row)
            return acc + jnp.sum(jnp.where(hit, 1, 0).reshape(TQ // 8, 8, TQ), axis=0)
        acc = lax.fori_loop(0, nch, body, jnp.zeros((8, TQ), I32))
        return jnp.sum(acc, axis=0, keepdims=True)

    def splat16(pat, dtype):
        word = pat | (pat << 16)
        return pltpu.bitcast(jnp.broadcast_to(word, (8, TQ)), dtype)

    def tree_count(parts):
        while len(parts) > 1:
            parts = [x + y for x, y in zip(parts[::2], parts[1::2])] + parts[len(parts) & ~1:]
        total = parts[0]
        return (total.astype(F32) if total.dtype == BF16 else total).astype(I32)

    def count16(scr, pat, dtype):
        thr_p = splat16(pat, dtype)
        one, zero = jnp.ones((), dtype), jnp.zeros((), dtype)

        def body(cs, acc):
            h0 = pl.multiple_of(cs * (SCAN * HALF), SCAN * HALF)
            k = pltpu.bitcast(scr[pl.ds(h0, SCAN * HALF), :], dtype)
            return acc + tree_count([jnp.where(k[16 * j:16 * (j + 1)] >= thr_p, one, zero)
                                     for j in range(SCAN * TQ // 16)])
        acc = lax.fori_loop(0, nscan, body, jnp.zeros((16, TQ), I32))
        return jnp.sum(acc, axis=0, keepdims=True)

    def hi_bit(it, carry):
        t_hi, cnt_t = carry
        cand = t_hi | jnp.left_shift(jnp.int32(1), 15 - it)
        n = cand - 0x8000
        pat = _bf16_pattern(n)
        pat = jnp.where((n >= 1) & (n < 0x80), 0x0080, pat)
        pat = jnp.where((n <= -1) & (n > -0x80), 0x0000, pat)
        pat = jnp.where(n < -0x7F80, 0xFF80, pat)
        cnt = count16(hi_s, pat, BF16)
        ok = cnt >= topk
        return jnp.where(ok, cand, t_hi), jnp.where(ok, cnt, cnt_t)

    zero_q = jnp.zeros((1, TQ), I32)
    t_hi, cnt_t = lax.fori_loop(0, 16, hi_bit, (zero_q, jnp.full((1, TQ), seq, I32)))
    thr_hi_p = splat16(_bf16_pattern(t_hi - 0x8000), BF16)

    def lo_prep(c2, acc):
        h0 = pl.multiple_of(c2 * TQ, TQ)
        hi = pltpu.bitcast(hi_s[pl.ds(h0, TQ), :], BF16)
        lo = pltpu.bitcast(lo_s[pl.ds(h0, TQ), :], I16)
        parts, above = [], []
        for j in range(2 * TQ // 16):
            hj = hi[16 * j:16 * (j + 1)]
            parts.append(jnp.where(hj == thr_hi_p, lo[16 * j:16 * (j + 1)], jnp.int16(-0x8000)))
            above.append(jnp.where(hj > thr_hi_p, jnp.bfloat16(1), jnp.bfloat16(0)))
        lo2_s[pl.ds(h0, TQ), :] = pltpu.bitcast(jnp.concatenate(parts, axis=0), I32)
        return acc + tree_count(above)

    acc_gt = lax.fori_loop(0, nscan * (SCAN // 2), lo_prep, jnp.zeros((16, TQ), I32))
    cnt_gt = jnp.sum(acc_gt, axis=0, keepdims=True)

    def lo_bit(it, carry):
        t_lo, cnt_t = carry
        cand = t_lo | jnp.left_shift(jnp.int32(1), 15 - it)
        cnt = cnt_gt + count16(lo2_s, cand ^ 0x8000, I16)
        ok = cnt >= topk
        return jnp.where(ok, cand, t_lo), jnp.where(ok, cnt, cnt_t)

    t_lo, cnt_t = lax.fori_loop(0, 16, lo_bit, (zero_q, cnt_t))
    thr = ((t_hi << 16) | t_lo) ^ INT_MIN

    tie_cut = (cnt_t > topk) & (thr > KEY_NEG_INF)
    any_cut = jnp.max(jnp.where(tie_cut, 1, 0))

    @pl.when(any_cut > 0)
    def _():
        need = topk - count(lambda k, p: k > thr)

        def jbit(it, j):
            cand = j | jnp.left_shift(jnp.int32(1), 12 - it)
            cnt = count(lambda k, p: (k == thr) & (p < cand))
            return jnp.where(cnt < need, cand, j)

        j = lax.fori_loop(0, 13, jbit, jnp.zeros((1, TQ), I32))
        j_last = jnp.where(tie_cut, j, seq)

        def demote(c, carry):
            k0 = pl.multiple_of(c * TQ, TQ)
            key = keys_s[pl.ds(k0, TQ), :]
            keys_s[pl.ds(k0, TQ), :] = jnp.where((key == thr) & (k0 + row > j_last), key - 1, key)
            return carry

        lax.fori_loop(0, nch, demote, 0)

    thr_sel = jnp.maximum(thr, KEY_NEG_INF + 1)

    for h in range(HEADS):
        qh = qat_ref[0, 0, h * A_HEAD_DIM:(h + 1) * A_HEAD_DIM, :]
        zero_half = jnp.zeros_like(qh)
        qpad_s[h] = jnp.concatenate([zero_half, qh] if h % 2 else [qh, zero_half], axis=0)

    def set_mask(c):
        k0 = pl.multiple_of(c * TQ, TQ)
        madd_s[...] = jnp.where(keys_s[pl.ds(k0, TQ), :] >= thr_sel, 0.0, MASKED)

    def scores(c, h, near):
        k0 = pl.multiple_of(c * TQ, TQ)
        pair = slice((h // 2) * LANES, (h // 2 + 1) * LANES)
        s = jnp.dot(ka_ref[0, pl.ds(k0, TQ), pair], qpad_s[h],
                    preferred_element_type=F32) + madd_s[...]
        if near:
            dl = i - c
            s = s + jnp.where(dl == 0, bias_s[h, 0], jnp.where(dl == 1, bias_s[h, 1], 0.0))
        s_scr[h] = s
        return jnp.max(s, axis=0, keepdims=True)

    def att_body(c, carry, lookahead):
        ms, cms = carry
        if lookahead is not None:
            set_mask(c + 1)
        new_ms, new_cms = [], []
        for h in range(HEADS):
            m_new = jnp.maximum(ms[h], cms[h])
            alpha = jnp.exp2(ms[h] - m_new)
            p = jnp.exp2((s_scr[h] - m_new).astype(BF16))
            new_ms.append(m_new)
            rows = slice(h * V_EXT, (h + 1) * V_EXT)
            pv = jnp.dot(vat_ref[0, c, rows, :], p, preferred_element_type=F32)
            acc_s[rows, :] = alpha * acc_s[rows, :] + pv
            if lookahead is not None:
                new_cms.append(scores(c + 1, h, near=lookahead))
        return tuple(new_ms), tuple(new_cms)

    meta_ok = lax.broadcasted_iota(I32, (META_ROWS, TQ), 0) < N_META
    ms0 = []
    for h in range(HEADS):
        pair = slice((h // 2) * LANES, (h // 2 + 1) * LANES)
        s = jnp.dot(kam_ref[:, pair], qpad_s[h], preferred_element_type=F32)
        s = jnp.where(meta_ok, s + jnp.where(i == 0, metab_s[h], 0.0), MASKED)
        m0 = jnp.max(s, axis=0, keepdims=True)
        p = jnp.exp2((s - m0).astype(BF16))
        ms0.append(m0)
        rows = slice(h * V_EXT, (h + 1) * V_EXT)
        acc_s[rows, :] = jnp.dot(vatm_ref[rows, :], p, preferred_element_type=F32)

    set_mask(0)
    init = (tuple(ms0), tuple(scores(0, h, near=True) for h in range(HEADS)))
    n_far = jnp.maximum(nch - 3, 0)

    def four_steps(j, carry):
        for u in range(4):
            carry = att_body(4 * j + u, carry, lookahead=False)
        return carry

    carry = lax.fori_loop(0, n_far // 4, four_steps, init)
    carry = lax.fori_loop(n_far // 4 * 4, n_far, functools.partial(att_body, lookahead=False), carry)
    carry = lax.fori_loop(n_far, nch - 1, functools.partial(att_body, lookahead=True), carry)
    att_body(nch - 1, carry, lookahead=None)

    outs = []
    for h in range(HEADS):
        num = acc_s[h * V_EXT:h * V_EXT + A_HEAD_DIM, :]
        den = acc_s[h * V_EXT + L_ROW:h * V_EXT + L_ROW + 1, :]
        outs.append(num * (1.0 / den))
    oa_ref[0] = jnp.concatenate(outs, axis=0).T


def _silu(x):
    return x * (1.0 / (1.0 + jnp.exp(-x)))


def _mla_out_kernel(kb_ref, vbt_ref, qbt_ref, kbm_ref, vbtm_ref, x_ref, lng_ref, lnb_ref, oa_ref,
                    g_ref, wout_ref, pg_ref, pb_ref, out_ref, s_scr, acc_s):
    i = pl.program_id(1)
    nch = i + 1
    row = lax.broadcasted_iota(I32, (TQ, TQ), 0)
    pos_q = i * TQ + lax.broadcasted_iota(I32, (TQ, TQ), 1)

    def scores(c, h, diagonal):
        k0 = pl.multiple_of(c * TQ, TQ)
        grp = slice(h * LANES, (h + 1) * LANES)
        s = jnp.dot(kb_ref[0, pl.ds(k0, TQ), grp], qbt_ref[0, 0, grp, :],
                    preferred_element_type=F32)
        if diagonal:
            s = jnp.where(k0 + row <= pos_q, s, MASKED)
        s_scr[h] = s
        return jnp.max(s, axis=0, keepdims=True)

    def body(c, carry, lookahead):
        ms, ls, cms = carry
        new_ms, new_ls, new_cms = [], [], []
        for h in range(HEADS):
            m_new = jnp.maximum(ms[h], cms[h])
            alpha = jnp.exp2(ms[h] - m_new)
            p = jnp.exp2(s_scr[h] - m_new)
            new_ls.append(alpha * ls[h] + jnp.sum(p, axis=0, keepdims=True))
            new_ms.append(m_new)
            rows = slice(h * B_V, (h + 1) * B_V)
            pv = jnp.dot(vbt_ref[0, c, rows, :], p.astype(BF16), preferred_element_type=F32)
            acc_s[rows, :] = alpha * acc_s[rows, :] + pv
            if lookahead is not None:
                new_cms.append(scores(c + 1, h, diagonal=lookahead))
        return tuple(new_ms), tuple(new_ls), tuple(new_cms)

    meta_ok = lax.broadcasted_iota(I32, (META_ROWS, TQ), 0) < N_META
    ms0, ls0 = [], []
    for h in range(HEADS):
        grp = slice(h * LANES, (h + 1) * LANES)
        s = jnp.dot(kbm_ref[:, grp], qbt_ref[0, 0, grp, :], preferred_element_type=F32)
        s = jnp.where(meta_ok, s, MASKED)
        m0 = jnp.max(s, axis=0, keepdims=True)
        p = jnp.exp2(s - m0)
        ms0.append(m0)
        ls0.append(jnp.sum(p, axis=0, keepdims=True))
        rows = slice(h * B_V, (h + 1) * B_V)
        acc_s[rows, :] = jnp.dot(vbtm_ref[rows, :], p.astype(BF16), preferred_element_type=F32)

    cms0 = tuple(scores(0, h, diagonal=True) for h in range(HEADS))
    init = (tuple(ms0), tuple(ls0), cms0)
    n_full = jnp.maximum(nch - 2, 0)

    def four_steps(j, carry):
        for u in range(4):
            carry = body(4 * j + u, carry, lookahead=False)
        return carry

    carry = lax.fori_loop(0, n_full // 4, four_steps, init)
    carry = lax.fori_loop(n_full // 4 * 4, n_full, functools.partial(body, lookahead=False), carry)
    carry = lax.fori_loop(n_full, nch - 1, functools.partial(body, lookahead=True), carry)
    ms, ls, _ = body(nch - 1, carry, lookahead=None)

    outs = []
    for h in range(HEADS):
        rows = slice(h * B_V, (h + 1) * B_V)
        outs.append(acc_s[rows, :] * (1.0 / ls[h]))
    ob = jnp.concatenate(outs, axis=0).T

    h = _layer_norm(x_ref[0], lng_ref[...], lnb_ref[...])
    g = g_ref[0]
    mixed = jnp.concatenate([oa_ref[0] * _silu(g[:, :A_WIDTH]), ob * _silu(g[:, A_WIDTH:])], axis=-1)
    out = jnp.dot(mixed.astype(BF16), wout_ref[...], preferred_element_type=F32)
    out_ref[0] = _layer_norm(ALPHA * h + out, pg_ref[...], pb_ref[...])


def _rot_cols(w):
    half = w.shape[-1] // 2
    return jnp.concatenate([-w[:, half:], w[:, :half]], axis=-1)


def _full(shape):
    return pl.BlockSpec(shape, lambda *_: (0,) * len(shape))


def _rope_tables(n, scale):
    d_q = B_NOPE + B_ROPE
    inv_freq = ROPE_THETA ** (-jnp.arange(0, B_ROPE, 2, dtype=F32) / B_ROPE)
    ang = jnp.arange(n, dtype=F32)[:, None] * inv_freq[None, :]
    pad = ((0, 0), (B_NOPE, LANES - d_q))
    cos128 = jnp.pad(jnp.tile(jnp.cos(ang), (1, 2)), pad)
    sin128 = jnp.pad(jnp.tile(jnp.sin(ang), (1, 2)), pad)
    nope = (lax.broadcasted_iota(I32, (1, LANES), 1) < B_NOPE).astype(F32)
    return cos128, sin128, ((cos128 + nope) * scale).T, (sin128 * scale).T


def kernel(x, meta_tokens, ln_emb_g, ln_emb_b, w_in, w_uq, q_norm_g, w_ukv, kv_norm_g, rel_bias,
           w_out, ln_post_g, ln_post_b):
    B, S, D = x.shape
    assert w_in.shape[0] == DEPTH == 1
    assert S % TQ == 0 and S < (1 << 13) and meta_tokens.shape[0] == N_META <= META_ROWS
    nq = S // TQ
    assert nq % SCAN == 0 and SCAN % 2 == 0
    topk = min(TOPK_MAX, S // 4) - N_META
    assert 0 < topk <= TQ

    w = w_in[0]
    d_q = B_NOPE + B_ROPE
    d_kv = B_NOPE + B_V

    def per_head(m, width):
        m = m.reshape(m.shape[0], HEADS, -1)
        return jnp.pad(m, ((0, 0), (0, 0), (0, width - m.shape[2]))).reshape(m.shape[0], -1)

    w_kpe = w[:, O_KPE:O_GB]
    pe_lanes = ((0, 0), (B_NOPE, LANES - d_q))
    wnn = jnp.concatenate([w[:, O_KA:O_VA], w[:, O_GA:O_QI], w[:, O_GB:O_END], w[:, O_CQ:O_KPE],
                           jnp.pad(w_kpe, pe_lanes), jnp.pad(_rot_cols(w_kpe), pe_lanes)],
                          axis=1).astype(BF16)
    assert wnn.shape[1] == NN_END
    wnt = jnp.concatenate([w[:, O_QA:O_KA], per_head(w[:, O_VA:O_GA], V_EXT)], axis=1).T.astype(BF16)
    assert wnt.shape[0] == NT_END
    wki = w[:, O_KI:O_WI].astype(F32)
    wsel = jnp.concatenate([w[:, O_QI:O_KI], jnp.pad(w[:, O_WI:O_CQ], ((0, 0), (0, 8)))],
                           axis=1).T.astype(F32)
    assert wsel.shape[0] == SEL_END
    wuq_t = per_head(w_uq[0], LANES).T.astype(BF16)
    wkv = w_ukv[0].reshape(KV_LORA, HEADS, d_kv)
    wk = per_head(wkv[:, :, :B_NOPE].reshape(KV_LORA, -1), LANES).astype(BF16)
    wv_t = wkv[:, :, B_NOPE:].reshape(KV_LORA, -1).T.astype(BF16)

    assert N_META + S >= META_PROJ_ROWS
    tabs = _rope_tables(N_META + S, d_q ** -0.5 * LOG2_E)
    tabs_x = (tabs[0][N_META:], tabs[1][N_META:], tabs[2][:, N_META:], tabs[3][:, N_META:])
    tabs_m = (tabs[0][:META_PROJ_ROWS], tabs[1][:META_PROJ_ROWS],
              tabs[2][:, :META_PROJ_ROWS], tabs[3][:, :META_PROJ_ROWS])

    row2 = lambda v: v.reshape(1, -1).astype(F32)
    meta_pad = jnp.pad(meta_tokens.astype(x.dtype), ((0, META_PROJ_ROWS - N_META), (0, 0)))[None]

    params = pltpu.CompilerParams(dimension_semantics=("arbitrary", "arbitrary"),
                                  vmem_limit_bytes=VMEM_LIMIT)
    sds = jax.ShapeDtypeStruct

    def project(rows_in, tabs, nb, nblk, blk, nsub, n_valid):
        rows = nsub * blk
        row_blk = lambda n: pl.BlockSpec((1, rows, n), lambda b, i: (b, i, 0))
        t_blk = lambda r, n: pl.BlockSpec((1, nsub, r, n), lambda b, i: (b, i, 0, 0))
        return pl.pallas_call(
            functools.partial(_proj_kernel, n_valid=n_valid),
            grid=(nb, nblk // nsub),
            in_specs=[row_blk(D), _full((1, D)), _full((1, D)), _full((D, NN_END)),
                      _full((NT_END, D)), _full((D, IDX_DIM)), _full((SEL_END, D)),
                      _full((HEADS * LANES, Q_LORA)), _full((KV_LORA, HEADS * LANES)),
                      _full((B_WIDTH, KV_LORA)), _full((1, Q_LORA)), _full((1, KV_LORA)),
                      pl.BlockSpec((rows, LANES), lambda b, i: (i, 0)),
                      pl.BlockSpec((rows, LANES), lambda b, i: (i, 0)),
                      pl.BlockSpec((LANES, rows), lambda b, i: (0, i)),
                      pl.BlockSpec((LANES, rows), lambda b, i: (0, i))],
            out_specs=[row_blk(A_WIDTH), row_blk(IDX_DIM), row_blk(A_WIDTH + B_WIDTH),
                       row_blk(HEADS * LANES), t_blk(A_WIDTH, blk),
                       t_blk(IDX_DIM, HEADS * blk), t_blk(HEADS, blk), t_blk(VT_WIDTH, blk),
                       t_blk(HEADS * LANES, blk), t_blk(B_WIDTH, blk)],
            out_shape=[sds((nb, nblk * blk, A_WIDTH), BF16), sds((nb, nblk * blk, IDX_DIM), F32),
                       sds((nb, nblk * blk, A_WIDTH + B_WIDTH), F32),
                       sds((nb, nblk * blk, HEADS * LANES), BF16),
                       sds((nb, nblk, A_WIDTH, blk), BF16),
                       sds((nb, nblk, IDX_DIM, HEADS * blk), F32),
                       sds((nb, nblk, HEADS, blk), F32), sds((nb, nblk, VT_WIDTH, blk), BF16),
                       sds((nb, nblk, HEADS * LANES, blk), BF16),
                       sds((nb, nblk, B_WIDTH, blk), BF16)],
            compiler_params=params, name="proj",
        )(rows_in, row2(ln_emb_g), row2(ln_emb_b), wnn, wnt, wki, wsel, wuq_t, wk, wv_t,
          row2(q_norm_g[0]), row2(kv_norm_g[0]), *tabs)

    ka, ki, gates, kb, qat, qit, wit, vat, qbt, vbt = project(x, tabs_x, B, nq, TQ, PROJ_BLOCKS, S)
    ka_m, _, _, kb_m, _, _, _, vat_m, _, vbt_m = project(meta_pad, tabs_m, 1, 1, META_PROJ_ROWS, 1,
                                                         N_META)
    ka_m, kb_m = ka_m[0, :META_ROWS], kb_m[0, :META_ROWS]
    vat_m, vbt_m = vat_m[0, 0, :, :META_ROWS], vbt_m[0, 0, :, :META_ROWS]

    batch_rows = lambda n: pl.BlockSpec((1, S, n), lambda b, i: (b, 0, 0))
    batch_t = lambda r: pl.BlockSpec((1, nq, r, TQ), lambda b, i: (b, 0, 0, 0))
    blk = lambda r, n: pl.BlockSpec((1, 1, r, n), lambda b, i: (b, i, 0, 0))
    out_blk = pl.BlockSpec((1, TQ, A_WIDTH), lambda b, i: (b, i, 0))
    oa = pl.pallas_call(
        functools.partial(_dsa_kernel, topk=topk, seq=S),
        grid=(B, nq),
        in_specs=[pl.BlockSpec(memory_space=pltpu.SMEM), batch_rows(A_WIDTH), batch_rows(IDX_DIM),
                  batch_t(VT_WIDTH), blk(A_WIDTH, TQ), blk(IDX_DIM, HEADS * TQ), blk(HEADS, TQ),
                  _full((META_ROWS, A_WIDTH)), _full((VT_WIDTH, META_ROWS))],
        out_specs=out_blk,
        out_shape=sds((B, S, A_WIDTH), F32),
        scratch_shapes=[pltpu.VMEM((S, TQ), I32),
                        pltpu.VMEM((HEADS, 2, TQ, TQ), F32),
                        pltpu.VMEM((HEADS, META_ROWS, TQ), F32),
                        pltpu.VMEM((VT_WIDTH, TQ), F32),
                        pltpu.VMEM((HEADS, LANES, TQ), BF16),
                        pltpu.VMEM((HEADS, TQ, TQ), F32),
                        pltpu.VMEM((TQ, TQ), F32),
                        pltpu.VMEM((S // 2, TQ), I32),
                        pltpu.VMEM((S // 2, TQ), I32),
                        pltpu.VMEM((S // 2, TQ), I32)],
        compiler_params=params, name="dsa",
    )(rel_bias.astype(F32), ka, ki, vat, qat, qit, wit, ka_m, vat_m)

    row_blk = lambda n: pl.BlockSpec((1, TQ, n), lambda b, i: (b, i, 0))
    return pl.pallas_call(
        _mla_out_kernel,
        grid=(B, nq),
        in_specs=[batch_rows(HEADS * LANES), batch_t(B_WIDTH), blk(HEADS * LANES, TQ),
                  _full((META_ROWS, HEADS * LANES)), _full((B_WIDTH, META_ROWS)),
                  row_blk(D), _full((1, D)), _full((1, D)), row_blk(A_WIDTH),
                  row_blk(A_WIDTH + B_WIDTH), _full((A_WIDTH + B_WIDTH, D)), _full((1, D)),
                  _full((1, D))],
        out_specs=row_blk(D),
        out_shape=sds((B, S, D), x.dtype),
        scratch_shapes=[pltpu.VMEM((HEADS, TQ, TQ), F32),
                        pltpu.VMEM((B_WIDTH, TQ), F32)],
        compiler_params=params, name="mla_out",
    )(kb, vbt, qbt, kb_m, vbt_m, x, row2(ln_emb_g), row2(ln_emb_b), oa, gates,
      w_out[0].astype(BF16), row2(ln_post_g[0]), row2(ln_post_b[0]))
```

```python
import functools
import math

import jax
import jax.numpy as jnp
from jax import lax
from jax.experimental import pallas as pl
from jax.experimental.pallas import tpu as pltpu

F32 = jnp.float32
BF16 = jnp.bfloat16
I32 = jnp.int32
I16 = jnp.int16

N_META = 16
HEADS = 8
A_HEAD_DIM = 64
A_WIDTH = HEADS * A_HEAD_DIM
IDX_DIM = 64
TOPK_MAX = 256
B_NOPE = 64
B_ROPE = 32
B_V = 64
B_WIDTH = HEADS * B_V
Q_LORA = 256
KV_LORA = 128
ROPE_THETA = 10000.0
REL_BUCKETS = 32
REL_MAX_DIST = 128
LN_EPS = 1e-5
RMS_EPS = 1e-6
DEPTH = 1
ALPHA = (2.0 * DEPTH) ** 0.25

LANES = 128
TQ = 256
V_EXT = 80
L_ROW = 64
VT_WIDTH = HEADS * V_EXT
HALF = TQ // 2
SCAN = 2
PROJ_BLOCKS = 2
META_ROWS = 16
META_PROJ_ROWS = LANES
MASKED = -1e30
LOG2_E = math.log2(math.e)
INT_MIN = -2147483648
KEY_NEG_INF = -0x7F800000
LOWEST_I16_PAIR = -0x7FFF8000
VMEM_LIMIT = 56 * 1024 * 1024

_SPLITS = (A_WIDTH, A_WIDTH, A_WIDTH, A_WIDTH, HEADS * IDX_DIM, IDX_DIM, HEADS,
           Q_LORA, KV_LORA, B_ROPE, B_WIDTH)
_OFF = [0]
for _s in _SPLITS:
    _OFF.append(_OFF[-1] + _s)
(O_QA, O_KA, O_VA, O_GA, O_QI, O_KI, O_WI, O_CQ, O_CKV, O_KPE, O_GB, O_END) = _OFF

NN_KA = 0
NN_G = NN_KA + A_WIDTH
NN_CQ = NN_G + A_WIDTH + B_WIDTH
NN_CKV = NN_CQ + Q_LORA
NN_PE = NN_CKV + KV_LORA
NN_PER = NN_PE + LANES
NN_END = NN_PER + LANES
NT_QA = 0
NT_VA = NT_QA + A_WIDTH
NT_END = NT_VA + VT_WIDTH
SEL_WI = HEADS * IDX_DIM
SEL_END = SEL_WI + 16

_NT_DIMS = (((1,), (1,)), ((), ()))


def _dot_nt(a, b):
    return lax.dot_general(a, b, _NT_DIMS, preferred_element_type=F32)


def _layer_norm(x, g, b):
    mu = jnp.mean(x, axis=-1, keepdims=True)
    xc = x - mu
    var = jnp.mean(xc * xc, axis=-1, keepdims=True)
    return xc * lax.rsqrt(var + LN_EPS) * g + b


def _rms_norm(x, g):
    return x * lax.rsqrt(jnp.mean(x * x, axis=-1, keepdims=True) + RMS_EPS) * g


def _proj_kernel(x_ref, lng_ref, lnb_ref, wnn_ref, wnt_ref, wki_ref, wsel_ref, wuq_ref, wk_ref,
                 wv_ref, qg_ref, kvg_ref, cos_ref, sin_ref, ct_ref, st_ref,
                 ka_ref, ki_ref, g_ref, kb_ref, qat_ref, qit_ref, wit_ref, vat_ref, qbt_ref,
                 vbt_ref, *, n_valid):
    rows = x_ref.shape[1]
    nsub = qat_ref.shape[1]
    blk = rows // nsub
    y = _layer_norm(x_ref[0], lng_ref[...], lnb_ref[...])
    if n_valid < rows:
        y = jnp.where(lax.broadcasted_iota(I32, (rows, 1), 0) < n_valid, y, 0.0)
    hb = y.astype(BF16)

    def nn(lo, hi):
        return jnp.dot(hb, wnn_ref[:, lo:hi], preferred_element_type=F32)

    def nt(lo, hi):
        return _dot_nt(wnt_ref[lo:hi, :], hb)

    ka_ref[0] = nn(NN_KA, NN_G).astype(BF16)
    g_ref[0] = nn(NN_G, NN_CQ)

    qat = (nt(NT_QA, NT_VA) * (A_HEAD_DIM ** -0.5 * LOG2_E)).astype(BF16)
    ones_row = jnp.where(lax.broadcasted_iota(I32, (VT_WIDTH, 1), 0) % V_EXT == L_ROW, 1.0, 0.0)
    vat = (nt(NT_VA, NT_END) + ones_row).astype(BF16)

    ki_ref[0] = jnp.dot(y, wki_ref[...], preferred_element_type=F32)
    qit = _dot_nt(wsel_ref[0:SEL_WI, :], y)
    wit = _dot_nt(wsel_ref[SEL_WI:SEL_END, :], y) * (HEADS ** -0.5)
    wit = wit * (IDX_DIM ** -0.5)
    for s in range(nsub):
        cols = slice(s * blk, (s + 1) * blk)
        qat_ref[0, s] = qat[:, cols]
        vat_ref[0, s] = vat[:, cols]
        wit_ref[0, s] = wit[0:HEADS, cols]
        for h in range(HEADS):
            qit_ref[0, s, :, h * blk:(h + 1) * blk] = qit[h * IDX_DIM:(h + 1) * IDX_DIM, cols]

    cqn = _rms_norm(nn(NN_CQ, NN_CKV), qg_ref[...]).astype(BF16)
    ckvn = _rms_norm(nn(NN_CKV, NN_PE), kvg_ref[...]).astype(BF16)
    qbt = _dot_nt(wuq_ref[...], cqn)
    ct = ct_ref[...]
    st = st_ref[...]
    half = B_ROPE // 2
    for h in range(HEADS):
        q = qbt[h * LANES:(h + 1) * LANES]
        x1, x2 = q[B_NOPE:B_NOPE + half], q[B_NOPE + half:B_NOPE + B_ROPE]
        rot = jnp.concatenate([jnp.zeros((B_NOPE, rows), F32), -x2, x1,
                               jnp.zeros((LANES - B_NOPE - B_ROPE, rows), F32)], axis=0)
        qh = (q * ct + rot * st).astype(BF16)
        for s in range(nsub):
            qbt_ref[0, s, h * LANES:(h + 1) * LANES, :] = qh[:, s * blk:(s + 1) * blk]
    kn = jnp.dot(ckvn, wk_ref[...], preferred_element_type=F32)
    kpe = nn(NN_PE, NN_PER) * cos_ref[...] + nn(NN_PER, NN_END) * sin_ref[...]
    for h in range(HEADS):
        cols = slice(h * LANES, (h + 1) * LANES)
        kb_ref[0, :, cols] = (kn[:, cols] + kpe).astype(BF16)
    vbt = _dot_nt(wv_ref[...], ckvn).astype(BF16)
    for s in range(nsub):
        vbt_ref[0, s] = vbt[:, s * blk:(s + 1) * blk]


_MAX_EXACT = REL_BUCKETS // 2
_BUCKET_STARTS = tuple(math.ceil(_MAX_EXACT * (REL_MAX_DIST / _MAX_EXACT) ** (k / (REL_BUCKETS - _MAX_EXACT)))
                       for k in range(1, REL_BUCKETS - _MAX_EXACT))


def _t5_bucket(d):
    d = jnp.maximum(d, 0)
    large = jnp.full(d.shape, _MAX_EXACT, I32)
    for start in _BUCKET_STARTS:
        large = large + jnp.where(d >= start, 1, 0)
    return jnp.where(d < _MAX_EXACT, d, large)


def _dsa_kernel(relb_ref, ka_ref, ki_ref, vat_ref, qat_ref, qit_ref, wit_ref, kam_ref, vatm_ref,
                oa_ref, keys_s, bias_s, metab_s, acc_s, qpad_s, s_scr, madd_s, hi_s, lo_s, lo2_s,
                *, topk, seq):
    b = pl.program_id(0)
    i = pl.program_id(1)
    nch = i + 1
    row = lax.broadcasted_iota(I32, (TQ, TQ), 0)
    pos_q = i * TQ + lax.broadcasted_iota(I32, (TQ, TQ), 1)

    @pl.when((b == 0) & (i == 0))
    def _():
        sub = lax.broadcasted_iota(I32, (8, TQ), 0)
        lane = lax.broadcasted_iota(I32, (8, TQ), 1)

        def fill(dst, n_rows, offset):
            def step(r8, carry):
                r0 = pl.multiple_of(r8 * 8, 8)
                bucket = _t5_bucket(offset + lane - (r0 + sub))
                for h in range(HEADS):
                    t = jnp.zeros((8, TQ), F32)
                    for bk in range(REL_BUCKETS):
                        t = jnp.where(bucket == bk, relb_ref[bk, h], t)
                    dst[h, pl.ds(r0, 8), :] = (t - relb_ref[REL_BUCKETS - 1, h]) * LOG2_E
                return carry
            lax.fori_loop(0, n_rows // 8, step, 0)

        for dl in range(2):
            fill(bias_s.at[:, dl], TQ, dl * TQ)
        fill(metab_s, META_ROWS, N_META)

    def idx_body(c, carry, diagonal):
        k0 = pl.multiple_of(c * TQ, TQ)
        rel = jnp.dot(ki_ref[0, pl.ds(k0, TQ), :], qit_ref[0, 0],
                      preferred_element_type=F32)
        s = jnp.zeros((TQ, TQ), F32)
        for h in range(HEADS):
            s = s + jnp.maximum(rel[:, h * TQ:(h + 1) * TQ], 0.0) * wit_ref[0, 0, h:h + 1, :]
        if diagonal:
            s = jnp.where(k0 + row <= pos_q, s, -jnp.inf)
        bits = lax.bitcast_convert_type(s, I32)
        key = jnp.where(bits < 0, INT_MIN - bits, bits)
        keys_s[pl.ds(k0, TQ), :] = key
        h0 = pl.multiple_of(c * HALF, HALF)
        ka_, kb_ = key[:HALF], key[HALF:]
        hi_s[pl.ds(h0, HALF), :] = ((ka_ >> 16) & 0xFFFF) | (kb_ & -0x10000)
        lo_s[pl.ds(h0, HALF), :] = ((ka_ & 0xFFFF) ^ 0x8000) | ((kb_ << 16) ^ INT_MIN)
        return carry

    def idx_four(j, carry):
        for u in range(4):
            idx_body(4 * j + u, carry, diagonal=False)
        return carry

    def idx_two(j, carry):
        idx_body(2 * j, carry, diagonal=False)
        return idx_body(2 * j + 1, carry, diagonal=False)

    lax.fori_loop(0, (nch - 1) // 4, idx_four, 0)
    lax.fori_loop((nch - 1) // 4 * 2, (nch - 1) // 2, idx_two, 0)
    lax.fori_loop((nch - 1) // 2 * 2, nch - 1, functools.partial(idx_body, diagonal=False), 0)
    idx_body(nch - 1, 0, diagonal=True)

    nscan = (nch + SCAN - 1) // SCAN

    def fill_lowest(c, carry):
        h0 = pl.multiple_of(c * HALF, HALF)
        lowest = jnp.full((HALF, TQ), LOWEST_I16_PAIR, I32)
        hi_s[pl.ds(h0, HALF), :] = lowest
        lo_s[pl.ds(h0, HALF), :] = lowest
        return carry

    lax.fori_loop(nch, nscan * SCAN, fill_lowest, 0)

    def count(pred):
        def body(c, acc):
            k0 = pl.multiple_of(c * TQ, TQ)
            hit = pred(keys_s[pl.ds(k0, TQ), :], k0 + row)
            return acc + jnp.sum(jnp.where(hit, 1, 0).reshape(TQ // 8, 8, TQ), axis=0)
        acc = lax.fori_loop(0, nch, body, jnp.zeros((8, TQ), I32))
        return jnp.sum(acc, axis=0, keepdims=True)

    def splat16(pat):
        word = pat | (pat << 16)
        return pltpu.bitcast(jnp.broadcast_to(word, (8, TQ)), I16)

    def count16(scr, pat):
        thr_p = splat16(pat)
        one, zero = jnp.int16(1), jnp.int16(0)

        def body(cs, acc):
            h0 = pl.multiple_of(cs * (SCAN * HALF), SCAN * HALF)
            k = pltpu.bitcast(scr[pl.ds(h0, SCAN * HALF), :], I16)
            parts = []
            for j in range(SCAN * TQ // 16):
                kj = k[16 * j:16 * (j + 1)]
                parts.append(jnp.where(kj >= thr_p, one, zero))
            while len(parts) > 1:
                parts = [x + y for x, y in zip(parts[::2], parts[1::2])] + parts[len(parts) & ~1:]
            return acc + parts[0].astype(I32)
        acc = lax.fori_loop(0, nscan, body, jnp.zeros((16, TQ), I32))
        return jnp.sum(acc, axis=0, keepdims=True)

    def hi_bit(it, carry):
        t_hi, cnt_t = carry
        cand = t_hi | jnp.left_shift(jnp.int32(1), 15 - it)
        cnt = count16(hi_s, cand ^ 0x8000)
        ok = cnt >= topk
        return jnp.where(ok, cand, t_hi), jnp.where(ok, cnt, cnt_t)

    zero_q = jnp.zeros((1, TQ), I32)
    t_hi, cnt_t = lax.fori_loop(0, 16, hi_bit, (zero_q, jnp.full((1, TQ), seq, I32)))
    thr_hi_p = splat16(t_hi ^ 0x8000)

    def lo_prep(c2, acc):
        h0 = pl.multiple_of(c2 * TQ, TQ)
        hi = pltpu.bitcast(hi_s[pl.ds(h0, TQ), :], I16)
        lo = pltpu.bitcast(lo_s[pl.ds(h0, TQ), :], I16)
        parts, above = [], []
        for j in range(2 * TQ // 16):
            hj = hi[16 * j:16 * (j + 1)]
            parts.append(jnp.where(hj == thr_hi_p, lo[16 * j:16 * (j + 1)], jnp.int16(-0x8000)))
            above.append(jnp.where(hj > thr_hi_p, jnp.int16(1), jnp.int16(0)))
        lo2_s[pl.ds(h0, TQ), :] = pltpu.bitcast(jnp.concatenate(parts, axis=0), I32)
        while len(above) > 1:
            above = [x + y for x, y in zip(above[::2], above[1::2])] + above[len(above) & ~1:]
        return acc + above[0].astype(I32)

    acc_gt = lax.fori_loop(0, nscan * (SCAN // 2), lo_prep, jnp.zeros((16, TQ), I32))
    cnt_gt = jnp.sum(acc_gt, axis=0, keepdims=True)

    def lo_bit(it, carry):
        t_lo, cnt_t = carry
        cand = t_lo | jnp.left_shift(jnp.int32(1), 15 - it)
        cnt = cnt_gt + count16(lo2_s, cand ^ 0x8000)
        ok = cnt >= topk
        return jnp.where(ok, cand, t_lo), jnp.where(ok, cnt, cnt_t)

    t_lo, cnt_t = lax.fori_loop(0, 16, lo_bit, (zero_q, cnt_t))
    thr = ((t_hi << 16) | t_lo) ^ INT_MIN

    tie_cut = (cnt_t > topk) & (thr > KEY_NEG_INF)
    any_cut = jnp.max(jnp.where(tie_cut, 1, 0))

    @pl.when(any_cut > 0)
    def _():
        need = topk - count(lambda k, p: k > thr)

        def jbit(it, j):
            cand = j | jnp.left_shift(jnp.int32(1), 12 - it)
            cnt = count(lambda k, p: (k == thr) & (p < cand))
            return jnp.where(cnt < need, cand, j)

        j = lax.fori_loop(0, 13, jbit, jnp.zeros((1, TQ), I32))
        j_last = jnp.where(tie_cut, j, seq)

        def demote(c, carry):
            k0 = pl.multiple_of(c * TQ, TQ)
            key = keys_s[pl.ds(k0, TQ), :]
            keys_s[pl.ds(k0, TQ), :] = jnp.where((key == thr) & (k0 + row > j_last), key - 1, key)
            return carry

        lax.fori_loop(0, nch, demote, 0)

    thr_sel = jnp.maximum(thr, KEY_NEG_INF + 1)

    for h in range(HEADS):
        qh = qat_ref[0, 0, h * A_HEAD_DIM:(h + 1) * A_HEAD_DIM, :]
        zero_half = jnp.zeros_like(qh)
        qpad_s[h] = jnp.concatenate([zero_half, qh] if h % 2 else [qh, zero_half], axis=0)

    def set_mask(c):
        k0 = pl.multiple_of(c * TQ, TQ)
        madd_s[...] = jnp.where(keys_s[pl.ds(k0, TQ), :] >= thr_sel, 0.0, MASKED)

    def scores(c, h, near):
        k0 = pl.multiple_of(c * TQ, TQ)
        pair = slice((h // 2) * LANES, (h // 2 + 1) * LANES)
        s = jnp.dot(ka_ref[0, pl.ds(k0, TQ), pair], qpad_s[h],
                    preferred_element_type=F32) + madd_s[...]
        if near:
            dl = i - c
            s = s + jnp.where(dl == 0, bias_s[h, 0], jnp.where(dl == 1, bias_s[h, 1], 0.0))
        s_scr[h] = s
        return jnp.max(s, axis=0, keepdims=True)

    def att_body(c, carry, lookahead):
        ms, cms = carry
        if lookahead is not None:
            set_mask(c + 1)
        new_ms, new_cms = [], []
        for h in range(HEADS):
            m_new = jnp.maximum(ms[h], cms[h])
            alpha = jnp.exp2(ms[h] - m_new)
            p = jnp.exp2((s_scr[h] - m_new).astype(BF16))
            new_ms.append(m_new)
            rows = slice(h * V_EXT, (h + 1) * V_EXT)
            pv = jnp.dot(vat_ref[0, c, rows, :], p, preferred_element_type=F32)
            acc_s[rows, :] = alpha * acc_s[rows, :] + pv
            if lookahead is not None:
                new_cms.append(scores(c + 1, h, near=lookahead))
        return tuple(new_ms), tuple(new_cms)

    meta_ok = lax.broadcasted_iota(I32, (META_ROWS, TQ), 0) < N_META
    ms0 = []
    for h in range(HEADS):
        pair = slice((h // 2) * LANES, (h // 2 + 1) * LANES)
        s = jnp.dot(kam_ref[:, pair], qpad_s[h], preferred_element_type=F32)
        s = jnp.where(meta_ok, s + jnp.where(i == 0, metab_s[h], 0.0), MASKED)
        m0 = jnp.max(s, axis=0, keepdims=True)
        p = jnp.exp2((s - m0).astype(BF16))
        ms0.append(m0)
        rows = slice(h * V_EXT, (h + 1) * V_EXT)
        acc_s[rows, :] = jnp.dot(vatm_ref[rows, :], p, preferred_element_type=F32)

    set_mask(0)
    init = (tuple(ms0), tuple(scores(0, h, near=True) for h in range(HEADS)))
    n_far = jnp.maximum(nch - 3, 0)

    def four_steps(j, carry):
        for u in range(4):
            carry = att_body(4 * j + u, carry, lookahead=False)
        return carry

    def two_steps(j, carry):
        return att_body(2 * j + 1, att_body(2 * j, carry, lookahead=False), lookahead=False)

    carry = lax.fori_loop(0, n_far // 4, four_steps, init)
    carry = lax.fori_loop(n_far // 4 * 2, n_far // 2, two_steps, carry)
    carry = lax.fori_loop(n_far // 2 * 2, n_far, functools.partial(att_body, lookahead=False), carry)
    carry = lax.fori_loop(n_far, nch - 1, functools.partial(att_body, lookahead=True), carry)
    att_body(nch - 1, carry, lookahead=None)

    outs = []
    for h in range(HEADS):
        num = acc_s[h * V_EXT:h * V_EXT + A_HEAD_DIM, :]
        den = acc_s[h * V_EXT + L_ROW:h * V_EXT + L_ROW + 1, :]
        outs.append(num * (1.0 / den))
    oa_ref[0] = jnp.concatenate(outs, axis=0).T


def _silu(x):
    return x * (1.0 / (1.0 + jnp.exp(-x)))


def _mla_out_kernel(kb_ref, vbt_ref, qbt_ref, kbm_ref, vbtm_ref, x_ref, lng_ref, lnb_ref, oa_ref,
                    g_ref, wout_ref, pg_ref, pb_ref, out_ref, s_scr, acc_s):
    i = pl.program_id(1)
    nch = i + 1
    row = lax.broadcasted_iota(I32, (TQ, TQ), 0)
    pos_q = i * TQ + lax.broadcasted_iota(I32, (TQ, TQ), 1)

    def scores(c, h, diagonal):
        k0 = pl.multiple_of(c * TQ, TQ)
        grp = slice(h * LANES, (h + 1) * LANES)
        s = jnp.dot(kb_ref[0, pl.ds(k0, TQ), grp], qbt_ref[0, 0, grp, :],
                    preferred_element_type=F32)
        if diagonal:
            s = jnp.where(k0 + row <= pos_q, s, MASKED)
        s_scr[h] = s
        return jnp.max(s, axis=0, keepdims=True)

    def body(c, carry, lookahead):
        ms, ls, cms = carry
        new_ms, new_ls, new_cms = [], [], []
        for h in range(HEADS):
            m_new = jnp.maximum(ms[h], cms[h])
            alpha = jnp.exp2(ms[h] - m_new)
            p = jnp.exp2(s_scr[h] - m_new)
            new_ls.append(alpha * ls[h] + jnp.sum(p, axis=0, keepdims=True))
            new_ms.append(m_new)
            rows = slice(h * B_V, (h + 1) * B_V)
            pv = jnp.dot(vbt_ref[0, c, rows, :], p.astype(BF16), preferred_element_type=F32)
            acc_s[rows, :] = alpha * acc_s[rows, :] + pv
            if lookahead is not None:
                new_cms.append(scores(c + 1, h, diagonal=lookahead))
        return tuple(new_ms), tuple(new_ls), tuple(new_cms)

    meta_ok = lax.broadcasted_iota(I32, (META_ROWS, TQ), 0) < N_META
    ms0, ls0 = [], []
    for h in range(HEADS):
        grp = slice(h * LANES, (h + 1) * LANES)
        s = jnp.dot(kbm_ref[:, grp], qbt_ref[0, 0, grp, :], preferred_element_type=F32)
        s = jnp.where(meta_ok, s, MASKED)
        m0 = jnp.max(s, axis=0, keepdims=True)
        p = jnp.exp2(s - m0)
        ms0.append(m0)
        ls0.append(jnp.sum(p, axis=0, keepdims=True))
        rows = slice(h * B_V, (h + 1) * B_V)
        acc_s[rows, :] = jnp.dot(vbtm_ref[rows, :], p.astype(BF16), preferred_element_type=F32)

    cms0 = tuple(scores(0, h, diagonal=True) for h in range(HEADS))
    init = (tuple(ms0), tuple(ls0), cms0)
    n_full = jnp.maximum(nch - 2, 0)

    def four_steps(j, carry):
        for u in range(4):
            carry = body(4 * j + u, carry, lookahead=False)
        return carry

    def two_steps(j, carry):
        return body(2 * j + 1, body(2 * j, carry, lookahead=False), lookahead=False)

    carry = lax.fori_loop(0, n_full // 4, four_steps, init)
    carry = lax.fori_loop(n_full // 4 * 2, n_full // 2, two_steps, carry)
    carry = lax.fori_loop(n_full // 2 * 2, n_full, functools.partial(body, lookahead=False), carry)
    carry = lax.fori_loop(n_full, nch - 1, functools.partial(body, lookahead=True), carry)
    ms, ls, _ = body(nch - 1, carry, lookahead=None)

    outs = []
    for h in range(HEADS):
        rows = slice(h * B_V, (h + 1) * B_V)
        outs.append(acc_s[rows, :] * (1.0 / ls[h]))
    ob = jnp.concatenate(outs, axis=0).T

    h = _layer_norm(x_ref[0], lng_ref[...], lnb_ref[...])
    g = g_ref[0]
    mixed = jnp.concatenate([oa_ref[0] * _silu(g[:, :A_WIDTH]), ob * _silu(g[:, A_WIDTH:])], axis=-1)
    out = jnp.dot(mixed.astype(BF16), wout_ref[...], preferred_element_type=F32)
    out_ref[0] = _layer_norm(ALPHA * h + out, pg_ref[...], pb_ref[...])


def _rot_cols(w):
    half = w.shape[-1] // 2
    return jnp.concatenate([-w[:, half:], w[:, :half]], axis=-1)


def _full(shape):
    return pl.BlockSpec(shape, lambda *_: (0,) * len(shape))


def _rope_tables(n, scale):
    d_q = B_NOPE + B_ROPE
    inv_freq = ROPE_THETA ** (-jnp.arange(0, B_ROPE, 2, dtype=F32) / B_ROPE)
    ang = jnp.arange(n, dtype=F32)[:, None] * inv_freq[None, :]
    pad = ((0, 0), (B_NOPE, LANES - d_q))
    cos128 = jnp.pad(jnp.tile(jnp.cos(ang), (1, 2)), pad)
    sin128 = jnp.pad(jnp.tile(jnp.sin(ang), (1, 2)), pad)
    nope = (lax.broadcasted_iota(I32, (1, LANES), 1) < B_NOPE).astype(F32)
    return cos128, sin128, ((cos128 + nope) * scale).T, (sin128 * scale).T


def kernel(x, meta_tokens, ln_emb_g, ln_emb_b, w_in, w_uq, q_norm_g, w_ukv, kv_norm_g, rel_bias,
           w_out, ln_post_g, ln_post_b):
    B, S, D = x.shape
    assert w_in.shape[0] == DEPTH == 1
    assert S % TQ == 0 and S < (1 << 13) and meta_tokens.shape[0] == N_META <= META_ROWS
    nq = S // TQ
    assert nq % SCAN == 0 and SCAN % 2 == 0
    topk = min(TOPK_MAX, S // 4) - N_META
    assert 0 < topk <= TQ

    w = w_in[0]
    d_q = B_NOPE + B_ROPE
    d_kv = B_NOPE + B_V

    def per_head(m, width):
        m = m.reshape(m.shape[0], HEADS, -1)
        return jnp.pad(m, ((0, 0), (0, 0), (0, width - m.shape[2]))).reshape(m.shape[0], -1)

    w_kpe = w[:, O_KPE:O_GB]
    pe_lanes = ((0, 0), (B_NOPE, LANES - d_q))
    wnn = jnp.concatenate([w[:, O_KA:O_VA], w[:, O_GA:O_QI], w[:, O_GB:O_END], w[:, O_CQ:O_KPE],
                           jnp.pad(w_kpe, pe_lanes), jnp.pad(_rot_cols(w_kpe), pe_lanes)],
                          axis=1).astype(BF16)
    assert wnn.shape[1] == NN_END
    wnt = jnp.concatenate([w[:, O_QA:O_KA], per_head(w[:, O_VA:O_GA], V_EXT)], axis=1).T.astype(BF16)
    assert wnt.shape[0] == NT_END
    wki = w[:, O_KI:O_WI].astype(F32)
    wsel = jnp.concatenate([w[:, O_QI:O_KI], jnp.pad(w[:, O_WI:O_CQ], ((0, 0), (0, 8)))],
                           axis=1).T.astype(F32)
    assert wsel.shape[0] == SEL_END
    wuq_t = per_head(w_uq[0], LANES).T.astype(BF16)
    wkv = w_ukv[0].reshape(KV_LORA, HEADS, d_kv)
    wk = per_head(wkv[:, :, :B_NOPE].reshape(KV_LORA, -1), LANES).astype(BF16)
    wv_t = wkv[:, :, B_NOPE:].reshape(KV_LORA, -1).T.astype(BF16)

    assert N_META + S >= META_PROJ_ROWS
    tabs = _rope_tables(N_META + S, d_q ** -0.5 * LOG2_E)
    tabs_x = (tabs[0][N_META:], tabs[1][N_META:], tabs[2][:, N_META:], tabs[3][:, N_META:])
    tabs_m = (tabs[0][:META_PROJ_ROWS], tabs[1][:META_PROJ_ROWS],
              tabs[2][:, :META_PROJ_ROWS], tabs[3][:, :META_PROJ_ROWS])

    row2 = lambda v: v.reshape(1, -1).astype(F32)
    meta_pad = jnp.pad(meta_tokens.astype(x.dtype), ((0, META_PROJ_ROWS - N_META), (0, 0)))[None]

    params = pltpu.CompilerParams(dimension_semantics=("arbitrary", "arbitrary"),
                                  vmem_limit_bytes=VMEM_LIMIT)
    sds = jax.ShapeDtypeStruct

    def project(rows_in, tabs, nb, nblk, blk, nsub, n_valid):
        rows = nsub * blk
        row_blk = lambda n: pl.BlockSpec((1, rows, n), lambda b, i: (b, i, 0))
        t_blk = lambda r, n: pl.BlockSpec((1, nsub, r, n), lambda b, i: (b, i, 0, 0))
        return pl.pallas_call(
            functools.partial(_proj_kernel, n_valid=n_valid),
            grid=(nb, nblk // nsub),
            in_specs=[row_blk(D), _full((1, D)), _full((1, D)), _full((D, NN_END)),
                      _full((NT_END, D)), _full((D, IDX_DIM)), _full((SEL_END, D)),
                      _full((HEADS * LANES, Q_LORA)), _full((KV_LORA, HEADS * LANES)),
                      _full((B_WIDTH, KV_LORA)), _full((1, Q_LORA)), _full((1, KV_LORA)),
                      pl.BlockSpec((rows, LANES), lambda b, i: (i, 0)),
                      pl.BlockSpec((rows, LANES), lambda b, i: (i, 0)),
                      pl.BlockSpec((LANES, rows), lambda b, i: (0, i)),
                      pl.BlockSpec((LANES, rows), lambda b, i: (0, i))],
            out_specs=[row_blk(A_WIDTH), row_blk(IDX_DIM), row_blk(A_WIDTH + B_WIDTH),
                       row_blk(HEADS * LANES), t_blk(A_WIDTH, blk),
                       t_blk(IDX_DIM, HEADS * blk), t_blk(HEADS, blk), t_blk(VT_WIDTH, blk),
                       t_blk(HEADS * LANES, blk), t_blk(B_WIDTH, blk)],
            out_shape=[sds((nb, nblk * blk, A_WIDTH), BF16), sds((nb, nblk * blk, IDX_DIM), F32),
                       sds((nb, nblk * blk, A_WIDTH + B_WIDTH), F32),
                       sds((nb, nblk * blk, HEADS * LANES), BF16),
                       sds((nb, nblk, A_WIDTH, blk), BF16),
                       sds((nb, nblk, IDX_DIM, HEADS * blk), F32),
                       sds((nb, nblk, HEADS, blk), F32), sds((nb, nblk, VT_WIDTH, blk), BF16),
                       sds((nb, nblk, HEADS * LANES, blk), BF16),
                       sds((nb, nblk, B_WIDTH, blk), BF16)],
            compiler_params=params, name="proj",
        )(rows_in, row2(ln_emb_g), row2(ln_emb_b), wnn, wnt, wki, wsel, wuq_t, wk, wv_t,
          row2(q_norm_g[0]), row2(kv_norm_g[0]), *tabs)

    ka, ki, gates, kb, qat, qit, wit, vat, qbt, vbt = project(x, tabs_x, B, nq, TQ, PROJ_BLOCKS, S)
    ka_m, _, _, kb_m, _, _, _, vat_m, _, vbt_m = project(meta_pad, tabs_m, 1, 1, META_PROJ_ROWS, 1,
                                                         N_META)
    ka_m, kb_m = ka_m[0, :META_ROWS], kb_m[0, :META_ROWS]
    vat_m, vbt_m = vat_m[0, 0, :, :META_ROWS], vbt_m[0, 0, :, :META_ROWS]

    batch_rows = lambda n: pl.BlockSpec((1, S, n), lambda b, i: (b, 0, 0))
    batch_t = lambda r: pl.BlockSpec((1, nq, r, TQ), lambda b, i: (b, 0, 0, 0))
    blk = lambda r, n: pl.BlockSpec((1, 1, r, n), lambda b, i: (b, i, 0, 0))
    out_blk = pl.BlockSpec((1, TQ, A_WIDTH), lambda b, i: (b, i, 0))
    oa = pl.pallas_call(
        functools.partial(_dsa_kernel, topk=topk, seq=S),
        grid=(B, nq),
        in_specs=[pl.BlockSpec(memory_space=pltpu.SMEM), batch_rows(A_WIDTH), batch_rows(IDX_DIM),
                  batch_t(VT_WIDTH), blk(A_WIDTH, TQ), blk(IDX_DIM, HEADS * TQ), blk(HEADS, TQ),
                  _full((META_ROWS, A_WIDTH)), _full((VT_WIDTH, META_ROWS))],
        out_specs=out_blk,
        out_shape=sds((B, S, A_WIDTH), F32),
        scratch_shapes=[pltpu.VMEM((S, TQ), I32),
                        pltpu.VMEM((HEADS, 2, TQ, TQ), F32),
                        pltpu.VMEM((HEADS, META_ROWS, TQ), F32),
                        pltpu.VMEM((VT_WIDTH, TQ), F32),
                        pltpu.VMEM((HEADS, LANES, TQ), BF16),
                        pltpu.VMEM((HEADS, TQ, TQ), F32),
                        pltpu.VMEM((TQ, TQ), F32),
                        pltpu.VMEM((S // 2, TQ), I32),
                        pltpu.VMEM((S // 2, TQ), I32),
                        pltpu.VMEM((S // 2, TQ), I32)],
        compiler_params=params, name="dsa",
    )(rel_bias.astype(F32), ka, ki, vat, qat, qit, wit, ka_m, vat_m)

    row_blk = lambda n: pl.BlockSpec((1, TQ, n), lambda b, i: (b, i, 0))
    return pl.pallas_call(
        _mla_out_kernel,
        grid=(B, nq),
        in_specs=[batch_rows(HEADS * LANES), batch_t(B_WIDTH), blk(HEADS * LANES, TQ),
                  _full((META_ROWS, HEADS * LANES)), _full((B_WIDTH, META_ROWS)),
                  row_blk(D), _full((1, D)), _full((1, D)), row_blk(A_WIDTH),
                  row_blk(A_WIDTH + B_WIDTH), _full((A_WIDTH + B_WIDTH, D)), _full((1, D)),
                  _full((1, D))],
        out_specs=row_blk(D),
        out_shape=sds((B, S, D), x.dtype),
        scratch_shapes=[pltpu.VMEM((HEADS, TQ, TQ), F32),
                        pltpu.VMEM((B_WIDTH, TQ), F32)],
        compiler_params=params, name="mla_out",
    )(kb, vbt, qbt, kb_m, vbt_m, x, row2(ln_emb_g), row2(ln_emb_b), oa, gates,
      w_out[0].astype(BF16), row2(ln_post_g[0]), row2(ln_post_b[0]))
```

```python
import functools
import math

import jax
import jax.numpy as jnp
from jax import lax
from jax.experimental import pallas as pl
from jax.experimental.pallas import tpu as pltpu

F32 = jnp.float32
BF16 = jnp.bfloat16
I32 = jnp.int32
I16 = jnp.int16

N_META = 16
HEADS = 8
A_HEAD_DIM = 64
A_WIDTH = HEADS * A_HEAD_DIM
IDX_DIM = 64
TOPK_MAX = 256
B_NOPE = 64
B_ROPE = 32
B_V = 64
B_WIDTH = HEADS * B_V
Q_LORA = 256
KV_LORA = 128
ROPE_THETA = 10000.0
REL_BUCKETS = 32
REL_MAX_DIST = 128
LN_EPS = 1e-5
RMS_EPS = 1e-6
DEPTH = 1
ALPHA = (2.0 * DEPTH) ** 0.25

LANES = 128
TQ = 256
V_EXT = 80
L_ROW = 64
VT_WIDTH = HEADS * V_EXT
HALF = TQ // 2
SCAN = 2
PROJ_BLOCKS = 2
META_ROWS = 16
META_PROJ_ROWS = LANES
MASKED = -1e30
LOG2_E = math.log2(math.e)
INT_MIN = -2147483648
KEY_NEG_INF = -0x7F800000
LOWEST_I16_PAIR = -0x7FFF8000
VMEM_LIMIT = 56 * 1024 * 1024

_SPLITS = (A_WIDTH, A_WIDTH, A_WIDTH, A_WIDTH, HEADS * IDX_DIM, IDX_DIM, HEADS,
           Q_LORA, KV_LORA, B_ROPE, B_WIDTH)
_OFF = [0]
for _s in _SPLITS:
    _OFF.append(_OFF[-1] + _s)
(O_QA, O_KA, O_VA, O_GA, O_QI, O_KI, O_WI, O_CQ, O_CKV, O_KPE, O_GB, O_END) = _OFF

NN_KA = 0
NN_G = NN_KA + A_WIDTH
NN_CQ = NN_G + A_WIDTH + B_WIDTH
NN_CKV = NN_CQ + Q_LORA
NN_PE = NN_CKV + KV_LORA
NN_PER = NN_PE + LANES
NN_END = NN_PER + LANES
NT_QA = 0
NT_VA = NT_QA + A_WIDTH
NT_END = NT_VA + VT_WIDTH
SEL_WI = HEADS * IDX_DIM
SEL_END = SEL_WI + 16

_NT_DIMS = (((1,), (1,)), ((), ()))


def _dot_nt(a, b):
    return lax.dot_general(a, b, _NT_DIMS, preferred_element_type=F32)


def _layer_norm(x, g, b):
    mu = jnp.mean(x, axis=-1, keepdims=True)
    xc = x - mu
    var = jnp.mean(xc * xc, axis=-1, keepdims=True)
    return xc * lax.rsqrt(var + LN_EPS) * g + b


def _rms_norm(x, g):
    return x * lax.rsqrt(jnp.mean(x * x, axis=-1, keepdims=True) + RMS_EPS) * g


def _proj_kernel(x_ref, lng_ref, lnb_ref, wnn_ref, wnt_ref, wki_ref, wsel_ref, wuq_ref, wk_ref,
                 wv_ref, qg_ref, kvg_ref, cos_ref, sin_ref, ct_ref, st_ref,
                 ka_ref, ki_ref, g_ref, kb_ref, qat_ref, qit_ref, wit_ref, vat_ref, qbt_ref,
                 vbt_ref, *, n_valid):
    rows = x_ref.shape[1]
    nsub = qat_ref.shape[1]
    blk = rows // nsub
    y = _layer_norm(x_ref[0], lng_ref[...], lnb_ref[...])
    if n_valid < rows:
        y = jnp.where(lax.broadcasted_iota(I32, (rows, 1), 0) < n_valid, y, 0.0)
    hb = y.astype(BF16)

    def nn(lo, hi):
        return jnp.dot(hb, wnn_ref[:, lo:hi], preferred_element_type=F32)

    def nt(lo, hi):
        return _dot_nt(wnt_ref[lo:hi, :], hb)

    ka_ref[0] = nn(NN_KA, NN_G).astype(BF16)
    g_ref[0] = nn(NN_G, NN_CQ)

    qat = (nt(NT_QA, NT_VA) * (A_HEAD_DIM ** -0.5 * LOG2_E)).astype(BF16)
    ones_row = jnp.where(lax.broadcasted_iota(I32, (VT_WIDTH, 1), 0) % V_EXT == L_ROW, 1.0, 0.0)
    vat = (nt(NT_VA, NT_END) + ones_row).astype(BF16)

    ki_ref[0] = jnp.dot(y, wki_ref[...], preferred_element_type=F32)
    qit = _dot_nt(wsel_ref[0:SEL_WI, :], y)
    wit = _dot_nt(wsel_ref[SEL_WI:SEL_END, :], y) * (HEADS ** -0.5)
    wit = wit * (IDX_DIM ** -0.5)
    for s in range(nsub):
        cols = slice(s * blk, (s + 1) * blk)
        qat_ref[0, s] = qat[:, cols]
        vat_ref[0, s] = vat[:, cols]
        wit_ref[0, s] = wit[0:HEADS, cols]
        for h in range(HEADS):
            qit_ref[0, s, :, h * blk:(h + 1) * blk] = qit[h * IDX_DIM:(h + 1) * IDX_DIM, cols]

    cqn = _rms_norm(nn(NN_CQ, NN_CKV), qg_ref[...]).astype(BF16)
    ckvn = _rms_norm(nn(NN_CKV, NN_PE), kvg_ref[...]).astype(BF16)
    qbt = _dot_nt(wuq_ref[...], cqn)
    ct = ct_ref[...]
    st = st_ref[...]
    half = B_ROPE // 2
    for h in range(HEADS):
        q = qbt[h * LANES:(h + 1) * LANES]
        x1, x2 = q[B_NOPE:B_NOPE + half], q[B_NOPE + half:B_NOPE + B_ROPE]
        rot = jnp.concatenate([jnp.zeros((B_NOPE, rows), F32), -x2, x1,
                               jnp.zeros((LANES - B_NOPE - B_ROPE, rows), F32)], axis=0)
        qh = (q * ct + rot * st).astype(BF16)
        for s in range(nsub):
            qbt_ref[0, s, h * LANES:(h + 1) * LANES, :] = qh[:, s * blk:(s + 1) * blk]
    kn = jnp.dot(ckvn, wk_ref[...], preferred_element_type=F32)
    kpe = nn(NN_PE, NN_PER) * cos_ref[...] + nn(NN_PER, NN_END) * sin_ref[...]
    for h in range(HEADS):
        cols = slice(h * LANES, (h + 1) * LANES)
        kb_ref[0, :, cols] = (kn[:, cols] + kpe).astype(BF16)
    vbt = _dot_nt(wv_ref[...], ckvn).astype(BF16)
    for s in range(nsub):
        vbt_ref[0, s] = vbt[:, s * blk:(s + 1) * blk]


_MAX_EXACT = REL_BUCKETS // 2
_BUCKET_STARTS = tuple(math.ceil(_MAX_EXACT * (REL_MAX_DIST / _MAX_EXACT) ** (k / (REL_BUCKETS - _MAX_EXACT)))
                       for k in range(1, REL_BUCKETS - _MAX_EXACT))


def _t5_bucket(d):
    d = jnp.maximum(d, 0)
    large = jnp.full(d.shape, _MAX_EXACT, I32)
    for start in _BUCKET_STARTS:
        large = large + jnp.where(d >= start, 1, 0)
    return jnp.where(d < _MAX_EXACT, d, large)


def _dsa_kernel(relb_ref, ka_ref, ki_ref, vat_ref, qat_ref, qit_ref, wit_ref, kam_ref, vatm_ref,
                oa_ref, keys_s, bias_s, metab_s, acc_s, qpad_s, s_scr, madd_s, hi_s, lo_s, lo2_s,
                *, topk, seq):
    b = pl.program_id(0)
    i = pl.program_id(1)
    nch = i + 1
    row = lax.broadcasted_iota(I32, (TQ, TQ), 0)
    pos_q = i * TQ + lax.broadcasted_iota(I32, (TQ, TQ), 1)

    @pl.when((b == 0) & (i == 0))
    def _():
        sub = lax.broadcasted_iota(I32, (8, TQ), 0)
        lane = lax.broadcasted_iota(I32, (8, TQ), 1)

        def fill(dst, n_rows, offset):
            def step(r8, carry):
                r0 = pl.multiple_of(r8 * 8, 8)
                bucket = _t5_bucket(offset + lane - (r0 + sub))
                for h in range(HEADS):
                    t = jnp.zeros((8, TQ), F32)
                    for bk in range(REL_BUCKETS):
                        t = jnp.where(bucket == bk, relb_ref[bk, h], t)
                    dst[h, pl.ds(r0, 8), :] = (t - relb_ref[REL_BUCKETS - 1, h]) * LOG2_E
                return carry
            lax.fori_loop(0, n_rows // 8, step, 0)

        for dl in range(2):
            fill(bias_s.at[:, dl], TQ, dl * TQ)
        fill(metab_s, META_ROWS, N_META)

    def idx_body(c, carry, diagonal):
        k0 = pl.multiple_of(c * TQ, TQ)
        rel = jnp.dot(ki_ref[0, pl.ds(k0, TQ), :], qit_ref[0, 0],
                      preferred_element_type=F32)
        s = jnp.zeros((TQ, TQ), F32)
        for h in range(HEADS):
            s = s + jnp.maximum(rel[:, h * TQ:(h + 1) * TQ], 0.0) * wit_ref[0, 0, h:h + 1, :]
        if diagonal:
            s = jnp.where(k0 + row <= pos_q, s, -jnp.inf)
        bits = lax.bitcast_convert_type(s, I32)
        key = jnp.where(bits < 0, INT_MIN - bits, bits)
        keys_s[pl.ds(k0, TQ), :] = key
        h0 = pl.multiple_of(c * HALF, HALF)
        ka_, kb_ = key[:HALF], key[HALF:]
        hi_s[pl.ds(h0, HALF), :] = ((ka_ >> 16) & 0xFFFF) | (kb_ & -0x10000)
        lo_s[pl.ds(h0, HALF), :] = ((ka_ & 0xFFFF) ^ 0x8000) | ((kb_ << 16) ^ INT_MIN)
        return carry

    def idx_four(j, carry):
        for u in range(4):
            idx_body(4 * j + u, carry, diagonal=False)
        return carry

    def idx_two(j, carry):
        idx_body(2 * j, carry, diagonal=False)
        return idx_body(2 * j + 1, carry, diagonal=False)

    lax.fori_loop(0, (nch - 1) // 4, idx_four, 0)
    lax.fori_loop((nch - 1) // 4 * 2, (nch - 1) // 2, idx_two, 0)
    lax.fori_loop((nch - 1) // 2 * 2, nch - 1, functools.partial(idx_body, diagonal=False), 0)
    idx_body(nch - 1, 0, diagonal=True)

    nscan = (nch + SCAN - 1) // SCAN

    def fill_lowest(c, carry):
        h0 = pl.multiple_of(c * HALF, HALF)
        lowest = jnp.full((HALF, TQ), LOWEST_I16_PAIR, I32)
        hi_s[pl.ds(h0, HALF), :] = lowest
        lo_s[pl.ds(h0, HALF), :] = lowest
        return carry

    lax.fori_loop(nch, nscan * SCAN, fill_lowest, 0)

    def count(pred):
        def body(c, acc):
            k0 = pl.multiple_of(c * TQ, TQ)
            hit = pred(keys_s[pl.ds(k0, TQ), :], k0 + row)
            return acc + jnp.sum(jnp.where(hit, 1, 0).reshape(TQ // 8, 8, TQ), axis=0)
        acc = lax.fori_loop(0, nch, body, jnp.zeros((8, TQ), I32))
        return jnp.sum(acc, axis=0, keepdims=True)

    def splat16(pat):
        word = pat | (pat << 16)
        return pltpu.bitcast(jnp.broadcast_to(word, (8, TQ)), I16)

    def count16(scr, pat):
        thr_p = splat16(pat)
        one, zero = jnp.int16(1), jnp.int16(0)

        def body(cs, acc):
            h0 = pl.multiple_of(cs * (SCAN * HALF), SCAN * HALF)
            k = pltpu.bitcast(scr[pl.ds(h0, SCAN * HALF), :], I16)
            parts = []
            for j in range(SCAN * TQ // 16):
                kj = k[16 * j:16 * (j + 1)]
                parts.append(jnp.where(kj >= thr_p, one, zero))
            while len(parts) > 1:
                parts = [x + y for x, y in zip(parts[::2], parts[1::2])] + parts[len(parts) & ~1:]
            return acc + parts[0]
        acc = lax.fori_loop(0, nscan, body, jnp.zeros((16, TQ), I16))
        return jnp.sum(acc.astype(I32), axis=0, keepdims=True)

    def hi_bit(it, carry):
        t_hi, cnt_t = carry
        cand = t_hi | jnp.left_shift(jnp.int32(1), 15 - it)
        cnt = count16(hi_s, cand ^ 0x8000)
        ok = cnt >= topk
        return jnp.where(ok, cand, t_hi), jnp.where(ok, cnt, cnt_t)

    zero_q = jnp.zeros((1, TQ), I32)
    t_hi, cnt_t = lax.fori_loop(0, 16, hi_bit, (zero_q, jnp.full((1, TQ), seq, I32)))
    thr_hi_p = splat16(t_hi ^ 0x8000)

    def lo_prep(c2, acc):
        h0 = pl.multiple_of(c2 * TQ, TQ)
        hi = pltpu.bitcast(hi_s[pl.ds(h0, TQ), :], I16)
        lo = pltpu.bitcast(lo_s[pl.ds(h0, TQ), :], I16)
        parts, above = [], []
        for j in range(2 * TQ // 16):
            hj = hi[16 * j:16 * (j + 1)]
            parts.append(jnp.where(hj == thr_hi_p, lo[16 * j:16 * (j + 1)], jnp.int16(-0x8000)))
            above.append(jnp.where(hj > thr_hi_p, jnp.int16(1), jnp.int16(0)))
        lo2_s[pl.ds(h0, TQ), :] = pltpu.bitcast(jnp.concatenate(parts, axis=0), I32)
        while len(above) > 1:
            above = [x + y for x, y in zip(above[::2], above[1::2])] + above[len(above) & ~1:]
        return acc + above[0].astype(I32)

    acc_gt = lax.fori_loop(0, nscan * (SCAN // 2), lo_prep, jnp.zeros((16, TQ), I32))
    cnt_gt = jnp.sum(acc_gt, axis=0, keepdims=True)

    def lo_bit(it, carry):
        t_lo, cnt_t = carry
        cand = t_lo | jnp.left_shift(jnp.int32(1), 15 - it)
        cnt = cnt_gt + count16(lo2_s, cand ^ 0x8000)
        ok = cnt >= topk
        return jnp.where(ok, cand, t_lo), jnp.where(ok, cnt, cnt_t)

    t_lo, cnt_t = lax.fori_loop(0, 16, lo_bit, (zero_q, cnt_t))
    thr = ((t_hi << 16) | t_lo) ^ INT_MIN

    tie_cut = (cnt_t > topk) & (thr > KEY_NEG_INF)
    any_cut = jnp.max(jnp.where(tie_cut, 1, 0))

    @pl.when(any_cut > 0)
    def _():
        need = topk - count(lambda k, p: k > thr)

        def jbit(it, j):
            cand = j | jnp.left_shift(jnp.int32(1), 12 - it)
            cnt = count(lambda k, p: (k == thr) & (p < cand))
            return jnp.where(cnt < need, cand, j)

        j = lax.fori_loop(0, 13, jbit, jnp.zeros((1, TQ), I32))
        j_last = jnp.where(tie_cut, j, seq)

        def demote(c, carry):
            k0 = pl.multiple_of(c * TQ, TQ)
            key = keys_s[pl.ds(k0, TQ), :]
            keys_s[pl.ds(k0, TQ), :] = jnp.where((key == thr) & (k0 + row > j_last), key - 1, key)
            return carry

        lax.fori_loop(0, nch, demote, 0)

    thr_sel = jnp.maximum(thr, KEY_NEG_INF + 1)

    for h in range(HEADS):
        qh = qat_ref[0, 0, h * A_HEAD_DIM:(h + 1) * A_HEAD_DIM, :]
        zero_half = jnp.zeros_like(qh)
        qpad_s[h] = jnp.concatenate([zero_half, qh] if h % 2 else [qh, zero_half], axis=0)

    def set_mask(c):
        k0 = pl.multiple_of(c * TQ, TQ)
        madd_s[...] = jnp.where(keys_s[pl.ds(k0, TQ), :] >= thr_sel, 0.0, MASKED)

    def scores(c, h, near):
        k0 = pl.multiple_of(c * TQ, TQ)
        pair = slice((h // 2) * LANES, (h // 2 + 1) * LANES)
        s = jnp.dot(ka_ref[0, pl.ds(k0, TQ), pair], qpad_s[h],
                    preferred_element_type=F32) + madd_s[...]
        if near:
            dl = i - c
            s = s + jnp.where(dl == 0, bias_s[h, 0], jnp.where(dl == 1, bias_s[h, 1], 0.0))
        s_scr[h] = s
        return jnp.max(s, axis=0, keepdims=True)

    def att_body(c, carry, lookahead):
        ms, cms = carry
        if lookahead is not None:
            set_mask(c + 1)
        new_ms, new_cms = [], []
        for h in range(HEADS):
            m_new = jnp.maximum(ms[h], cms[h])
            alpha = jnp.exp2(ms[h] - m_new)
            p = jnp.exp2((s_scr[h] - m_new).astype(BF16))
            new_ms.append(m_new)
            rows = slice(h * V_EXT, (h + 1) * V_EXT)
            pv = jnp.dot(vat_ref[0, c, rows, :], p, preferred_element_type=F32)
            acc_s[rows, :] = alpha * acc_s[rows, :] + pv
            if lookahead is not None:
                new_cms.append(scores(c + 1, h, near=lookahead))
        return tuple(new_ms), tuple(new_cms)

    meta_ok = lax.broadcasted_iota(I32, (META_ROWS, TQ), 0) < N_META
    ms0 = []
    for h in range(HEADS):
        pair = slice((h // 2) * LANES, (h // 2 + 1) * LANES)
        s = jnp.dot(kam_ref[:, pair], qpad_s[h], preferred_element_type=F32)
        s = jnp.where(meta_ok, s + jnp.where(i == 0, metab_s[h], 0.0), MASKED)
        m0 = jnp.max(s, axis=0, keepdims=True)
        p = jnp.exp2((s - m0).astype(BF16))
        ms0.append(m0)
        rows = slice(h * V_EXT, (h + 1) * V_EXT)
        acc_s[rows, :] = jnp.dot(vatm_ref[rows, :], p, preferred_element_type=F32)

    set_mask(0)
    init = (tuple(ms0), tuple(scores(0, h, near=True) for h in range(HEADS)))
    n_far = jnp.maximum(nch - 3, 0)

    def four_steps(j, carry):
        for u in range(4):
            carry = att_body(4 * j + u, carry, lookahead=False)
        return carry

    def two_steps(j, carry):
        return att_body(2 * j + 1, att_body(2 * j, carry, lookahead=False), lookahead=False)

    carry = lax.fori_loop(0, n_far // 4, four_steps, init)
    carry = lax.fori_loop(n_far // 4 * 2, n_far // 2, two_steps, carry)
    carry = lax.fori_loop(n_far // 2 * 2, n_far, functools.partial(att_body, lookahead=False), carry)
    carry = lax.fori_loop(n_far, nch - 1, functools.partial(att_body, lookahead=True), carry)
    att_body(nch - 1, carry, lookahead=None)

    outs = []
    for h in range(HEADS):
        num = acc_s[h * V_EXT:h * V_EXT + A_HEAD_DIM, :]
        den = acc_s[h * V_EXT + L_ROW:h * V_EXT + L_ROW + 1, :]
        outs.append(num * (1.0 / den))
    oa_ref[0] = jnp.concatenate(outs, axis=0).T


def _silu(x):
    return x * (1.0 / (1.0 + jnp.exp(-x)))


def _mla_out_kernel(kb_ref, vbt_ref, qbt_ref, kbm_ref, vbtm_ref, x_ref, lng_ref, lnb_ref, oa_ref,
                    g_ref, wout_ref, pg_ref, pb_ref, out_ref, s_scr, acc_s):
    i = pl.program_id(1)
    nch = i + 1
    row = lax.broadcasted_iota(I32, (TQ, TQ), 0)
    pos_q = i * TQ + lax.broadcasted_iota(I32, (TQ, TQ), 1)

    def scores(c, h, diagonal):
        k0 = pl.multiple_of(c * TQ, TQ)
        grp = slice(h * LANES, (h + 1) * LANES)
        s = jnp.dot(kb_ref[0, pl.ds(k0, TQ), grp], qbt_ref[0, 0, grp, :],
                    preferred_element_type=F32)
        if diagonal:
            s = jnp.where(k0 + row <= pos_q, s, MASKED)
        s_scr[h] = s
        return jnp.max(s, axis=0, keepdims=True)

    def body(c, carry, lookahead):
        ms, ls, cms = carry
        new_ms, new_ls, new_cms = [], [], []
        for h in range(HEADS):
            m_new = jnp.maximum(ms[h], cms[h])
            alpha = jnp.exp2(ms[h] - m_new)
            p = jnp.exp2(s_scr[h] - m_new)
            new_ls.append(alpha * ls[h] + jnp.sum(p, axis=0, keepdims=True))
            new_ms.append(m_new)
            rows = slice(h * B_V, (h + 1) * B_V)
            pv = jnp.dot(vbt_ref[0, c, rows, :], p.astype(BF16), preferred_element_type=F32)
            acc_s[rows, :] = alpha * acc_s[rows, :] + pv
            if lookahead is not None:
                new_cms.append(scores(c + 1, h, diagonal=lookahead))
        return tuple(new_ms), tuple(new_ls), tuple(new_cms)

    meta_ok = lax.broadcasted_iota(I32, (META_ROWS, TQ), 0) < N_META
    ms0, ls0 = [], []
    for h in range(HEADS):
        grp = slice(h * LANES, (h + 1) * LANES)
        s = jnp.dot(kbm_ref[:, grp], qbt_ref[0, 0, grp, :], preferred_element_type=F32)
        s = jnp.where(meta_ok, s, MASKED)
        m0 = jnp.max(s, axis=0, keepdims=True)
        p = jnp.exp2(s - m0)
        ms0.append(m0)
        ls0.append(jnp.sum(p, axis=0, keepdims=True))
        rows = slice(h * B_V, (h + 1) * B_V)
        acc_s[rows, :] = jnp.dot(vbtm_ref[rows, :], p.astype(BF16), preferred_element_type=F32)

    cms0 = tuple(scores(0, h, diagonal=True) for h in range(HEADS))
    init = (tuple(ms0), tuple(ls0), cms0)
    n_full = jnp.maximum(nch - 2, 0)

    def four_steps(j, carry):
        for u in range(4):
            carry = body(4 * j + u, carry, lookahead=False)
        return carry

    def two_steps(j, carry):
        return body(2 * j + 1, body(2 * j, carry, lookahead=False), lookahead=False)

    carry = lax.fori_loop(0, n_full // 4, four_steps, init)
    carry = lax.fori_loop(n_full // 4 * 2, n_full // 2, two_steps, carry)
    carry = lax.fori_loop(n_full // 2 * 2, n_full, functools.partial(body, lookahead=False), carry)
    carry = lax.fori_loop(n_full, nch - 1, functools.partial(body, lookahead=True), carry)
    ms, ls, _ = body(nch - 1, carry, lookahead=None)

    outs = []
    for h in range(HEADS):
        rows = slice(h * B_V, (h + 1) * B_V)
        outs.append(acc_s[rows, :] * (1.0 / ls[h]))
    ob = jnp.concatenate(outs, axis=0).T

    h = _layer_norm(x_ref[0], lng_ref[...], lnb_ref[...])
    g = g_ref[0]
    mixed = jnp.concatenate([oa_ref[0] * _silu(g[:, :A_WIDTH]), ob * _silu(g[:, A_WIDTH:])], axis=-1)
    out = jnp.dot(mixed.astype(BF16), wout_ref[...], preferred_element_type=F32)
    out_ref[0] = _layer_norm(ALPHA * h + out, pg_ref[...], pb_ref[...])


def _rot_cols(w):
    half = w.shape[-1] // 2
    return jnp.concatenate([-w[:, half:], w[:, :half]], axis=-1)


def _full(shape):
    return pl.BlockSpec(shape, lambda *_: (0,) * len(shape))


def _rope_tables(n, scale):
    d_q = B_NOPE + B_ROPE
    inv_freq = ROPE_THETA ** (-jnp.arange(0, B_ROPE, 2, dtype=F32) / B_ROPE)
    ang = jnp.arange(n, dtype=F32)[:, None] * inv_freq[None, :]
    pad = ((0, 0), (B_NOPE, LANES - d_q))
    cos128 = jnp.pad(jnp.tile(jnp.cos(ang), (1, 2)), pad)
    sin128 = jnp.pad(jnp.tile(jnp.sin(ang), (1, 2)), pad)
    nope = (lax.broadcasted_iota(I32, (1, LANES), 1) < B_NOPE).astype(F32)
    return cos128, sin128, ((cos128 + nope) * scale).T, (sin128 * scale).T


def kernel(x, meta_tokens, ln_emb_g, ln_emb_b, w_in, w_uq, q_norm_g, w_ukv, kv_norm_g, rel_bias,
           w_out, ln_post_g, ln_post_b):
    B, S, D = x.shape
    assert w_in.shape[0] == DEPTH == 1
    assert S % TQ == 0 and S < (1 << 13) and meta_tokens.shape[0] == N_META <= META_ROWS
    nq = S // TQ
    assert nq % SCAN == 0 and SCAN % 2 == 0
    topk = min(TOPK_MAX, S // 4) - N_META
    assert 0 < topk <= TQ

    w = w_in[0]
    d_q = B_NOPE + B_ROPE
    d_kv = B_NOPE + B_V

    def per_head(m, width):
        m = m.reshape(m.shape[0], HEADS, -1)
        return jnp.pad(m, ((0, 0), (0, 0), (0, width - m.shape[2]))).reshape(m.shape[0], -1)

    w_kpe = w[:, O_KPE:O_GB]
    pe_lanes = ((0, 0), (B_NOPE, LANES - d_q))
    wnn = jnp.concatenate([w[:, O_KA:O_VA], w[:, O_GA:O_QI], w[:, O_GB:O_END], w[:, O_CQ:O_KPE],
                           jnp.pad(w_kpe, pe_lanes), jnp.pad(_rot_cols(w_kpe), pe_lanes)],
                          axis=1).astype(BF16)
    assert wnn.shape[1] == NN_END
    wnt = jnp.concatenate([w[:, O_QA:O_KA], per_head(w[:, O_VA:O_GA], V_EXT)], axis=1).T.astype(BF16)
    assert wnt.shape[0] == NT_END
    wki = w[:, O_KI:O_WI].astype(F32)
    wsel = jnp.concatenate([w[:, O_QI:O_KI], jnp.pad(w[:, O_WI:O_CQ], ((0, 0), (0, 8)))],
                           axis=1).T.astype(F32)
    assert wsel.shape[0] == SEL_END
    wuq_t = per_head(w_uq[0], LANES).T.astype(BF16)
    wkv = w_ukv[0].reshape(KV_LORA, HEADS, d_kv)
    wk = per_head(wkv[:, :, :B_NOPE].reshape(KV_LORA, -1), LANES).astype(BF16)
    wv_t = wkv[:, :, B_NOPE:].reshape(KV_LORA, -1).T.astype(BF16)

    assert N_META + S >= META_PROJ_ROWS
    tabs = _rope_tables(N_META + S, d_q ** -0.5 * LOG2_E)
    tabs_x = (tabs[0][N_META:], tabs[1][N_META:], tabs[2][:, N_META:], tabs[3][:, N_META:])
    tabs_m = (tabs[0][:META_PROJ_ROWS], tabs[1][:META_PROJ_ROWS],
              tabs[2][:, :META_PROJ_ROWS], tabs[3][:, :META_PROJ_ROWS])

    row2 = lambda v: v.reshape(1, -1).astype(F32)
    meta_pad = jnp.pad(meta_tokens.astype(x.dtype), ((0, META_PROJ_ROWS - N_META), (0, 0)))[None]

    params = pltpu.CompilerParams(dimension_semantics=("arbitrary", "arbitrary"),
                                  vmem_limit_bytes=VMEM_LIMIT)
    sds = jax.ShapeDtypeStruct

    def project(rows_in, tabs, nb, nblk, blk, nsub, n_valid):
        rows = nsub * blk
        row_blk = lambda n: pl.BlockSpec((1, rows, n), lambda b, i: (b, i, 0))
        t_blk = lambda r, n: pl.BlockSpec((1, nsub, r, n), lambda b, i: (b, i, 0, 0))
        return pl.pallas_call(
            functools.partial(_proj_kernel, n_valid=n_valid),
            grid=(nb, nblk // nsub),
            in_specs=[row_blk(D), _full((1, D)), _full((1, D)), _full((D, NN_END)),
                      _full((NT_END, D)), _full((D, IDX_DIM)), _full((SEL_END, D)),
                      _full((HEADS * LANES, Q_LORA)), _full((KV_LORA, HEADS * LANES)),
                      _full((B_WIDTH, KV_LORA)), _full((1, Q_LORA)), _full((1, KV_LORA)),
                      pl.BlockSpec((rows, LANES), lambda b, i: (i, 0)),
                      pl.BlockSpec((rows, LANES), lambda b, i: (i, 0)),
                      pl.BlockSpec((LANES, rows), lambda b, i: (0, i)),
                      pl.BlockSpec((LANES, rows), lambda b, i: (0, i))],
            out_specs=[row_blk(A_WIDTH), row_blk(IDX_DIM), row_blk(A_WIDTH + B_WIDTH),
                       row_blk(HEADS * LANES), t_blk(A_WIDTH, blk),
                       t_blk(IDX_DIM, HEADS * blk), t_blk(HEADS, blk), t_blk(VT_WIDTH, blk),
                       t_blk(HEADS * LANES, blk), t_blk(B_WIDTH, blk)],
            out_shape=[sds((nb, nblk * blk, A_WIDTH), BF16), sds((nb, nblk * blk, IDX_DIM), F32),
                       sds((nb, nblk * blk, A_WIDTH + B_WIDTH), F32),
                       sds((nb, nblk * blk, HEADS * LANES), BF16),
                       sds((nb, nblk, A_WIDTH, blk), BF16),
                       sds((nb, nblk, IDX_DIM, HEADS * blk), F32),
                       sds((nb, nblk, HEADS, blk), F32), sds((nb, nblk, VT_WIDTH, blk), BF16),
                       sds((nb, nblk, HEADS * LANES, blk), BF16),
                       sds((nb, nblk, B_WIDTH, blk), BF16)],
            compiler_params=params, name="proj",
        )(rows_in, row2(ln_emb_g), row2(ln_emb_b), wnn, wnt, wki, wsel, wuq_t, wk, wv_t,
          row2(q_norm_g[0]), row2(kv_norm_g[0]), *tabs)

    ka, ki, gates, kb, qat, qit, wit, vat, qbt, vbt = project(x, tabs_x, B, nq, TQ, PROJ_BLOCKS, S)
    ka_m, _, _, kb_m, _, _, _, vat_m, _, vbt_m = project(meta_pad, tabs_m, 1, 1, META_PROJ_ROWS, 1,
                                                         N_META)
    ka_m, kb_m = ka_m[0, :META_ROWS], kb_m[0, :META_ROWS]
    vat_m, vbt_m = vat_m[0, 0, :, :META_ROWS], vbt_m[0, 0, :, :META_ROWS]

    batch_rows = lambda n: pl.BlockSpec((1, S, n), lambda b, i: (b, 0, 0))
    batch_t = lambda r: pl.BlockSpec((1, nq, r, TQ), lambda b, i: (b, 0, 0, 0))
    blk = lambda r, n: pl.BlockSpec((1, 1, r, n), lambda b, i: (b, i, 0, 0))
    out_blk = pl.BlockSpec((1, TQ, A_WIDTH), lambda b, i: (b, i, 0))
    oa = pl.pallas_call(
        functools.partial(_dsa_kernel, topk=topk, seq=S),
        grid=(B, nq),
        in_specs=[pl.BlockSpec(memory_space=pltpu.SMEM), batch_rows(A_WIDTH), batch_rows(IDX_DIM),
                  batch_t(VT_WIDTH), blk(A_WIDTH, TQ), blk(IDX_DIM, HEADS * TQ), blk(HEADS, TQ),
                  _full((META_ROWS, A_WIDTH)), _full((VT_WIDTH, META_ROWS))],
        out_specs=out_blk,
        out_shape=sds((B, S, A_WIDTH), F32),
        scratch_shapes=[pltpu.VMEM((S, TQ), I32),
                        pltpu.VMEM((HEADS, 2, TQ, TQ), F32),
                        pltpu.VMEM((HEADS, META_ROWS, TQ), F32),
                        pltpu.VMEM((VT_WIDTH, TQ), F32),
                        pltpu.VMEM((HEADS, LANES, TQ), BF16),
                        pltpu.VMEM((HEADS, TQ, TQ), F32),
                        pltpu.VMEM((TQ, TQ), F32),
                        pltpu.VMEM((S // 2, TQ), I32),
                        pltpu.VMEM((S // 2, TQ), I32),
                        pltpu.VMEM((S // 2, TQ), I32)],
        compiler_params=params, name="dsa",
    )(rel_bias.astype(F32), ka, ki, vat, qat, qit, wit, ka_m, vat_m)

    row_blk = lambda n: pl.BlockSpec((1, TQ, n), lambda b, i: (b, i, 0))
    return pl.pallas_call(
        _mla_out_kernel,
        grid=(B, nq),
        in_specs=[batch_rows(HEADS * LANES), batch_t(B_WIDTH), blk(HEADS * LANES, TQ),
                  _full((META_ROWS, HEADS * LANES)), _full((B_WIDTH, META_ROWS)),
                  row_blk(D), _full((1, D)), _full((1, D)), row_blk(A_WIDTH),
                  row_blk(A_WIDTH + B_WIDTH), _full((A_WIDTH + B_WIDTH, D)), _full((1, D)),
                  _full((1, D))],
        out_specs=row_blk(D),
        out_shape=sds((B, S, D), x.dtype),
        scratch_shapes=[pltpu.VMEM((HEADS, TQ, TQ), F32),
                        pltpu.VMEM((B_WIDTH, TQ), F32)],
        compiler_params=params, name="mla_out",
    )(kb, vbt, qbt, kb_m, vbt_m, x, row2(ln_emb_g), row2(ln_emb_b), oa, gates,
      w_out[0].astype(BF16), row2(ln_post_g[0]), row2(ln_post_b[0]))
```
